```python
import math
import jax, jax.numpy as jnp
from jax import lax
import numpy as np

D_MODEL = 1024
BATCH = 8
SEQ = 2048
DEPTH = 2
DEC_BATCH = 2
DEC_SEQ = 16384
PAST_LEN = 128

GRID_W = 64
N_BRANCH = 4
BRANCH_W = D_MODEL // 2
QBLK = 128
NORM_EPS = 1e-6

MLSTM_HEADS = 4
MLSTM_HD = BRANCH_W // MLSTM_HEADS
MLSTM_CHUNK = 128
GQA_HD = 64
GQA_HEADS = BRANCH_W // GQA_HD
GQA_KV_HEADS = 2
ROPE_THETA = 10000.0
SSD_HD = 64
SSD_HEADS = BRANCH_W // SSD_HD
SSD_GROUPS = 2
SSD_STATE = 128
SSD_CONV = 5
SSD_CHUNK = 128
DIFF_HD = 64
DIFF_HEADS = BRANCH_W // (2 * DIFF_HD)
MOE_GROUPS = 4
MOE_EPG = 8
MOE_EXPERTS = MOE_GROUPS * MOE_EPG
MOE_TOPK = 2
MOE_FF = D_MODEL // 2
MOE_BLK = 128

MLSTM_COLS = 4 * BRANCH_W + 4 * MLSTM_HEADS
GQA_COLS = GQA_HEADS * GQA_HD + 2 * GQA_KV_HEADS * GQA_HD
SSD_XBC = BRANCH_W + 2 * SSD_GROUPS * SSD_STATE
SSD_COLS = BRANCH_W + SSD_XBC + 2 * SSD_HEADS
DIFF_COLS = 3 * BRANCH_W
IN_COLS = MLSTM_COLS + GQA_COLS + SSD_COLS + DIFF_COLS

kernel_name = "hybrid_gated_branch_encoder"


def rms_norm(x, g):
    xf = x.astype(jnp.float32)
    y = xf * lax.rsqrt(jnp.mean(xf * xf, axis=-1, keepdims=True) + NORM_EPS)
    return (y * g.astype(jnp.float32)).astype(x.dtype)


def mlstm_chunkwise(q, k, v, i_pre, f_pre):
    Bsz, H, N, d = q.shape
    L = MLSTM_CHUNK
    nc = N // L
    k = k * (d ** -0.5)
    logf = jax.nn.log_sigmoid(f_pre)

    def to_chunks(a):
        return jnp.moveaxis(a.reshape(a.shape[:2] + (nc, L) + a.shape[3:]), 2, 0)

    qc, kc, vc, ic, fc = to_chunks(q), to_chunks(k), to_chunks(v), to_chunks(i_pre), to_chunks(logf)
    causal = jnp.tril(jnp.ones((L, L), dtype=bool))

    def step(carry, inp):
        C, n, m = carry
        qb, kb, vb, ib, fb = inp
        bcum = jnp.cumsum(fb, axis=-1)
        dmat = jnp.where(causal, bcum[..., :, None] - bcum[..., None, :] + ib[..., None, :], -jnp.inf)
        inter = bcum + m[..., None]
        m_t = jnp.maximum(inter, jnp.max(dmat, axis=-1))
        w_intra = jnp.exp(dmat - m_t[..., None])
        w_inter = jnp.exp(inter - m_t)
        s = jnp.einsum('bhtd,bhsd->bhts', qb, kb) * w_intra
        num = jnp.einsum('bhts,bhsd->bhtd', s, vb) + w_inter[..., None] * jnp.einsum('bhvk,bhtk->bhtv', C, qb)
        den = jnp.sum(s, axis=-1) + w_inter * jnp.einsum('bhk,bhtk->bht', n, qb)
        h = num / jnp.maximum(jnp.abs(den), jnp.exp(-m_t))[..., None]
        btot = bcum[..., -1]
        dec = btot[..., None] - bcum + ib
        m_new = jnp.maximum(btot + m, jnp.max(dec, axis=-1))
        ws = jnp.exp(dec - m_new[..., None])
        wc = jnp.exp(btot + m - m_new)
        C = wc[..., None, None] * C + jnp.einsum('bhs,bhsv,bhsk->bhvk', ws, vb, kb)
        n = wc[..., None] * n + jnp.einsum('bhs,bhsk->bhk', ws, kb)
        return (C, n, m_new), h

    init = (jnp.zeros((Bsz, H, d, d), q.dtype), jnp.zeros((Bsz, H, d), q.dtype), jnp.zeros((Bsz, H), q.dtype))
    _, hs = lax.scan(step, init, (qc, kc, vc, ic, fc))
    return jnp.moveaxis(hs, 0, 2).reshape(Bsz, H, N, d)


def mlstm_branch(p, gate_bias, norm_g):
    Bsz, N, _ = p.shape
    W = BRANCH_W

    def heads(a):
        return a.reshape(Bsz, N, MLSTM_HEADS, MLSTM_HD).transpose(0, 2, 1, 3)

    q, k, v = heads(p[..., :W]), heads(p[..., W:2 * W]), heads(p[..., 2 * W:3 * W])
    o = p[..., 3 * W:4 * W]
    g = (p[..., 4 * W:] + gate_bias).reshape(Bsz, N, 4, MLSTM_HEADS).transpose(2, 0, 3, 1)
    hf = mlstm_chunkwise(q, k, v, g[0], g[1])
    rev = lambda a: jnp.flip(a, axis=2)
    hb = rev(mlstm_chunkwise(rev(q), rev(k), rev(v), rev(g[2]), rev(g[3])))
    h = rms_norm(hf + hb, norm_g.reshape(MLSTM_HEADS, 1, MLSTM_HD))
    h = h.transpose(0, 2, 1, 3).reshape(Bsz, N, W)
    return jax.nn.sigmoid(o) * h


def axial_rope_angles(n):
    rows = n // GRID_W
    row = jnp.repeat(jnp.arange(rows, dtype=jnp.float32), GRID_W)
    col = (jnp.arange(n) % GRID_W).astype(jnp.float32)
    half = GQA_HD // 2
    inv = 1.0 / (ROPE_THETA ** (jnp.arange(0, half, 2, dtype=jnp.float32) / half))
    ang = jnp.concatenate([row[:, None] * inv, col[:, None] * inv], axis=-1)
    return jnp.cos(ang), jnp.sin(ang)


def apply_rope(x, cos, sin):
    xp = x.reshape(x.shape[:-1] + (x.shape[-1] // 2, 2))
    x0, x1 = xp[..., 0], xp[..., 1]
    c, s = cos[None, :, None, :], sin[None, :, None, :]
    return jnp.stack([x0 * c - x1 * s, x0 * s + x1 * c], axis=-1).reshape(x.shape)


def gqa_branch(p, q_norm, k_norm, cos, sin):
    Bsz, N, _ = p.shape
    nq, nkv = GQA_HEADS * GQA_HD, GQA_KV_HEADS * GQA_HD
    q = p[..., :nq].reshape(Bsz, N, GQA_HEADS, GQA_HD)
    k = p[..., nq:nq + nkv].reshape(Bsz, N, GQA_KV_HEADS, GQA_HD)
    v = p[..., nq + nkv:].reshape(Bsz, N, GQA_KV_HEADS, GQA_HD)
    q = apply_rope(rms_norm(q, q_norm), cos, sin) * (GQA_HD ** -0.5)
    k = apply_rope(rms_norm(k, k_norm), cos, sin)
    grp = GQA_HEADS // GQA_KV_HEADS
    nb = N // QBLK
    qb = q.reshape(Bsz, nb, QBLK, GQA_KV_HEADS, grp, GQA_HD).swapaxes(0, 1)

    def block(qblk):
        s = jnp.einsum('bqhgd,bkhd->bhgqk', qblk, k)
        pr = jax.nn.softmax(s, axis=-1)
        return jnp.einsum('bhgqk,bkhd->bqhgd', pr, v)

    o = lax.map(block, qb)
    return o.swapaxes(0, 1).reshape(Bsz, N, BRANCH_W)


def ssd_chunked(xdt, a, bm, cm):
    Bsz, N, H, P = xdt.shape
    G, S = bm.shape[2], bm.shape[3]
    HG = H // G
    L = SSD_CHUNK
    nc = N // L
    x = xdt.reshape(Bsz, nc, L, G, HG, P)
    bc = bm.reshape(Bsz, nc, L, G, S)
    cc = cm.reshape(Bsz, nc, L, G, S)
    acs = jnp.cumsum(a.reshape(Bsz, nc, L, G, HG).transpose(0, 1, 3, 4, 2), axis=-1)
    causal = jnp.tril(jnp.ones((L, L), dtype=bool))
    decay_ts = jnp.exp(jnp.where(causal, acs[..., :, None] - acs[..., None, :], -jnp.inf))
    cb = jnp.einsum('bctgn,bcsgn->bcgts', cc, bc)
    y_diag = jnp.einsum('bcghts,bcsghp->bctghp', cb[:, :, :, None] * decay_ts, x)
    decay_end = jnp.exp(acs[..., -1:] - acs)
    states = jnp.einsum('bcsgn,bcghs,bcsghp->bcghpn', bc, decay_end, x)
    chunk_decay = jnp.exp(acs[..., -1])

    def carry_state(h, inp):
        st, dec = inp
        return dec[..., None, None] * h + st, h

    h0 = jnp.zeros((Bsz, G, HG, P, S), xdt.dtype)
    _, h_in = lax.scan(carry_state, h0, (jnp.moveaxis(states, 1, 0), jnp.moveaxis(chunk_decay, 1, 0)))
    h_in = jnp.moveaxis(h_in, 0, 1)
    y_off = jnp.einsum('bctgn,bcghpn,bcght->bctghp', cc, h_in, jnp.exp(acs))
    return (y_diag + y_off).reshape(Bsz, N, H, P)


def ssd_branch(p, conv_w, conv_b, dt_bias, a_log, d_skip, norm_g):
    Bsz, N, _ = p.shape
    W = BRANCH_W
    GS = SSD_GROUPS * SSD_STATE
    z = p[..., :W]
    xbc = p[..., W:W + SSD_XBC]
    dt_raw = p[..., W + SSD_XBC:].reshape(Bsz, N, 2, SSD_HEADS)
    xbc = lax.conv_general_dilated(
        xbc, conv_w.astype(xbc.dtype)[:, None, :], window_strides=(1,),
        padding=((SSD_CONV // 2, SSD_CONV // 2),), dimension_numbers=('NWC', 'WIO', 'NWC'),
        feature_group_count=SSD_XBC)
    xbc = jax.nn.silu(xbc + conv_b)
    x = xbc[..., :W].reshape(Bsz, N, SSD_HEADS, SSD_HD)
    bm = xbc[..., W:W + GS].reshape(Bsz, N, SSD_GROUPS, SSD_STATE)
    cm = xbc[..., W + GS:].reshape(Bsz, N, SSD_GROUPS, SSD_STATE)
    dt = jax.nn.softplus(dt_raw + dt_bias)
    A = -jnp.exp(a_log)
    yf = ssd_chunked(x * dt[:, :, 0, :, None], dt[:, :, 0] * A[0], bm, cm)
    rev = lambda a: jnp.flip(a, axis=1)
    yb = rev(ssd_chunked(rev(x * dt[:, :, 1, :, None]), rev(dt[:, :, 1] * A[1]), rev(bm), rev(cm)))
    y = yf + yb + d_skip[:, None] * x
    y = y.reshape(Bsz, N, W) * jax.nn.silu(z)
    y = rms_norm(y.reshape(Bsz, N, SSD_GROUPS, W // SSD_GROUPS), norm_g.reshape(SSD_GROUPS, W // SSD_GROUPS))
    return y.reshape(Bsz, N, W)


def diff_branch(p, lam_params, norm_g, lambda_init):
    Bsz, N, _ = p.shape
    W = BRANCH_W
    q = p[..., :W].reshape(Bsz, N, DIFF_HEADS, 2, DIFF_HD) * (DIFF_HD ** -0.5)
    k = p[..., W:2 * W].reshape(Bsz, N, DIFF_HEADS, 2, DIFF_HD)
    v = p[..., 2 * W:].reshape(Bsz, N, DIFF_HEADS, 2 * DIFF_HD)
    lp = lam_params.astype(jnp.float32)
    lam = jnp.exp(jnp.sum(lp[0] * lp[1])) - jnp.exp(jnp.sum(lp[2] * lp[3])) + lambda_init
    slopes = 2.0 ** (-8.0 * jnp.arange(1, DIFF_HEADS + 1, dtype=jnp.float32) / DIFF_HEADS)
    kpos = jnp.arange(N, dtype=jnp.float32)
    nb = N // QBLK
    qb = q.reshape(Bsz, nb, QBLK, DIFF_HEADS, 2, DIFF_HD).swapaxes(0, 1)
    starts = jnp.arange(nb, dtype=jnp.float32) * QBLK

    def block(args):
        qblk, t0 = args
        qpos = t0 + jnp.arange(QBLK, dtype=jnp.float32)
        bias = -slopes[:, None, None] * jnp.abs(qpos[:, None] - kpos[None, :])
        s = jnp.einsum('bqhcd,bkhcd->bhcqk', qblk, k) + bias[None, :, None]
        pr = jax.nn.softmax(s, axis=-1)
        a = pr[:, :, 0] - lam * pr[:, :, 1]
        return jnp.einsum('bhqk,bkhe->bqhe', a, v)

    o = lax.map(block, (qb, starts)).swapaxes(0, 1).reshape(Bsz, N, DIFF_HEADS, 2 * DIFF_HD)
    o = rms_norm(o, norm_g.reshape(DIFF_HEADS, 2 * DIFF_HD)) * (1.0 - lambda_init)
    return o.reshape(Bsz, N, W)


def hier_moe(h, w_rg, w_re, w1, w3, w2):
    Bsz, N, D = h.shape
    T = Bsz * N
    ht = h.reshape(T, D)
    g_prob = jax.nn.softmax((ht @ w_rg).astype(jnp.float32), axis=-1)
    g_w, g_idx = lax.top_k(g_prob, 1)
    e_logits = (ht @ w_re).astype(jnp.float32).reshape(T, MOE_GROUPS, MOE_EPG)
    e_sel = jnp.take_along_axis(e_logits, g_idx[:, :, None], axis=1)[:, 0]
    e_w, e_loc = lax.top_k(jax.nn.softmax(e_sel, axis=-1), MOE_TOPK)
    e_w = e_w / jnp.sum(e_w, axis=-1, keepdims=True) * g_w
    e_idx = g_idx * MOE_EPG + e_loc
    flat_e = e_idx.reshape(-1)
    flat_w = e_w.reshape(-1)
    flat_tok = jnp.repeat(jnp.arange(T), MOE_TOPK)
    order = jnp.argsort(flat_e)
    se, stok, sw = flat_e[order], flat_tok[order], flat_w[order]
    counts = jnp.bincount(flat_e, length=MOE_EXPERTS)
    starts = jnp.cumsum(counts) - counts
    pcounts = (counts + MOE_BLK - 1) // MOE_BLK * MOE_BLK
    pends = jnp.cumsum(pcounts)
    pstarts = pends - pcounts
    dest = pstarts[se] + (jnp.arange(T * MOE_TOPK) - starts[se])
    R = T * MOE_TOPK + MOE_EXPERTS * MOE_BLK
    nblk = R // MOE_BLK
    xs = jnp.zeros((R, D), ht.dtype).at[dest].set(ht[stok])
    blk_e = jnp.minimum(jnp.searchsorted(pends, jnp.arange(nblk) * MOE_BLK, side='right'), MOE_EXPERTS - 1)

    def expert_block(args):
        xb, e = args
        return (jax.nn.silu(xb @ w1[e]) * (xb @ w3[e])) @ w2[e]

    ys = lax.map(expert_block, (xs.reshape(nblk, MOE_BLK, D), blk_e)).reshape(R, D)
    out = jnp.zeros((T, D), jnp.float32).at[stok].add(ys[dest].astype(jnp.float32) * sw[:, None])
    return out.astype(h.dtype).reshape(Bsz, N, D)


def setup_inputs(seed: int = 0) -> dict:
    key = jax.random.key(seed)
    ks = iter(jax.random.split(key, 40))
    nrm = lambda shape, scale: jax.random.normal(next(ks), shape, jnp.float32) * scale
    gain = lambda shape: 1.0 + nrm(shape, 0.02)
    D = D_MODEL
    gate_bias = jnp.concatenate([
        nrm((DEPTH, MLSTM_HEADS), 0.1), 3.0 + nrm((DEPTH, MLSTM_HEADS), 0.5),
        nrm((DEPTH, MLSTM_HEADS), 0.1), 3.0 + nrm((DEPTH, MLSTM_HEADS), 0.5)], axis=-1)
    dt0 = jnp.exp(jax.random.uniform(next(ks), (DEPTH, 2, SSD_HEADS), jnp.float32, math.log(1e-3), math.log(1e-1)))
    dt_bias = dt0 + jnp.log(-jnp.expm1(-dt0))
    a_log = jnp.log(jax.random.uniform(next(ks), (DEPTH, 2, SSD_HEADS), jnp.float32, 1.0, 16.0))
    return {
        "x_prompt": nrm((BATCH, SEQ, D), 1.0),
        "x_sample": nrm((DEC_BATCH, DEC_SEQ, D), 1.0),
        "norm_mix": gain((DEPTH, D)),
        "w_in": nrm((DEPTH, D, IN_COLS), D ** -0.5),
        "mlstm_gate_bias": gate_bias,
        "mlstm_norm": gain((DEPTH, BRANCH_W)),
        "gqa_q_norm": gain((DEPTH, GQA_HD)),
        "gqa_k_norm": gain((DEPTH, GQA_HD)),
        "ssd_conv_w": nrm((DEPTH, SSD_CONV, SSD_XBC), SSD_CONV ** -0.5),
        "ssd_conv_b": nrm((DEPTH, SSD_XBC), 0.02),
        "ssd_dt_bias": dt_bias,
        "ssd_a_log": a_log,
        "ssd_d": gain((DEPTH, SSD_HEADS)),
        "ssd_norm": gain((DEPTH, BRANCH_W)),
        "diff_lambda": nrm((DEPTH, 4, DIFF_HD), 0.1),
        "diff_norm": gain((DEPTH, BRANCH_W)),
        "w_branch": nrm((DEPTH, N_BRANCH, BRANCH_W, D), BRANCH_W ** -0.5),
        "w_gate": nrm((DEPTH, N_BRANCH, D, D), D ** -0.5),
        "w_out": nrm((DEPTH, D, D), D ** -0.5),
        "norm_ffn": gain((DEPTH, D)),
        "w_router_group": nrm((DEPTH, D, MOE_GROUPS), D ** -0.5),
        "w_router_expert": nrm((DEPTH, D, MOE_EXPERTS), D ** -0.5),
        "moe_w_gate": nrm((DEPTH, MOE_EXPERTS, D, MOE_FF), D ** -0.5),
        "moe_w_up": nrm((DEPTH, MOE_EXPERTS, D, MOE_FF), D ** -0.5),
        "moe_w_down": nrm((DEPTH, MOE_EXPERTS, MOE_FF, D), MOE_FF ** -0.5),
        "norm_final": gain((D,)),
    }


def reference(x_prompt, x_sample, norm_mix, w_in, mlstm_gate_bias, mlstm_norm, gqa_q_norm, gqa_k_norm,
              ssd_conv_w, ssd_conv_b, ssd_dt_bias, ssd_a_log, ssd_d, ssd_norm, diff_lambda, diff_norm,
              w_branch, w_gate, w_out, norm_ffn, w_router_group, w_router_expert,
              moe_w_gate, moe_w_up, moe_w_down, norm_final):
    o0 = MLSTM_COLS
    o1 = o0 + GQA_COLS
    o2 = o1 + SSD_COLS

    def trunk(x):
        n = x.shape[1]
        cos, sin = axial_rope_angles(n)
        for l in range(DEPTH):
            lambda_init = 0.8 - 0.6 * math.exp(-0.3 * l)
            h = rms_norm(x, norm_mix[l])
            proj = (h @ w_in[l]).astype(jnp.float32)
            branches = (
                mlstm_branch(proj[..., :o0], mlstm_gate_bias[l], mlstm_norm[l]),
                gqa_branch(proj[..., o0:o1], gqa_q_norm[l], gqa_k_norm[l], cos, sin),
                ssd_branch(proj[..., o1:o2], ssd_conv_w[l], ssd_conv_b[l], ssd_dt_bias[l], ssd_a_log[l],
                           ssd_d[l], ssd_norm[l]),
                diff_branch(proj[..., o2:], diff_lambda[l], diff_norm[l], lambda_init),
            )
            merged = jnp.zeros_like(x)
            for i, yb in enumerate(branches):
                gate = jax.nn.sigmoid(h @ w_gate[l, i])
                merged = merged + gate * (yb.astype(x.dtype) @ w_branch[l, i])
            x = x + merged @ w_out[l]
            h2 = rms_norm(x, norm_ffn[l])
            x = x + hier_moe(h2, w_router_group[l], w_router_expert[l], moe_w_gate[l], moe_w_up[l], moe_w_down[l])
        return rms_norm(x, norm_final)

    y_prompt = trunk(x_prompt)
    y_sample = trunk(x_sample)
    return (y_prompt, y_sample)
```

```python
import functools
import math

import jax
import jax.numpy as jnp
import numpy as np
from jax import lax
from jax.experimental import pallas as pl
from jax.experimental.pallas import tpu as pltpu

F32 = jnp.float32
BF16 = jnp.bfloat16

D_MODEL = 1024
DEPTH = 2
GRID_W = 64
BRANCH_W = 512
NORM_EPS = 1e-6
MLSTM_HEADS = 4
MLSTM_HD = 128
GQA_HD = 64
GQA_HEADS = 8
GQA_KV_HEADS = 2
ROPE_THETA = 10000.0
SSD_HD = 64
SSD_HEADS = 8
SSD_GROUPS = 2
SSD_STATE = 128
SSD_CONV = 5
DIFF_HD = 64
DIFF_HEADS = 4
MOE_GROUPS = 4
MOE_EPG = 8
MOE_EXPERTS = 32
MOE_FF = 512
CHUNK = 128
LOG2E = 1.4426950408889634

_MLSTM_COLS = 4 * BRANCH_W + 4 * MLSTM_HEADS
_GQA_COLS = GQA_HEADS * GQA_HD + 2 * GQA_KV_HEADS * GQA_HD
_SSD_XBC = BRANCH_W + 2 * SSD_GROUPS * SSD_STATE
_SSD_COLS = BRANCH_W + _SSD_XBC + 2 * SSD_HEADS
_O0 = _MLSTM_COLS
_O1 = _O0 + _GQA_COLS
_O2 = _O1 + _SSD_COLS

ML_Q, ML_K, ML_V, ML_O = 0, 512, 1024, 1536
GQ_Q = 2048
SS_Z = 2560
DF_Q, DF_K, DF_V = 3072, 3584, 4096
SS_X, SS_BC = 4608, 5120
GQ_KV = 5632
P_COLS = 5888
SMALL_COLS = 128

LANE = 128
VMEM_LIMIT = 48 * 1024 * 1024

MOE_BLK = 256


def _cparams(sem):
    return pltpu.CompilerParams(dimension_semantics=sem, vmem_limit_bytes=VMEM_LIMIT)


def _dot(a, b):
    return jnp.dot(a, b, preferred_element_type=F32)


def _dot_nt(a, b):
    return lax.dot_general(a, b, (((1,), (1,)), ((), ())), preferred_element_type=F32)


def _dot_tn(a, b):
    return lax.dot_general(a, b, (((0,), (0,)), ((), ())), preferred_element_type=F32)


def _dot_hi(a, b):
    return jnp.dot(a, b, preferred_element_type=F32, precision=lax.Precision.HIGHEST)


def _sigmoid(x):
    return 1.0 / (1.0 + jnp.exp(-x))


def _silu(x):
    return x * _sigmoid(x)


def _log_sigmoid(x):
    return jnp.minimum(x, 0.0) - jnp.log(1.0 + jnp.exp(-jnp.abs(x)))


def _softplus(x):
    return jnp.maximum(x, 0.0) + jnp.log(1.0 + jnp.exp(-jnp.abs(x)))


def _main_col_index():
    gq = np.arange(GQA_HEADS * GQA_HD).reshape(GQA_HEADS, GQA_HD // 2, 2)
    gq = np.concatenate([gq[..., 0], gq[..., 1]], axis=-1).reshape(-1)
    gk = np.arange(GQA_KV_HEADS * GQA_HD).reshape(GQA_KV_HEADS, GQA_HD // 2, 2)
    gk = np.concatenate([gk[..., 0], gk[..., 1]], axis=-1).reshape(-1)
    segs = [
        np.arange(0, 4 * BRANCH_W),
        _O0 + gq,
        _O1 + np.arange(0, BRANCH_W),
        _O2 + np.arange(0, 3 * BRANCH_W),
        _O1 + BRANCH_W + np.arange(0, _SSD_XBC),
        _O0 + GQA_HEADS * GQA_HD + gk,
        _O0 + GQA_HEADS * GQA_HD + GQA_KV_HEADS * GQA_HD + np.arange(GQA_KV_HEADS * GQA_HD),
    ]
    idx = np.concatenate(segs)
    assert idx.shape[0] == P_COLS
    return idx


def _small_col_index():
    return np.concatenate([4 * BRANCH_W + np.arange(4 * MLSTM_HEADS),
                           _O1 + BRANCH_W + _SSD_XBC + np.arange(2 * SSD_HEADS)])


_MAIN_IDX = _main_col_index()
_SMALL_IDX = _small_col_index()
_DEINT64 = np.concatenate([np.arange(0, GQA_HD, 2), np.arange(1, GQA_HD, 2)])


def _inproj_kernel(x_ref, g_ref, w_ref, ws_ref, p_ref, h_ref, s_ref, h_scr):
    @pl.when(pl.program_id(1) == 0)
    def _():
        x = x_ref[...]
        ms = jnp.mean(x * x, axis=-1, keepdims=True)
        h = (x * lax.rsqrt(ms + NORM_EPS) * g_ref[...]).astype(BF16)
        h_scr[...] = h
        h_ref[...] = h
        s_ref[...] = _dot(h, ws_ref[...])

    p_ref[...] = _dot(h_scr[...], w_ref[...]).astype(BF16)


def _in_proj(x, g, w_main, w_small, tm=1024, tn=256):
    T = x.shape[0]
    return pl.pallas_call(
        _inproj_kernel,
        grid=(T // tm, P_COLS // tn),
        in_specs=[
            pl.BlockSpec((tm, D_MODEL), lambda i, j: (i, 0)),
            pl.BlockSpec((1, D_MODEL), lambda i, j: (0, 0)),
            pl.BlockSpec((D_MODEL, tn), lambda i, j: (0, j)),
            pl.BlockSpec((D_MODEL, SMALL_COLS), lambda i, j: (0, 0)),
        ],
        out_specs=[
            pl.BlockSpec((tm, tn), lambda i, j: (i, j)),
            pl.BlockSpec((tm, D_MODEL), lambda i, j: (i, 0)),
            pl.BlockSpec((tm, SMALL_COLS), lambda i, j: (i, 0)),
        ],
        out_shape=[
            jax.ShapeDtypeStruct((T, P_COLS), BF16),
            jax.ShapeDtypeStruct((T, D_MODEL), BF16),
            jax.ShapeDtypeStruct((T, SMALL_COLS), F32),
        ],
        scratch_shapes=[pltpu.VMEM((tm, D_MODEL), BF16)],
        compiler_params=_cparams(("parallel", "arbitrary")),
        name="in_proj",
    )(x, g, w_main, w_small)


def _tri_masks(L, reverse):
    r = lax.broadcasted_iota(jnp.int32, (L, L), 0)
    c = lax.broadcasted_iota(jnp.int32, (L, L), 1)
    return (c >= r) if reverse else (c <= r)


def _mlstm_kernel(*refs, reverse, final):
    if final:
        (q_ref, k_ref, v_ref, sm_ref, gb_ref, o_ref, hb_ref, ng_ref,
         out_ref, ct_scr, n_scr, m_scr) = refs
    else:
        q_ref, k_ref, v_ref, sm_ref, gb_ref, out_ref, ct_scr, n_scr, m_scr = refs
    L = CHUNK
    H = MLSTM_HEADS
    d = MLSTM_HD

    @pl.when(pl.program_id(1) == 0)
    def _():
        ct_scr[...] = jnp.zeros_like(ct_scr)
        n_scr[...] = jnp.zeros_like(n_scr)
        m_scr[...] = jnp.zeros_like(m_scr)

    mask = _tri_masks(L, reverse)
    tri = mask.astype(F32)
    gates = sm_ref[...] + gb_ref[...]
    logf = _log_sigmoid(gates)
    bcum = _dot_hi(tri, logf)
    gates_t = gates.T
    bcum_t = bcum.T
    i_off = 8 if reverse else 0
    f_off = i_off + 4
    edge = 0 if reverse else L - 1
    scale = d ** -0.5
    m_all = m_scr[...]

    hs = []
    for h in range(H):
        qh = q_ref[:, h * d:(h + 1) * d]
        ks = (k_ref[:, h * d:(h + 1) * d].astype(F32) * scale).astype(BF16)
        vh = v_ref[:, h * d:(h + 1) * d]
        b_col = bcum[:, f_off + h:f_off + h + 1]
        i_col = gates[:, i_off + h:i_off + h + 1]
        a_row = gates_t[i_off + h:i_off + h + 1, :] - bcum_t[f_off + h:f_off + h + 1, :]
        m_prev = m_all[:, h:h + 1]
        dmat = jnp.where(mask, b_col + a_row, -jnp.inf)
        inter = b_col + m_prev
        m_t = jnp.maximum(inter, jnp.max(dmat, axis=1, keepdims=True))
        w_intra = jnp.exp(dmat - m_t)
        w_inter = jnp.exp(inter - m_t)
        s = _dot_nt(qh, ks) * w_intra
        ct = ct_scr[h]
        n_row = n_scr[h]
        num = _dot(s.astype(BF16), vh) + w_inter * _dot(qh, ct.astype(BF16))
        qn = jnp.sum(qh.astype(F32) * n_row, axis=1, keepdims=True)
        den = jnp.sum(s, axis=1, keepdims=True) + w_inter * qn
        hs.append(num / jnp.maximum(jnp.abs(den), jnp.exp(-m_t)))
        btot = b_col[edge:edge + 1, :]
        dec = btot - b_col + i_col
        m_new = jnp.maximum(btot + m_prev, jnp.max(dec, axis=0, keepdims=True))
        ws = jnp.exp(dec - m_new)
        wc = jnp.exp(btot + m_prev - m_new)
        wsv = (ws * vh.astype(F32)).astype(BF16)
        ct_scr[h] = wc * ct + _dot_tn(ks, wsv)
        n_scr[h] = wc * n_row + jnp.sum(ws * ks.astype(F32), axis=0, keepdims=True)
        lane = lax.broadcasted_iota(jnp.int32, (1, LANE), 1)
        m_all = jnp.where(lane == h, m_new, m_all)
    m_scr[...] = m_all

    if not final:
        for h in range(H):
            out_ref[:, h * d:(h + 1) * d] = hs[h]
    else:
        for h in range(H):
            hsum = hs[h] + hb_ref[:, h * d:(h + 1) * d]
            ms = jnp.mean(hsum * hsum, axis=1, keepdims=True)
            y = hsum * lax.rsqrt(ms + NORM_EPS) * ng_ref[:, h * d:(h + 1) * d]
            o = o_ref[:, h * d:(h + 1) * d].astype(F32)
            out_ref[:, h * d:(h + 1) * d] = (_sigmoid(o) * y).astype(BF16)


def _mlstm_dir(p, small, gate_bias_row, B, N, reverse, hb=None, norm_g=None):
    L = CHUNK
    nc = N // L
    T = B * N
    final = hb is not None

    def row(b, c):
        return b * nc + ((nc - 1 - c) if reverse else c)

    W = BRANCH_W
    in_specs = [
        pl.BlockSpec((L, W), lambda b, c: (row(b, c), ML_Q // W)),
        pl.BlockSpec((L, W), lambda b, c: (row(b, c), ML_K // W)),
        pl.BlockSpec((L, W), lambda b, c: (row(b, c), ML_V // W)),
        pl.BlockSpec((L, SMALL_COLS), lambda b, c: (row(b, c), 0)),
        pl.BlockSpec((1, SMALL_COLS), lambda b, c: (0, 0)),
    ]
    args = [p, p, p, small, gate_bias_row]
    if final:
        in_specs += [
            pl.BlockSpec((L, W), lambda b, c: (row(b, c), ML_O // W)),
            pl.BlockSpec((L, W), lambda b, c: (row(b, c), 0)),
            pl.BlockSpec((1, W), lambda b, c: (0, 0)),
        ]
        args += [p, hb, norm_g]
    return pl.pallas_call(
        functools.partial(_mlstm_kernel, reverse=reverse, final=final),
        grid=(B, nc),
        in_specs=in_specs,
        out_specs=pl.BlockSpec((L, W), lambda b, c: (row(b, c), 0)),
        out_shape=jax.ShapeDtypeStruct((T, W), BF16 if final else F32),
        scratch_shapes=[
            pltpu.VMEM((MLSTM_HEADS, MLSTM_HD, MLSTM_HD), F32),
            pltpu.VMEM((MLSTM_HEADS, 1, MLSTM_HD), F32),
            pltpu.VMEM((1, LANE), F32),
        ],
        compiler_params=_cparams(("parallel", "arbitrary")),
        name="mlstm_bwd" if reverse else "mlstm_fwd",
    )(*args)


def _mlstm_branch(p, small, gate_bias_row, norm_g, B, N):
    hb = _mlstm_dir(p, small, gate_bias_row, B, N, reverse=True)
    return _mlstm_dir(p, small, gate_bias_row, B, N, reverse=False, hb=hb, norm_g=norm_g)


def _gqa_prep_kernel(q_ref, kv_ref, cos_ref, sin_ref, qg_ref, kg_ref, bd_ref, qo_ref, ko_ref, vo_ref):
    tm = q_ref.shape[0]
    bd = bd_ref[...]
    cos = cos_ref[...]
    sin = sin_ref[...]
    lane = lax.broadcasted_iota(jnp.int32, (tm, LANE), 1)
    first = (lane % GQA_HD) < (GQA_HD // 2)
    lo = lane < GQA_HD

    def norm_rope(x, g):
        ss = _dot((x * x).astype(BF16), bd)
        xn = x * lax.rsqrt(ss * (1.0 / GQA_HD) + NORM_EPS) * g
        partner = jnp.where(first, pltpu.roll(xn, LANE - GQA_HD // 2, axis=1),
                            pltpu.roll(xn, GQA_HD // 2, axis=1))
        return xn * cos + partner * sin

    qscale = (GQA_HD ** -0.5) * LOG2E
    for a in range(GQA_HEADS * GQA_HD // LANE):
        x = q_ref[:, a * LANE:(a + 1) * LANE].astype(F32)
        qo_ref[:, a * LANE:(a + 1) * LANE] = (norm_rope(x, qg_ref[...]) * qscale).astype(BF16)
    k = norm_rope(kv_ref[:, 0:LANE].astype(F32), kg_ref[...])
    k_sw = pltpu.roll(k, GQA_HD, axis=1)
    ko_ref[:, 0:LANE] = jnp.where(lo, k, k_sw).astype(BF16)
    ko_ref[:, LANE:2 * LANE] = jnp.where(lo, k_sw, k).astype(BF16)
    v = kv_ref[:, LANE:2 * LANE].astype(F32)
    v_sw = pltpu.roll(v, GQA_HD, axis=1)
    vo_ref[:, 0:LANE] = jnp.where(lo, v, 1.0).astype(BF16)
    vo_ref[:, LANE:2 * LANE] = jnp.where(lo, v_sw, 1.0).astype(BF16)


def _gqa_prep(p, cos_t, sin_t, qg, kg, bd, B, N, tm=512):
    T = B * N
    nb = N // tm
    return pl.pallas_call(
        _gqa_prep_kernel,
        grid=(T // tm,),
        in_specs=[
            pl.BlockSpec((tm, 512), lambda i: (i, GQ_Q // 512)),
            pl.BlockSpec((tm, 256), lambda i: (i, GQ_KV // 256)),
            pl.BlockSpec((tm, LANE), lambda i: (i % nb, 0)),
            pl.BlockSpec((tm, LANE), lambda i: (i % nb, 0)),
            pl.BlockSpec((1, LANE), lambda i: (0, 0)),
            pl.BlockSpec((1, LANE), lambda i: (0, 0)),
            pl.BlockSpec((LANE, LANE), lambda i: (0, 0)),
        ],
        out_specs=[
            pl.BlockSpec((tm, 512), lambda i: (i, 0)),
            pl.BlockSpec((tm, 256), lambda i: (i, 0)),
            pl.BlockSpec((tm, 256), lambda i: (i, 0)),
        ],
        out_shape=[
            jax.ShapeDtypeStruct((T, 512), BF16),
            jax.ShapeDtypeStruct((T, 256), BF16),
            jax.ShapeDtypeStruct((T, 256), BF16),
        ],
        compiler_params=_cparams(("parallel",)),
        name="gqa_prep",
    )(p, p, cos_t, sin_t, qg, kg, bd)


def _flash_step(qs_scr, m_scr, acc_scr, k, va, bias=None):
    s = _dot_nt(qs_scr[...], k)
    if bias is not None:
        s = s + bias
    m_prev = m_scr[...]
    m_new = jnp.maximum(m_prev, jnp.max(s, axis=1, keepdims=True))
    alpha = jnp.exp2(m_prev - m_new)
    p = jnp.exp2(s - m_new[:, 0:1])
    pv = _dot(p.astype(BF16), va)
    nv = acc_scr.shape[1]
    if nv == LANE:
        acc_scr[...] = alpha * acc_scr[...] + pv
    else:
        acc_scr[...] = jnp.concatenate([alpha] * (nv // LANE), axis=1) * acc_scr[...] + pv
    m_scr[...] = m_new


def _gqa_flash_kernel(q_ref, k_ref, v_ref, o_ref, qs_scr, m_scr, acc_scr):
    tq = q_ref.shape[0]
    kj = pl.program_id(3)
    lane = lax.broadcasted_iota(jnp.int32, (tq, LANE), 1)
    lo = lane < GQA_HD

    @pl.when(kj == 0)
    def _():
        zero = jnp.zeros((tq, LANE), BF16)
        for a in range(2):
            qa = q_ref[:, a * LANE:(a + 1) * LANE]
            qs_scr[(2 * a) * tq:(2 * a + 1) * tq, :] = jnp.where(lo, qa, zero)
            qs_scr[(2 * a + 1) * tq:(2 * a + 2) * tq, :] = jnp.where(lo, zero, qa)
        m_scr[...] = jnp.full_like(m_scr, -jnp.inf)
        acc_scr[...] = jnp.zeros_like(acc_scr)

    _flash_step(qs_scr, m_scr, acc_scr, k_ref[...], v_ref[...])

    @pl.when(kj == pl.num_programs(3) - 1)
    def _():
        for a in range(2):
            o0 = acc_scr[(2 * a) * tq:(2 * a + 1) * tq, :]
            o1 = acc_scr[(2 * a + 1) * tq:(2 * a + 2) * tq, :]
            n0 = o0 / o0[:, GQA_HD:GQA_HD + 1]
            n1 = o1 / o1[:, GQA_HD:GQA_HD + 1]
            o_ref[:, a * LANE:(a + 1) * LANE] = jnp.where(lo, n0, pltpu.roll(n1, GQA_HD, axis=1)).astype(BF16)


def _gqa_flash(qn, k2, va, B, N, tq=256, tk=512):
    T = B * N
    nq, nk = N // tq, N // tk
    return pl.pallas_call(
        _gqa_flash_kernel,
        grid=(B, GQA_KV_HEADS, nq, nk),
        in_specs=[
            pl.BlockSpec((tq, 256), lambda b, g, i, j: (b * nq + i, g)),
            pl.BlockSpec((tk, LANE), lambda b, g, i, j: (b * nk + j, g)),
            pl.BlockSpec((tk, LANE), lambda b, g, i, j: (b * nk + j, g)),
        ],
        out_specs=pl.BlockSpec((tq, 256), lambda b, g, i, j: (b * nq + i, g)),
        out_shape=jax.ShapeDtypeStruct((T, 512), BF16),
        scratch_shapes=[
            pltpu.VMEM((4 * tq, LANE), BF16),
            pltpu.VMEM((4 * tq, LANE), F32),
            pltpu.VMEM((4 * tq, LANE), F32),
        ],
        compiler_params=_cparams(("parallel", "parallel", "parallel", "arbitrary")),
        name="gqa_flash",
    )(qn, k2, va)


def _rope_tables(N):
    rows = N // GRID_W
    row = jnp.repeat(jnp.arange(rows, dtype=F32), GRID_W)
    col = (jnp.arange(N) % GRID_W).astype(F32)
    half = GQA_HD // 2
    inv = 1.0 / (ROPE_THETA ** (jnp.arange(0, half, 2, dtype=F32) / half))
    ang = jnp.concatenate([row[:, None] * inv, col[:, None] * inv], axis=-1)
    cos, sin = jnp.cos(ang), jnp.sin(ang)
    cos_h = jnp.concatenate([cos, cos], axis=-1)
    sin_h = jnp.concatenate([-sin, sin], axis=-1)
    return jnp.tile(cos_h, (1, LANE // GQA_HD)), jnp.tile(sin_h, (1, LANE // GQA_HD))


def _gqa_branch(p, cos_t, sin_t, qg, kg, bd, B, N):
    qn, k2, va = _gqa_prep(p, cos_t, sin_t, qg, kg, bd, B, N)
    return _gqa_flash(qn, k2, va, B, N)


def _diff_flash_kernel(q_ref, k_ref, v_ref, lp_ref, ng_ref, o_ref, qs_scr, m_scr, acc_scr, *, lambda_init):
    tq = q_ref.shape[0]
    tk = k_ref.shape[0]
    h = pl.program_id(1)
    qi = pl.program_id(2)
    kj = pl.program_id(3)

    @pl.when(kj == 0)
    def _():
        lane = lax.broadcasted_iota(jnp.int32, (tq, LANE), 1)
        lo = lane < DIFF_HD
        zero = jnp.zeros((tq, LANE), BF16)
        q = q_ref[...]
        qs_scr[0:tq, :] = jnp.where(lo, q, zero)
        qs_scr[tq:2 * tq, :] = jnp.where(lo, zero, q)
        m_scr[...] = jnp.full_like(m_scr, -jnp.inf)
        acc_scr[...] = jnp.zeros_like(acc_scr)

    slope = jnp.exp2(-2.0 * (jnp.full((1, 1), h, jnp.int32).astype(F32) + 1.0)) * LOG2E
    qpos = (qi * tq + lax.broadcasted_iota(jnp.int32, (tq, 1), 0)).astype(F32)
    kpos = (kj * tk + lax.broadcasted_iota(jnp.int32, (1, tk), 1)).astype(F32)
    bias = -slope * jnp.abs(qpos - kpos)
    bias2 = jnp.concatenate([bias, bias], axis=0)
    v = v_ref[...]
    va = jnp.concatenate([v, jnp.ones_like(v)], axis=1)
    _flash_step(qs_scr, m_scr, acc_scr, k_ref[...], va, bias=bias2)

    @pl.when(kj == pl.num_programs(3) - 1)
    def _():
        lp = lp_ref[...]
        s01 = jnp.sum(jnp.sum(lp[0:1] * lp[1:2], axis=1, keepdims=True), axis=0, keepdims=True)
        s23 = jnp.sum(jnp.sum(lp[2:3] * lp[3:4], axis=1, keepdims=True), axis=0, keepdims=True)
        lam = jnp.exp(s01) - jnp.exp(s23) + lambda_init
        acc = acc_scr[...]
        o0 = acc[0:tq, 0:LANE] / acc[0:tq, LANE:LANE + 1]
        o1 = acc[tq:2 * tq, 0:LANE] / acc[tq:2 * tq, LANE:LANE + 1]
        o = o0 - lam * o1
        ms = jnp.mean(o * o, axis=1, keepdims=True)
        o_ref[...] = (o * lax.rsqrt(ms + NORM_EPS) * ng_ref[...] * (1.0 - lambda_init)).astype(BF16)


def _diff_branch(p, lam_params, norm_g, lambda_init, B, N, tq=512, tk=512):
    T = B * N
    nq, nk = N // tq, N // tk
    return pl.pallas_call(
        functools.partial(_diff_flash_kernel, lambda_init=lambda_init),
        grid=(B, DIFF_HEADS, nq, nk),
        in_specs=[
            pl.BlockSpec((tq, LANE), lambda b, h, i, j: (b * nq + i, DF_Q // LANE + h)),
            pl.BlockSpec((tk, LANE), lambda b, h, i, j: (b * nk + j, DF_K // LANE + h)),
            pl.BlockSpec((tk, LANE), lambda b, h, i, j: (b * nk + j, DF_V // LANE + h)),
            pl.BlockSpec((4, DIFF_HD), lambda b, h, i, j: (0, 0)),
            pl.BlockSpec((1, LANE), lambda b, h, i, j: (0, h)),
        ],
        out_specs=pl.BlockSpec((tq, LANE), lambda b, h, i, j: (b * nq + i, h)),
        out_shape=jax.ShapeDtypeStruct((T, BRANCH_W), BF16),
        scratch_shapes=[
            pltpu.VMEM((2 * tq, LANE), BF16),
            pltpu.VMEM((2 * tq, LANE), F32),
            pltpu.VMEM((2 * tq, 2 * LANE), F32),
        ],
        compiler_params=_cparams(("parallel", "parallel", "parallel", "arbitrary")),
        name="diff_flash",
    )(p, p, p, lam_params, norm_g)


SSD_HALO = 16


def _ssd_prep_kernel(x_ref, xp_ref, xn_ref, bc_ref, bcp_ref, bcn_ref, sm_ref, cw_ref, cb_ref,
                     dtb_ref, arow_ref, xo_ref, bco_ref, dto_ref):
    tc = x_ref.shape[0]
    c = pl.program_id(1)
    nc = pl.num_programs(1)
    has_prev = jnp.where(c > 0, 1.0, 0.0)
    has_next = jnp.where(c < nc - 1, 1.0, 0.0)
    pad = SSD_CONV // 2

    def conv(main_ref, prev_ref, next_ref, off):
        xf = jnp.concatenate([prev_ref[...].astype(F32) * has_prev, main_ref[...].astype(F32),
                              next_ref[...].astype(F32) * has_next], axis=0)
        n = xf.shape[0]
        acc = jnp.zeros((tc, xf.shape[1]), F32) + cb_ref[:, off:off + xf.shape[1]]
        for j in range(SSD_CONV):
            sh = pltpu.roll(xf, (pad - j) % n, axis=0)[SSD_HALO:SSD_HALO + tc, :]
            acc = acc + sh * cw_ref[j:j + 1, off:off + xf.shape[1]]
        return _silu(acc)

    xo_ref[...] = conv(x_ref, xp_ref, xn_ref, 0).astype(BF16)
    bco_ref[...] = conv(bc_ref, bcp_ref, bcn_ref, BRANCH_W).astype(BF16)
    dt = _softplus(sm_ref[...] + dtb_ref[...])
    a = pltpu.roll(dt * arow_ref[...], LANE - 16, axis=1)
    lane = lax.broadcasted_iota(jnp.int32, (tc, LANE), 1)
    dto_ref[...] = jnp.where(lane < 16, a, jnp.where(lane < 32, dt, 0.0))


def _ssd_prep(p, small, conv_w, conv_b, dtb_row, a_row, B, N, tc=512):
    T = B * N
    nc = N // tc
    hb = tc // SSD_HALO
    W = BRANCH_W

    def main(col):
        return pl.BlockSpec((tc, W), lambda b, c: (b * nc + c, col))

    def prev(col):
        return pl.BlockSpec((SSD_HALO, W), lambda b, c: (jnp.maximum((b * nc + c) * hb - 1, 0), col))

    def nxt(col):
        return pl.BlockSpec((SSD_HALO, W), lambda b, c: (jnp.minimum((b * nc + c + 1) * hb, T // SSD_HALO - 1), col))

    cx, cbc = SS_X // W, SS_BC // W
    return pl.pallas_call(
        _ssd_prep_kernel,
        grid=(B, nc),
        in_specs=[
            main(cx), prev(cx), nxt(cx), main(cbc), prev(cbc), nxt(cbc),
            pl.BlockSpec((tc, SMALL_COLS), lambda b, c: (b * nc + c, 0)),
            pl.BlockSpec((SSD_CONV, 2 * W), lambda b, c: (0, 0)),
            pl.BlockSpec((1, 2 * W), lambda b, c: (0, 0)),
            pl.BlockSpec((1, LANE), lambda b, c: (0, 0)),
            pl.BlockSpec((1, LANE), lambda b, c: (0, 0)),
        ],
        out_specs=[
            pl.BlockSpec((tc, W), lambda b, c: (b * nc + c, 0)),
            pl.BlockSpec((tc, W), lambda b, c: (b * nc + c, 0)),
            pl.BlockSpec((tc, LANE), lambda b, c: (b * nc + c, 0)),
        ],
        out_shape=[
            jax.ShapeDtypeStruct((T, W), BF16),
            jax.ShapeDtypeStruct((T, W), BF16),
            jax.ShapeDtypeStruct((T, LANE), F32),
        ],
        compiler_params=_cparams(("parallel", "parallel")),
        name="ssd_prep",
    )(p, p, p, p, p, p, small, conv_w, conv_b, dtb_row, a_row)


def _ssd_kernel(*refs, reverse, final):
    if final:
        (x_ref, bc_ref, dta_ref, ea_ref, ed_ref, z_ref, yb_ref, dsk_ref, ng_ref,
         out_ref, hs_scr) = refs
    else:
        x_ref, bc_ref, dta_ref, ea_ref, ed_ref, out_ref, hs_scr = refs
    L = CHUNK
    S = SSD_STATE
    W = BRANCH_W
    GW = W // SSD_GROUPS

    @pl.when(pl.program_id(1) == 0)
    def _():
        hs_scr[...] = jnp.zeros_like(hs_scr)

    mask = _tri_masks(L, reverse)
    tri = mask.astype(F32)
    dta = dta_ref[...]
    acs = _dot_hi(tri, dta)
    acs_t = acs.T
    acs_e = _dot_hi(acs, ea_ref[...])
    dt_e = _dot_hi(dta, ed_ref[...])
    edge = 0 if reverse else L - 1
    a_off = 8 if reverse else 0
    acs_end = acs_e[edge:edge + 1, :]
    xf = x_ref[...].astype(F32)
    xdt = xf * dt_e
    xdt_b = xdt.astype(BF16)
    xdend = (xdt * jnp.exp(acs_end - acs_e)).astype(BF16)
    e_acs = jnp.exp(acs_e)
    cdec = jnp.exp(acs_end)
    lane = lax.broadcasted_iota(jnp.int32, (L, LANE), 1)
    lo = lane < SSD_HD
    zero = jnp.zeros((L, LANE), BF16)

    ys = []
    for g in range(SSD_GROUPS):
        bg = bc_ref[:, g * S:(g + 1) * S]
        cg = bc_ref[:, SSD_GROUPS * S + g * S:SSD_GROUPS * S + (g + 1) * S]
        cb = _dot_nt(cg, bg)
        hs = hs_scr[g]
        y_off = _dot(cg, hs.astype(BF16)) * e_acs[:, g * GW:(g + 1) * GW]
        hs_scr[g] = cdec[:, g * GW:(g + 1) * GW] * hs + _dot_tn(bg, xdend[:, g * GW:(g + 1) * GW])
        for pr in range(GW // LANE):
            col = g * GW + pr * LANE
            xp = xdt_b[:, col:col + LANE]
            yd = jnp.zeros((L, LANE), F32)
            for hh in range(2):
                hd = (col // SSD_HD) + hh
                a_col = acs[:, a_off + hd:a_off + hd + 1]
                a_row = acs_t[a_off + hd:a_off + hd + 1, :]
                decay = jnp.exp(jnp.where(mask, a_col - a_row, -jnp.inf))
                wm = (cb * decay).astype(BF16)
                xm = jnp.where(lo, xp, zero) if hh == 0 else jnp.where(lo, zero, xp)
                yd = yd + _dot(wm, xm)
            ys.append(yd + y_off[:, pr * LANE:(pr + 1) * LANE])
    y = jnp.concatenate(ys, axis=1)

    if not final:
        out_ref[...] = y
    else:
        y = y + yb_ref[...] + dsk_ref[...] * xf
        y = y * _silu(z_ref[...].astype(F32))
        outs = []
        for g in range(SSD_GROUPS):
            yg = y[:, g * GW:(g + 1) * GW]
            ms = jnp.mean(yg * yg, axis=1, keepdims=True)
            outs.append(yg * lax.rsqrt(ms + NORM_EPS))
        out_ref[...] = (jnp.concatenate(outs, axis=1) * ng_ref[...]).astype(BF16)


def _ssd_dir(xs, bc, dta, ea, ed, B, N, reverse, p=None, yb=None, dskip_row=None, norm_g=None):
    L = CHUNK
    nc = N // L
    T = B * N
    W = BRANCH_W
    final = yb is not None

    def row(b, c):
        return b * nc + ((nc - 1 - c) if reverse else c)

    in_specs = [
        pl.BlockSpec((L, W), lambda b, c: (row(b, c), 0)),
        pl.BlockSpec((L, W), lambda b, c: (row(b, c), 0)),
        pl.BlockSpec((L, LANE), lambda b, c: (row(b, c), 0)),
        pl.BlockSpec((LANE, W), lambda b, c: (0, 0)),
        pl.BlockSpec((LANE, W), lambda b, c: (0, 0)),
    ]
    args = [xs, bc, dta, ea, ed]
    if final:
        in_specs += [
            pl.BlockSpec((L, W), lambda b, c: (row(b, c), SS_Z // W)),
            pl.BlockSpec((L, W), lambda b, c: (row(b, c), 0)),
            pl.BlockSpec((1, W), lambda b, c: (0, 0)),
            pl.BlockSpec((1, W), lambda b, c: (0, 0)),
        ]
        args += [p, yb, dskip_row, norm_g]
    return pl.pallas_call(
        functools.partial(_ssd_kernel, reverse=reverse, final=final),
        grid=(B, nc),
        in_specs=in_specs,
        out_specs=pl.BlockSpec((L, W), lambda b, c: (row(b, c), 0)),
        out_shape=jax.ShapeDtypeStruct((T, W), BF16 if final else F32),
        scratch_shapes=[pltpu.VMEM((SSD_GROUPS, SSD_STATE, W // SSD_GROUPS), F32)],
        compiler_params=_cparams(("parallel", "arbitrary")),
        name="ssd_bwd" if reverse else "ssd_fwd",
    )(*args)


def _ssd_expand_mats():
    ea_f = np.zeros((LANE, BRANCH_W), np.float32)
    ea_b = np.zeros((LANE, BRANCH_W), np.float32)
    ed_f = np.zeros((LANE, BRANCH_W), np.float32)
    ed_b = np.zeros((LANE, BRANCH_W), np.float32)
    for h in range(SSD_HEADS):
        ea_f[h, h * SSD_HD:(h + 1) * SSD_HD] = 1.0
        ea_b[8 + h, h * SSD_HD:(h + 1) * SSD_HD] = 1.0
        ed_f[16 + h, h * SSD_HD:(h + 1) * SSD_HD] = 1.0
        ed_b[24 + h, h * SSD_HD:(h + 1) * SSD_HD] = 1.0
    return ea_f, ea_b, ed_f, ed_b


_EA_F, _EA_B, _ED_F, _ED_B = _ssd_expand_mats()


def _ssd_branch(p, small, conv_w, conv_b, dtb_row, a_row, dskip_row, norm_g, B, N):
    xs, bc, dta = _ssd_prep(p, small, conv_w, conv_b, dtb_row, a_row, B, N)
    yb = _ssd_dir(xs, bc, dta, jnp.asarray(_EA_B), jnp.asarray(_ED_B), B, N, reverse=True)
    return _ssd_dir(xs, bc, dta, jnp.asarray(_EA_F), jnp.asarray(_ED_F), B, N, reverse=False,
                    p=p, yb=yb, dskip_row=dskip_row, norm_g=norm_g)


def _merge_kernel(h_ref, y0_ref, y1_ref, y2_ref, y3_ref, wg_ref, wb_ref, o_ref):
    h = h_ref[...]
    acc = None
    for i, y_ref in enumerate((y0_ref, y1_ref, y2_ref, y3_ref)):
        gate = _sigmoid(_dot(h, wg_ref[i]))
        term = gate * _dot(y_ref[...], wb_ref[i])
        acc = term if acc is None else acc + term
    o_ref[...] = acc.astype(BF16)


def _merge(h, ys, wg, wb, tm=1024, tn=256):
    T = h.shape[0]
    return pl.pallas_call(
        _merge_kernel,
        grid=(T // tm, D_MODEL // tn),
        in_specs=[pl.BlockSpec((tm, D_MODEL), lambda i, j: (i, 0))]
        + [pl.BlockSpec((tm, BRANCH_W), lambda i, j: (i, 0))] * 4
        + [
            pl.BlockSpec((4, D_MODEL, tn), lambda i, j: (0, 0, j)),
            pl.BlockSpec((4, BRANCH_W, tn), lambda i, j: (0, 0, j)),
        ],
        out_specs=pl.BlockSpec((tm, tn), lambda i, j: (i, j)),
        out_shape=jax.ShapeDtypeStruct((T, D_MODEL), BF16),
        compiler_params=_cparams(("parallel", "arbitrary")),
        name="merge",
    )(h, *ys, wg, wb)


def _outproj_kernel(x_ref, m_ref, wo_ref, g_ref, wr_ref, xo_ref, h2_ref, r_ref):
    x = x_ref[...] + _dot(m_ref[...], wo_ref[...])
    xo_ref[...] = x
    ms = jnp.mean(x * x, axis=-1, keepdims=True)
    h2 = x * lax.rsqrt(ms + NORM_EPS) * g_ref[...]
    h2_ref[...] = h2.astype(BF16)
    logits = _dot_hi(h2, wr_ref[...])
    tm = x.shape[0]
    lane = lax.broadcasted_iota(jnp.int32, (tm, LANE), 1).astype(F32)
    neg = -jnp.inf
    big = float(LANE)
    gl = jnp.where(lane < MOE_GROUPS, logits, neg)
    gmax = jnp.max(gl, axis=1, keepdims=True)
    g_idx = jnp.min(jnp.where(gl == gmax, lane, big), axis=1, keepdims=True)
    g_w = 1.0 / jnp.sum(jnp.exp(gl - gmax), axis=1, keepdims=True)
    e_lo = MOE_GROUPS + g_idx * MOE_EPG
    el = jnp.where(lane >= e_lo, jnp.where(lane < e_lo + MOE_EPG, logits, neg), neg)
    m1 = jnp.max(el, axis=1, keepdims=True)
    i1 = jnp.min(jnp.where(el == m1, lane, big), axis=1, keepdims=True)
    el2 = jnp.where(lane == i1, neg, el)
    m2 = jnp.max(el2, axis=1, keepdims=True)
    i2 = jnp.min(jnp.where(el2 == m2, lane, big), axis=1, keepdims=True)
    den = jnp.sum(jnp.exp(el - m1), axis=1, keepdims=True)
    p1 = 1.0 / den
    p2 = jnp.exp(m2 - m1) / den
    w1 = p1 / (p1 + p2) * g_w
    w2 = p2 / (p1 + p2) * g_w
    e1 = i1 - MOE_GROUPS
    e2 = i2 - MOE_GROUPS
    r_ref[...] = jnp.where(lane == 0, e1, jnp.where(lane == 1, e2, jnp.where(lane == 2, w1, jnp.where(lane == 3, w2, 0.0))))


def _out_proj(x, merged, wo, g, wr, tm=512):
    T = x.shape[0]
    return pl.pallas_call(
        _outproj_kernel,
        grid=(T // tm,),
        in_specs=[
            pl.BlockSpec((tm, D_MODEL), lambda i: (i, 0)),
            pl.BlockSpec((tm, D_MODEL), lambda i: (i, 0)),
            pl.BlockSpec((D_MODEL, D_MODEL), lambda i: (0, 0)),
            pl.BlockSpec((1, D_MODEL), lambda i: (0, 0)),
            pl.BlockSpec((D_MODEL, LANE), lambda i: (0, 0)),
        ],
        out_specs=[
            pl.BlockSpec((tm, D_MODEL), lambda i: (i, 0)),
            pl.BlockSpec((tm, D_MODEL), lambda i: (i, 0)),
            pl.BlockSpec((tm, LANE), lambda i: (i, 0)),
        ],
        out_shape=[
            jax.ShapeDtypeStruct((T, D_MODEL), F32),
            jax.ShapeDtypeStruct((T, D_MODEL), BF16),
            jax.ShapeDtypeStruct((T, LANE), F32),
        ],
        compiler_params=_cparams(("parallel",)),
        name="out_proj_router",
    )(x, merged, wo, g, wr)


def _moe_ffn_kernel(be_ref, nu_ref, xs_ref, sw_ref, w1_ref, w3_ref, w2_ref, ys_ref):
    @pl.when(pl.program_id(0) < nu_ref[0])
    def _():
        xb = xs_ref[...]
        a = _dot(xb, w1_ref[0])
        u = _dot(xb, w3_ref[0])
        hmid = (_silu(a) * u).astype(BF16)
        ys_ref[...] = (_dot(hmid, w2_ref[0]) * sw_ref[...]).astype(BF16)


def _moe_ffn(xs, sw, blk_e, nused, w1, w3, w2):
    R = xs.shape[0]
    nblk = R // MOE_BLK
    return pl.pallas_call(
        _moe_ffn_kernel,
        grid_spec=pltpu.PrefetchScalarGridSpec(
            num_scalar_prefetch=2,
            grid=(nblk,),
            in_specs=[
                pl.BlockSpec((MOE_BLK, D_MODEL), lambda b, be, nu: (b, 0)),
                pl.BlockSpec((MOE_BLK, 1), lambda b, be, nu: (b, 0)),
                pl.BlockSpec((1, D_MODEL, MOE_FF), lambda b, be, nu: (be[b], 0, 0)),
                pl.BlockSpec((1, D_MODEL, MOE_FF), lambda b, be, nu: (be[b], 0, 0)),
                pl.BlockSpec((1, MOE_FF, D_MODEL), lambda b, be, nu: (be[b], 0, 0)),
            ],
            out_specs=pl.BlockSpec((MOE_BLK, D_MODEL), lambda b, be, nu: (b, 0)),
        ),
        out_shape=jax.ShapeDtypeStruct((R, D_MODEL), BF16),
        compiler_params=_cparams(("arbitrary",)),
        name="moe_ffn",
    )(blk_e, nused, xs, sw, w1, w3, w2)


def _combine_kernel(x_ref, a_ref, b_ref, g_ref, o_ref, *, final):
    x = x_ref[...] + (a_ref[...].astype(F32) + b_ref[...].astype(F32))
    if final:
        ms = jnp.mean(x * x, axis=-1, keepdims=True)
        x = x * lax.rsqrt(ms + NORM_EPS) * g_ref[...]
    o_ref[...] = x


def _combine(x, a, b, g, final, tm=512):
    T = x.shape[0]
    return pl.pallas_call(
        functools.partial(_combine_kernel, final=final),
        grid=(T // tm,),
        in_specs=[pl.BlockSpec((tm, D_MODEL), lambda i: (i, 0))] * 3
        + [pl.BlockSpec((1, D_MODEL), lambda i: (0, 0))],
        out_specs=pl.BlockSpec((tm, D_MODEL), lambda i: (i, 0)),
        out_shape=jax.ShapeDtypeStruct((T, D_MODEL), F32),
        compiler_params=_cparams(("parallel",)),
        name="moe_combine",
    )(x, a, b, g)


def _moe(x, h2, route, w1, w3, w2, g_final, final):
    T = x.shape[0]
    K = 2
    e_idx = route[:, 0:K].astype(jnp.int32)
    e_w = route[:, K:2 * K]
    flat_e = e_idx.reshape(-1)
    flat_w = e_w.reshape(-1)
    onehot = (flat_e[:, None] == jnp.arange(MOE_EXPERTS)[None, :]).astype(jnp.int32)
    csum = jnp.cumsum(onehot, axis=0)
    rank = jnp.take_along_axis(csum, flat_e[:, None], axis=1)[:, 0] - 1
    counts = csum[-1]
    pcounts = (counts + MOE_BLK - 1) // MOE_BLK * MOE_BLK
    pends = jnp.cumsum(pcounts)
    pstarts = pends - pcounts
    dest = pstarts[flat_e] + rank
    R = T * K + MOE_EXPERTS * MOE_BLK
    nblk = R // MOE_BLK
    row_src = jnp.zeros((R,), jnp.int32).at[dest].set(jnp.arange(T * K, dtype=jnp.int32) // K)
    row_w = jnp.zeros((R,), F32).at[dest].set(flat_w)
    nused = (pends[-1] // MOE_BLK).astype(jnp.int32)
    blk_start = jnp.arange(nblk, dtype=jnp.int32) * MOE_BLK
    blk_e = jnp.minimum(jnp.searchsorted(pends, blk_start, side='right'), MOE_EXPERTS - 1).astype(jnp.int32)
    last_e = blk_e[jnp.maximum(nused - 1, 0)]
    blk_e = jnp.where(jnp.arange(nblk) < nused, blk_e, last_e)
    xs = jnp.take(h2, row_src, axis=0)
    ys = _moe_ffn(xs, row_w[:, None], blk_e, nused.reshape(1), w1, w3, w2)
    pos = dest.reshape(T, K)
    ga = jnp.take(ys, pos[:, 0], axis=0)
    gb = jnp.take(ys, pos[:, 1], axis=0)
    return _combine(x, ga, gb, g_final, final)


def _prep_layer(l, w_in, mlstm_gate_bias, ssd_conv_w, ssd_conv_b, ssd_dt_bias, ssd_a_log, ssd_d,
                gqa_q_norm, gqa_k_norm, w_router_group, w_router_expert):
    w = w_in[l]
    wm = jnp.take(w, jnp.asarray(_MAIN_IDX), axis=1)
    dscale = jnp.ones((P_COLS,), F32).at[DF_Q:DF_Q + BRANCH_W].set((DIFF_HD ** -0.5) * LOG2E)
    wm = (wm * dscale[None, :]).astype(BF16)
    ws = jnp.zeros((D_MODEL, SMALL_COLS), F32).at[:, :32].set(jnp.take(w, jnp.asarray(_SMALL_IDX), axis=1)).astype(BF16)
    gb_row = jnp.zeros((1, SMALL_COLS), F32).at[0, :16].set(mlstm_gate_bias[l])
    dtb_row = jnp.zeros((1, LANE), F32).at[0, 16:32].set(ssd_dt_bias[l].reshape(-1))
    a_row = jnp.zeros((1, LANE), F32).at[0, 16:32].set((-jnp.exp(ssd_a_log[l])).reshape(-1))
    dskip_row = jnp.repeat(ssd_d[l], SSD_HD)[None, :]
    qg = jnp.tile(gqa_q_norm[l][jnp.asarray(_DEINT64)], LANE // GQA_HD)[None, :]
    kg = jnp.tile(gqa_k_norm[l][jnp.asarray(_DEINT64)], LANE // GQA_HD)[None, :]
    wr = jnp.zeros((D_MODEL, LANE), F32).at[:, :MOE_GROUPS].set(w_router_group[l])
    wr = wr.at[:, MOE_GROUPS:MOE_GROUPS + MOE_EXPERTS].set(w_router_expert[l])
    return dict(wm=wm, ws=ws, gb_row=gb_row, dtb_row=dtb_row, a_row=a_row, dskip_row=dskip_row,
                qg=qg, kg=kg, wr=wr, conv_w=ssd_conv_w[l], conv_b=ssd_conv_b[l][None, :])


def _bd_ones():
    i = np.arange(LANE)
    return (i[:, None] // GQA_HD == i[None, :] // GQA_HD).astype(np.float32)


def _trunk(x3, prm, L):
    B, N, _ = x3.shape
    x = x3.reshape(B * N, D_MODEL)
    cos_t, sin_t = _rope_tables(N)
    bd = jnp.asarray(_bd_ones()).astype(BF16)
    for l in range(DEPTH):
        lp = L[l]
        lambda_init = 0.8 - 0.6 * math.exp(-0.3 * l)
        p, h, small = _in_proj(x, prm["norm_mix"][l][None, :], lp["wm"], lp["ws"])
        y0 = _mlstm_branch(p, small, lp["gb_row"], prm["mlstm_norm"][l][None, :], B, N)
        y1 = _gqa_branch(p, cos_t, sin_t, lp["qg"], lp["kg"], bd, B, N)
        y2 = _ssd_branch(p, small, lp["conv_w"], lp["conv_b"], lp["dtb_row"], lp["a_row"], lp["dskip_row"],
                         prm["ssd_norm"][l][None, :], B, N)
        y3 = _diff_branch(p, prm["diff_lambda"][l], prm["diff_norm"][l][None, :], lambda_init, B, N)
        merged = _merge(h, (y0, y1, y2, y3), lp["wg"], lp["wb"])
        x, h2, route = _out_proj(x, merged, lp["wo"], prm["norm_ffn"][l][None, :], lp["wr"])
        x = _moe(x, h2, route, lp["w1"], lp["w3"], lp["w2"], prm["norm_final"][None, :], final=(l == DEPTH - 1))
    return x.reshape(B, N, D_MODEL)


def kernel(x_prompt, x_sample, norm_mix, w_in, mlstm_gate_bias, mlstm_norm, gqa_q_norm, gqa_k_norm, ssd_conv_w, ssd_conv_b, ssd_dt_bias, ssd_a_log, ssd_d, ssd_norm, diff_lambda, diff_norm, w_branch, w_gate, w_out, norm_ffn, w_router_group, w_router_expert, moe_w_gate, moe_w_up, moe_w_down, norm_final):
    prm = dict(norm_mix=norm_mix, mlstm_norm=mlstm_norm, ssd_norm=ssd_norm, diff_lambda=diff_lambda,
               diff_norm=diff_norm, norm_ffn=norm_ffn, norm_final=norm_final)
    layers = []
    for l in range(DEPTH):
        lp = _prep_layer(l, w_in, mlstm_gate_bias, ssd_conv_w, ssd_conv_b, ssd_dt_bias, ssd_a_log, ssd_d,
                         gqa_q_norm, gqa_k_norm, w_router_group, w_router_expert)
        lp["wg"] = w_gate[l].astype(BF16)
        lp["wb"] = w_branch[l].astype(BF16)
        lp["wo"] = w_out[l].astype(BF16)
        lp["w1"] = moe_w_gate[l].astype(BF16)
        lp["w3"] = moe_w_up[l].astype(BF16)
        lp["w2"] = moe_w_down[l].astype(BF16)
        layers.append(lp)
    return (_trunk(x_prompt, prm, layers), _trunk(x_sample, prm, layers))
```

```python
import functools
import math

import jax
import jax.numpy as jnp
import numpy as np
from jax import lax
from jax.experimental import pallas as pl
from jax.experimental.pallas import tpu as pltpu

F32 = jnp.float32
BF16 = jnp.bfloat16

D_MODEL = 1024
DEPTH = 2
GRID_W = 64
BRANCH_W = 512
NORM_EPS = 1e-6
MLSTM_HEADS = 4
MLSTM_HD = 128
GQA_HD = 64
GQA_HEADS = 8
GQA_KV_HEADS = 2
ROPE_THETA = 10000.0
SSD_HD = 64
SSD_HEADS = 8
SSD_GROUPS = 2
SSD_STATE = 128
SSD_CONV = 5
DIFF_HD = 64
DIFF_HEADS = 4
MOE_GROUPS = 4
MOE_EPG = 8
MOE_EXPERTS = 32
MOE_FF = 512
CHUNK = 128
LOG2E = 1.4426950408889634

_MLSTM_COLS = 4 * BRANCH_W + 4 * MLSTM_HEADS
_GQA_COLS = GQA_HEADS * GQA_HD + 2 * GQA_KV_HEADS * GQA_HD
_SSD_XBC = BRANCH_W + 2 * SSD_GROUPS * SSD_STATE
_SSD_COLS = BRANCH_W + _SSD_XBC + 2 * SSD_HEADS
_O0 = _MLSTM_COLS
_O1 = _O0 + _GQA_COLS
_O2 = _O1 + _SSD_COLS

ML_Q, ML_K, ML_V, ML_O = 0, 512, 1024, 1536
GQ_Q = 2048
SS_Z = 2560
DF_Q, DF_K, DF_V = 3072, 3584, 4096
SS_X, SS_BC = 4608, 5120
GQ_KV = 5632
P_COLS = 5888
P_PAD = 6144
SMALL_COLS = 128

LANE = 128
VMEM_LIMIT = 48 * 1024 * 1024

MOE_BLK = 256


def _cparams(sem):
    return pltpu.CompilerParams(dimension_semantics=sem, vmem_limit_bytes=VMEM_LIMIT)


def _dot(a, b):
    return jnp.dot(a, b, preferred_element_type=F32)


def _dot_nt(a, b):
    return lax.dot_general(a, b, (((1,), (1,)), ((), ())), preferred_element_type=F32)


def _dot_tn(a, b):
    return lax.dot_general(a, b, (((0,), (0,)), ((), ())), preferred_element_type=F32)


def _dot_hi(a, b):
    return jnp.dot(a, b, preferred_element_type=F32, precision=lax.Precision.HIGHEST)


def _sigmoid(x):
    return 1.0 / (1.0 + jnp.exp(-x))


def _silu(x):
    return x * _sigmoid(x)


def _log_sigmoid(x):
    return jnp.minimum(x, 0.0) - jnp.log(1.0 + jnp.exp(-jnp.abs(x)))


def _softplus(x):
    return jnp.maximum(x, 0.0) + jnp.log(1.0 + jnp.exp(-jnp.abs(x)))


def _main_col_index():
    gq = np.arange(GQA_HEADS * GQA_HD).reshape(GQA_HEADS, GQA_HD // 2, 2)
    gq = np.concatenate([gq[..., 0], gq[..., 1]], axis=-1).reshape(-1)
    gk = np.arange(GQA_KV_HEADS * GQA_HD).reshape(GQA_KV_HEADS, GQA_HD // 2, 2)
    gk = np.concatenate([gk[..., 0], gk[..., 1]], axis=-1).reshape(-1)
    segs = [
        np.arange(0, 4 * BRANCH_W),
        _O0 + gq,
        _O1 + np.arange(0, BRANCH_W),
        _O2 + np.arange(0, 3 * BRANCH_W),
        _O1 + BRANCH_W + np.arange(0, _SSD_XBC),
        _O0 + GQA_HEADS * GQA_HD + gk,
        _O0 + GQA_HEADS * GQA_HD + GQA_KV_HEADS * GQA_HD + np.arange(GQA_KV_HEADS * GQA_HD),
    ]
    idx = np.concatenate(segs)
    assert idx.shape[0] == P_COLS
    return idx


def _small_col_index():
    return np.concatenate([4 * BRANCH_W + np.arange(4 * MLSTM_HEADS),
                           _O1 + BRANCH_W + _SSD_XBC + np.arange(2 * SSD_HEADS)])


_MAIN_IDX = _main_col_index()
_SMALL_IDX = _small_col_index()
_DEINT64 = np.concatenate([np.arange(0, GQA_HD, 2), np.arange(1, GQA_HD, 2)])


def _inproj_kernel(x_ref, g_ref, w_ref, ws_ref, p_ref, h_ref, s_ref, h_scr):
    @pl.when(pl.program_id(1) == 0)
    def _():
        x = x_ref[...]
        ms = jnp.mean(x * x, axis=-1, keepdims=True)
        h = (x * lax.rsqrt(ms + NORM_EPS) * g_ref[...]).astype(BF16)
        h_scr[...] = h
        h_ref[...] = h
        s_ref[...] = _dot(h, ws_ref[...])

    p_ref[...] = _dot(h_scr[...], w_ref[...]).astype(BF16)


def _in_proj(x, g, w_main, w_small, tm=1024, tn=1024):
    T = x.shape[0]
    return pl.pallas_call(
        _inproj_kernel,
        grid=(T // tm, P_PAD // tn),
        in_specs=[
            pl.BlockSpec((tm, D_MODEL), lambda i, j: (i, 0)),
            pl.BlockSpec((1, D_MODEL), lambda i, j: (0, 0)),
            pl.BlockSpec((D_MODEL, tn), lambda i, j: (0, j)),
            pl.BlockSpec((D_MODEL, SMALL_COLS), lambda i, j: (0, 0)),
        ],
        out_specs=[
            pl.BlockSpec((tm, tn), lambda i, j: (i, j)),
            pl.BlockSpec((tm, D_MODEL), lambda i, j: (i, 0)),
            pl.BlockSpec((tm, SMALL_COLS), lambda i, j: (i, 0)),
        ],
        out_shape=[
            jax.ShapeDtypeStruct((T, P_PAD), BF16),
            jax.ShapeDtypeStruct((T, D_MODEL), BF16),
            jax.ShapeDtypeStruct((T, SMALL_COLS), F32),
        ],
        scratch_shapes=[pltpu.VMEM((tm, D_MODEL), BF16)],
        compiler_params=_cparams(("parallel", "arbitrary")),
        name="in_proj",
    )(x, g, w_main, w_small)


def _tri_masks(L, reverse):
    r = lax.broadcasted_iota(jnp.int32, (L, L), 0)
    c = lax.broadcasted_iota(jnp.int32, (L, L), 1)
    return (c >= r) if reverse else (c <= r)


def _mlstm_kernel(*refs, reverse, final):
    if final:
        (q_ref, k_ref, v_ref, sm_ref, gb_ref, o_ref, hb_ref, ng_ref,
         out_ref, ct_scr, n_scr, m_scr) = refs
    else:
        q_ref, k_ref, v_ref, sm_ref, gb_ref, out_ref, ct_scr, n_scr, m_scr = refs
    L = CHUNK
    H = MLSTM_HEADS
    d = MLSTM_HD

    @pl.when(pl.program_id(1) == 0)
    def _():
        ct_scr[...] = jnp.zeros_like(ct_scr)
        n_scr[...] = jnp.zeros_like(n_scr)
        m_scr[...] = jnp.zeros_like(m_scr)

    mask = _tri_masks(L, reverse)
    tri = mask.astype(F32)
    gates = sm_ref[...] + gb_ref[...]
    logf = _log_sigmoid(gates)
    bcum = _dot_hi(tri, logf)
    gates_t = gates.T
    bcum_t = bcum.T
    i_off = 8 if reverse else 0
    f_off = i_off + 4
    edge = 0 if reverse else L - 1
    scale = d ** -0.5
    m_all = m_scr[...]

    hs = []
    for h in range(H):
        qh = q_ref[:, h * d:(h + 1) * d]
        ks = (k_ref[:, h * d:(h + 1) * d].astype(F32) * scale).astype(BF16)
        vh = v_ref[:, h * d:(h + 1) * d]
        b_col = bcum[:, f_off + h:f_off + h + 1]
        i_col = gates[:, i_off + h:i_off + h + 1]
        a_row = gates_t[i_off + h:i_off + h + 1, :] - bcum_t[f_off + h:f_off + h + 1, :]
        m_prev = m_all[:, h:h + 1]
        dmat = jnp.where(mask, b_col + a_row, -jnp.inf)
        inter = b_col + m_prev
        m_t = jnp.maximum(inter, jnp.max(dmat, axis=1, keepdims=True))
        w_intra = jnp.exp(dmat - m_t)
        w_inter = jnp.exp(inter - m_t)
        s = _dot_nt(qh, ks) * w_intra
        ct = ct_scr[h]
        n_row = n_scr[h]
        num = _dot(s.astype(BF16), vh) + w_inter * _dot(qh, ct.astype(BF16))
        qn = jnp.sum(qh.astype(F32) * n_row, axis=1, keepdims=True)
        den = jnp.sum(s, axis=1, keepdims=True) + w_inter * qn
        hs.append(num / jnp.maximum(jnp.abs(den), jnp.exp(-m_t)))
        btot = b_col[edge:edge + 1, :]
        dec = btot - b_col + i_col
        m_new = jnp.maximum(btot + m_prev, jnp.max(dec, axis=0, keepdims=True))
        ws = jnp.exp(dec - m_new)
        wc = jnp.exp(btot + m_prev - m_new)
        wsv = (ws * vh.astype(F32)).astype(BF16)
        ct_scr[h] = wc * ct + _dot_tn(ks, wsv)
        n_scr[h] = wc * n_row + jnp.sum(ws * ks.astype(F32), axis=0, keepdims=True)
        lane = lax.broadcasted_iota(jnp.int32, (1, LANE), 1)
        m_all = jnp.where(lane == h, m_new, m_all)
    m_scr[...] = m_all

    if not final:
        for h in range(H):
            out_ref[:, h * d:(h + 1) * d] = hs[h]
    else:
        for h in range(H):
            hsum = hs[h] + hb_ref[:, h * d:(h + 1) * d]
            ms = jnp.mean(hsum * hsum, axis=1, keepdims=True)
            y = hsum * lax.rsqrt(ms + NORM_EPS) * ng_ref[:, h * d:(h + 1) * d]
            o = o_ref[:, h * d:(h + 1) * d].astype(F32)
            out_ref[:, h * d:(h + 1) * d] = (_sigmoid(o) * y).astype(BF16)


def _mlstm_dir(p, small, gate_bias_row, B, N, reverse, hb=None, norm_g=None):
    L = CHUNK
    nc = N // L
    T = B * N
    final = hb is not None

    def row(b, c):
        return b * nc + ((nc - 1 - c) if reverse else c)

    W = BRANCH_W
    in_specs = [
        pl.BlockSpec((L, W), lambda b, c: (row(b, c), ML_Q // W)),
        pl.BlockSpec((L, W), lambda b, c: (row(b, c), ML_K // W)),
        pl.BlockSpec((L, W), lambda b, c: (row(b, c), ML_V // W)),
        pl.BlockSpec((L, SMALL_COLS), lambda b, c: (row(b, c), 0)),
        pl.BlockSpec((1, SMALL_COLS), lambda b, c: (0, 0)),
    ]
    args = [p, p, p, small, gate_bias_row]
    if final:
        in_specs += [
            pl.BlockSpec((L, W), lambda b, c: (row(b, c), ML_O // W)),
            pl.BlockSpec((L, W), lambda b, c: (row(b, c), 0)),
            pl.BlockSpec((1, W), lambda b, c: (0, 0)),
        ]
        args += [p, hb, norm_g]
    return pl.pallas_call(
        functools.partial(_mlstm_kernel, reverse=reverse, final=final),
        grid=(B, nc),
        in_specs=in_specs,
        out_specs=pl.BlockSpec((L, W), lambda b, c: (row(b, c), 0)),
        out_shape=jax.ShapeDtypeStruct((T, W), BF16 if final else F32),
        scratch_shapes=[
            pltpu.VMEM((MLSTM_HEADS, MLSTM_HD, MLSTM_HD), F32),
            pltpu.VMEM((MLSTM_HEADS, 1, MLSTM_HD), F32),
            pltpu.VMEM((1, LANE), F32),
        ],
        compiler_params=_cparams(("parallel", "arbitrary")),
        name="mlstm_bwd" if reverse else "mlstm_fwd",
    )(*args)


def _mlstm_branch(p, small, gate_bias_row, norm_g, B, N):
    hb = _mlstm_dir(p, small, gate_bias_row, B, N, reverse=True)
    return _mlstm_dir(p, small, gate_bias_row, B, N, reverse=False, hb=hb, norm_g=norm_g)


def _gqa_prep_kernel(q_ref, kv_ref, cos_ref, sin_ref, qg_ref, kg_ref, bd_ref, qo_ref, ko_ref, vo_ref):
    tm = q_ref.shape[0]
    bd = bd_ref[...]
    cos = cos_ref[...]
    sin = sin_ref[...]
    lane = lax.broadcasted_iota(jnp.int32, (tm, LANE), 1)
    first = (lane % GQA_HD) < (GQA_HD // 2)
    lo = lane < GQA_HD

    def norm_rope(x, g):
        ss = _dot((x * x).astype(BF16), bd)
        xn = x * lax.rsqrt(ss * (1.0 / GQA_HD) + NORM_EPS) * g
        partner = jnp.where(first, pltpu.roll(xn, LANE - GQA_HD // 2, axis=1),
                            pltpu.roll(xn, GQA_HD // 2, axis=1))
        return xn * cos + partner * sin

    qscale = (GQA_HD ** -0.5) * LOG2E
    for a in range(GQA_HEADS * GQA_HD // LANE):
        x = q_ref[:, a * LANE:(a + 1) * LANE].astype(F32)
        qo_ref[:, a * LANE:(a + 1) * LANE] = (norm_rope(x, qg_ref[...]) * qscale).astype(BF16)
    k = norm_rope(kv_ref[:, 0:LANE].astype(F32), kg_ref[...])
    k_sw = pltpu.roll(k, GQA_HD, axis=1)
    ko_ref[:, 0:LANE] = jnp.where(lo, k, k_sw).astype(BF16)
    ko_ref[:, LANE:2 * LANE] = jnp.where(lo, k_sw, k).astype(BF16)
    v = kv_ref[:, LANE:2 * LANE].astype(F32)
    v_sw = pltpu.roll(v, GQA_HD, axis=1)
    vo_ref[:, 0:LANE] = jnp.where(lo, v, 1.0).astype(BF16)
    vo_ref[:, LANE:2 * LANE] = jnp.where(lo, v_sw, 1.0).astype(BF16)


def _gqa_prep(p, cos_t, sin_t, qg, kg, bd, B, N, tm=512):
    T = B * N
    nb = N // tm
    return pl.pallas_call(
        _gqa_prep_kernel,
        grid=(T // tm,),
        in_specs=[
            pl.BlockSpec((tm, 512), lambda i: (i, GQ_Q // 512)),
            pl.BlockSpec((tm, 256), lambda i: (i, GQ_KV // 256)),
            pl.BlockSpec((tm, LANE), lambda i: (i % nb, 0)),
            pl.BlockSpec((tm, LANE), lambda i: (i % nb, 0)),
            pl.BlockSpec((1, LANE), lambda i: (0, 0)),
            pl.BlockSpec((1, LANE), lambda i: (0, 0)),
            pl.BlockSpec((LANE, LANE), lambda i: (0, 0)),
        ],
        out_specs=[
            pl.BlockSpec((tm, 512), lambda i: (i, 0)),
            pl.BlockSpec((tm, 256), lambda i: (i, 0)),
            pl.BlockSpec((tm, 256), lambda i: (i, 0)),
        ],
        out_shape=[
            jax.ShapeDtypeStruct((T, 512), BF16),
            jax.ShapeDtypeStruct((T, 256), BF16),
            jax.ShapeDtypeStruct((T, 256), BF16),
        ],
        compiler_params=_cparams(("parallel",)),
        name="gqa_prep",
    )(p, p, cos_t, sin_t, qg, kg, bd)


FLASH_TKB = 512


def _softmax_keys(s_ref, p_ref, m_scr, al_scr, l_scr, M):
    tkb = s_ref.shape[0]
    for c in range(M // LANE):
        cols = slice(c * LANE, (c + 1) * LANE)
        mx = jnp.max(jnp.max(s_ref[:, cols].reshape(tkb // 8, 8, LANE), axis=0), axis=0, keepdims=True)
        m_prev = m_scr[:, cols]
        m_new = jnp.maximum(m_prev, mx)
        alpha = jnp.exp2(m_prev - m_new)
        p = jnp.exp2(s_ref[:, cols] - m_new)
        p_ref[:, cols] = p.astype(BF16)
        if l_scr is not None:
            lsum = jnp.sum(jnp.sum(p.reshape(tkb // 8, 8, LANE), axis=0), axis=0, keepdims=True)
            l_scr[:, cols] = alpha * l_scr[:, cols] + lsum
        al_scr[:, cols] = alpha
        m_scr[:, cols] = m_new


def _flash_blocks(nb, qk, pv, softmax, fix=None):
    qk(0)
    for i in range(nb):
        if i + 1 < nb:
            qk(i + 1)
        if fix is not None:
            fix(i)
        softmax(i)
        pv(i)


def _gqa_flash_kernel(q_ref, k_ref, v_ref, o_ref, qs_scr, m_scr, al_scr, acc_scr, s_scr, p_scr):
    tq = q_ref.shape[0]
    M = 4 * tq
    tkb = s_scr.shape[1]
    nb = k_ref.shape[0] // tkb
    kj = pl.program_id(3)

    @pl.when(kj == 0)
    def _():
        lane = lax.broadcasted_iota(jnp.int32, (tq, LANE), 1)
        lo = lane < GQA_HD
        zero = jnp.zeros((tq, LANE), BF16)
        for a in range(2):
            qa = q_ref[:, a * LANE:(a + 1) * LANE]
            qs_scr[(2 * a) * tq:(2 * a + 1) * tq, :] = jnp.where(lo, qa, zero)
            qs_scr[(2 * a + 1) * tq:(2 * a + 2) * tq, :] = jnp.where(lo, zero, qa)
        m_scr[...] = jnp.full_like(m_scr, -jnp.inf)
        acc_scr[...] = jnp.zeros_like(acc_scr)

    def qk(i):
        s_scr[i % 2] = _dot_nt(k_ref[i * tkb:(i + 1) * tkb, :], qs_scr[...])

    def softmax(i):
        _softmax_keys(s_scr.at[i % 2], p_scr.at[i % 2], m_scr, al_scr, None, M)

    def pv(i):
        acc_scr[...] = al_scr[...] * acc_scr[...] + _dot_tn(v_ref[i * tkb:(i + 1) * tkb, :], p_scr[i % 2])

    _flash_blocks(nb, qk, pv, softmax)

    @pl.when(kj == pl.num_programs(3) - 1)
    def _():
        inv = 1.0 / acc_scr[GQA_HD:GQA_HD + 1, :]
        ot = jnp.concatenate([acc_scr[0:GQA_HD, h * tq:(h + 1) * tq] * inv[:, h * tq:(h + 1) * tq]
                              for h in range(4)], axis=0)
        o_ref[...] = ot.T.astype(BF16)


def _gqa_flash(qn, k2, va, B, N, tq=256, tk=2048):
    T = B * N
    nq, nk = N // tq, N // tk
    M = 4 * tq
    tkb = FLASH_TKB
    return pl.pallas_call(
        _gqa_flash_kernel,
        grid=(B, GQA_KV_HEADS, nq, nk),
        in_specs=[
            pl.BlockSpec((tq, 256), lambda b, g, i, j: (b * nq + i, g)),
            pl.BlockSpec((tk, LANE), lambda b, g, i, j: (b * nk + j, g)),
            pl.BlockSpec((tk, LANE), lambda b, g, i, j: (b * nk + j, g)),
        ],
        out_specs=pl.BlockSpec((tq, 256), lambda b, g, i, j: (b * nq + i, g)),
        out_shape=jax.ShapeDtypeStruct((T, 512), BF16),
        scratch_shapes=[
            pltpu.VMEM((M, LANE), BF16),
            pltpu.VMEM((1, M), F32),
            pltpu.VMEM((1, M), F32),
            pltpu.VMEM((LANE, M), F32),
            pltpu.VMEM((2, tkb, M), F32),
            pltpu.VMEM((2, tkb, M), BF16),
        ],
        compiler_params=_cparams(("parallel", "parallel", "parallel", "arbitrary")),
        name="gqa_flash",
    )(qn, k2, va)


def _rope_tables(N):
    rows = N // GRID_W
    row = jnp.repeat(jnp.arange(rows, dtype=F32), GRID_W)
    col = (jnp.arange(N) % GRID_W).astype(F32)
    half = GQA_HD // 2
    inv = 1.0 / (ROPE_THETA ** (jnp.arange(0, half, 2, dtype=F32) / half))
    ang = jnp.concatenate([row[:, None] * inv, col[:, None] * inv], axis=-1)
    cos, sin = jnp.cos(ang), jnp.sin(ang)
    cos_h = jnp.concatenate([cos, cos], axis=-1)
    sin_h = jnp.concatenate([-sin, sin], axis=-1)
    return jnp.tile(cos_h, (1, LANE // GQA_HD)), jnp.tile(sin_h, (1, LANE // GQA_HD))


def _gqa_branch(p, cos_t, sin_t, qg, kg, bd, B, N):
    qn, k2, va = _gqa_prep(p, cos_t, sin_t, qg, kg, bd, B, N)
    return _gqa_flash(qn, k2, va, B, N)


DIFF_SKIP_LOG2 = 40.0


def _alibi_slope_log2(h):
    return (2.0 ** (-8.0 * (h + 1) / DIFF_HEADS)) * LOG2E


def _alibi_tables(N):
    pos = np.arange(N)
    hi = ((pos // 128) * 128).astype(np.float32)
    lo = (pos % 128).astype(np.float32)
    kt = np.zeros((DIFF_HEADS, N, LANE), np.float32)
    qt = np.zeros((DIFF_HEADS, N, LANE), np.float32)
    for h in range(DIFF_HEADS):
        s = np.float32(_alibi_slope_log2(h))
        s1 = np.float32(s).astype(BF16).astype(np.float32)
        s2 = np.float32(s - s1).astype(BF16).astype(np.float32)
        for c, val in enumerate((hi, hi, lo, lo)):
            kt[h, :, c] = val
            qt[h, :, 4 + c] = -val
        for c, val in enumerate((s1, s2, s1, s2)):
            kt[h, :, 4 + c] = val
            qt[h, :, c] = val
    return (jnp.asarray(kt.reshape(DIFF_HEADS * N, LANE), BF16), jnp.asarray(qt.reshape(DIFF_HEADS * N, LANE), BF16))


def _diff_flash_kernel(lo_ref, hi_ref, q_ref, qa_ref, k_ref, ka_ref, v_ref, lp_ref, ng_ref, o_ref,
                       qs_scr, m_scr, l_scr, al_scr, acc_scr, s_scr, p_scr, *, lambda_init):
    tq = q_ref.shape[0]
    tk = k_ref.shape[0]
    M = 2 * tq
    tkb = s_scr.shape[1]
    nb = tk // tkb
    b, h, qi, kj = pl.program_id(0), pl.program_id(1), pl.program_id(2), pl.program_id(3)
    idx = (b * pl.num_programs(1) + h) * pl.num_programs(2) + qi
    first, last = lo_ref[idx], hi_ref[idx]

    @pl.when(kj == first)
    def _():
        lane = lax.broadcasted_iota(jnp.int32, (tq, LANE), 1)
        lo = lane < DIFF_HD
        zero = jnp.zeros((tq, LANE), BF16)
        q = q_ref[...]
        qa = qa_ref[...]
        for var, aug in enumerate((qa, -qa)):
            qs_scr[var, 0:tq, 0:LANE] = jnp.where(lo, q, zero)
            qs_scr[var, tq:M, 0:LANE] = jnp.where(lo, zero, q)
            qs_scr[var, 0:tq, LANE:2 * LANE] = aug
            qs_scr[var, tq:M, LANE:2 * LANE] = aug
        m_scr[...] = jnp.full_like(m_scr, -jnp.inf)
        l_scr[...] = jnp.zeros_like(l_scr)
        acc_scr[...] = jnp.zeros_like(acc_scr)

    is_left = (kj + 1) * tk <= qi * tq
    is_right = kj * tk >= (qi + 1) * tq
    active = jnp.logical_and(kj >= first, kj <= last)
    sel = jnp.where(is_right, 1, 0)

    def qk(i):
        rows = slice(i * tkb, (i + 1) * tkb)
        kk = jnp.concatenate([k_ref[rows, :], ka_ref[rows, :]], axis=1)
        s_scr[i % 2] = _dot_nt(kk, qs_scr[sel])

    def fix(i):
        slope = jnp.exp2(-2.0 * (jnp.full((1, 1), h, jnp.int32).astype(F32) + 1.0)) * LOG2E
        j = kj * tk + i * tkb + lax.broadcasted_iota(jnp.int32, (tkb, 1), 0)
        col = lax.broadcasted_iota(jnp.int32, (1, M), 1)
        iq = qi * tq + col - jnp.where(col >= tq, tq, 0)
        d = jnp.maximum(j - iq, 0).astype(F32)
        s_scr[i % 2] = s_scr[i % 2] - (2.0 * slope) * d

    def softmax(i):
        _softmax_keys(s_scr.at[i % 2], p_scr.at[i % 2], m_scr, al_scr, l_scr, M)

    def pv(i):
        acc_scr[...] = al_scr[...] * acc_scr[...] + _dot_tn(v_ref[i * tkb:(i + 1) * tkb, :], p_scr[i % 2])

    @pl.when(jnp.logical_and(active, jnp.logical_or(is_left, is_right)))
    def _():
        _flash_blocks(nb, qk, pv, softmax)

    @pl.when(jnp.logical_and(active, jnp.logical_not(jnp.logical_or(is_left, is_right))))
    def _():
        _flash_blocks(nb, qk, pv, softmax, fix=fix)

    @pl.when(kj == last)
    def _():
        lp = lp_ref[...]
        s01 = jnp.sum(jnp.sum(lp[0:1] * lp[1:2], axis=1, keepdims=True), axis=0, keepdims=True)
        s23 = jnp.sum(jnp.sum(lp[2:3] * lp[3:4], axis=1, keepdims=True), axis=0, keepdims=True)
        lam = jnp.exp(s01) - jnp.exp(s23) + lambda_init
        inv = 1.0 / l_scr[...]
        ot = acc_scr[:, 0:tq] * inv[:, 0:tq] - lam * (acc_scr[:, tq:M] * inv[:, tq:M])
        o = ot.T
        ms = jnp.mean(o * o, axis=1, keepdims=True)
        o_ref[...] = (o * lax.rsqrt(ms + NORM_EPS) * ng_ref[...] * (1.0 - lambda_init)).astype(BF16)


def _diff_bands(p, B, N, tq, tk):
    nq, nkt = N // tq, N // tk

    def norms(off):
        x = p[:, off:off + BRANCH_W].astype(F32).reshape(B, N, DIFF_HEADS, 2, DIFF_HD)
        return jnp.max(jnp.sqrt(jnp.sum(x * x, axis=-1)), axis=-1)

    qn = norms(DF_Q)
    kn = norms(DF_K)
    qmax = jnp.max(qn.reshape(B, nq, tq, DIFF_HEADS), axis=2)
    kself = jnp.max(kn.reshape(B, nq, tq, DIFF_HEADS), axis=2)
    kmax = jnp.max(kn.reshape(B, nkt, tk, DIFF_HEADS), axis=2)
    bound = qmax[:, :, None, :] * (kmax[:, None, :, :] + kself[:, :, None, :]) * 1.001 + DIFF_SKIP_LOG2
    q0 = np.arange(nq)[:, None] * tq
    k0 = np.arange(nkt)[None, :] * tk
    dist = np.maximum(0, np.maximum(q0 - (k0 + tk - 1), k0 - (q0 + tq - 1))).astype(np.float32)
    slopes = np.array([_alibi_slope_log2(h) for h in range(DIFF_HEADS)], np.float32)
    keep = bound > jnp.asarray(dist)[None, :, :, None] * jnp.asarray(slopes)[None, None, None, :]
    keep = jnp.transpose(keep, (0, 3, 1, 2))
    first = jnp.argmax(keep, axis=-1).astype(jnp.int32)
    last = (nkt - 1 - jnp.argmax(keep[..., ::-1], axis=-1)).astype(jnp.int32)
    return first.reshape(-1), last.reshape(-1)


def _diff_branch(p, ktab, qtab, lam_params, norm_g, lambda_init, B, N, tq=512, tk=2048):
    T = B * N
    nq, nk = N // tq, N // tk
    M = 2 * tq
    tkb = FLASH_TKB
    first, last = _diff_bands(p, B, N, tq, tk)

    def kstep(b, h, i, j, lo, hi):
        idx = (b * DIFF_HEADS + h) * nq + i
        return jnp.minimum(jnp.maximum(j, lo[idx]), hi[idx])

    return pl.pallas_call(
        functools.partial(_diff_flash_kernel, lambda_init=lambda_init),
        grid_spec=pltpu.PrefetchScalarGridSpec(
            num_scalar_prefetch=2,
            grid=(B, DIFF_HEADS, nq, nk),
            in_specs=[
                pl.BlockSpec((tq, LANE), lambda b, h, i, j, lo, hi: (b * nq + i, DF_Q // LANE + h)),
                pl.BlockSpec((tq, LANE), lambda b, h, i, j, lo, hi: (h * nq + i, 0)),
                pl.BlockSpec((tk, LANE), lambda b, h, i, j, lo, hi: (b * nk + kstep(b, h, i, j, lo, hi), DF_K // LANE + h)),
                pl.BlockSpec((tk, LANE), lambda b, h, i, j, lo, hi: (h * nk + kstep(b, h, i, j, lo, hi), 0)),
                pl.BlockSpec((tk, LANE), lambda b, h, i, j, lo, hi: (b * nk + kstep(b, h, i, j, lo, hi), DF_V // LANE + h)),
                pl.BlockSpec((4, DIFF_HD), lambda b, h, i, j, lo, hi: (0, 0)),
                pl.BlockSpec((1, LANE), lambda b, h, i, j, lo, hi: (0, h)),
            ],
            out_specs=pl.BlockSpec((tq, LANE), lambda b, h, i, j, lo, hi: (b * nq + i, h)),
            scratch_shapes=[
                pltpu.VMEM((2, M, 2 * LANE), BF16),
                pltpu.VMEM((1, M), F32),
                pltpu.VMEM((1, M), F32),
                pltpu.VMEM((1, M), F32),
                pltpu.VMEM((LANE, M), F32),
                pltpu.VMEM((2, tkb, M), F32),
                pltpu.VMEM((2, tkb, M), BF16),
            ],
        ),
        out_shape=jax.ShapeDtypeStruct((T, BRANCH_W), BF16),
        compiler_params=_cparams(("parallel", "parallel", "parallel", "arbitrary")),
        name="diff_flash",
    )(first, last, p, qtab, p, ktab, p, lam_params, norm_g)


SSD_HALO = 16


def _ssd_prep_kernel(x_ref, xp_ref, xn_ref, bc_ref, bcp_ref, bcn_ref, sm_ref, cw_ref, cb_ref,
                     dtb_ref, arow_ref, xo_ref, bco_ref, dto_ref):
    tc = x_ref.shape[0]
    c = pl.program_id(1)
    nc = pl.num_programs(1)
    has_prev = jnp.where(c > 0, 1.0, 0.0)
    has_next = jnp.where(c < nc - 1, 1.0, 0.0)
    pad = SSD_CONV // 2

    def conv(main_ref, prev_ref, next_ref, off):
        xf = jnp.concatenate([prev_ref[...].astype(F32) * has_prev, main_ref[...].astype(F32),
                              next_ref[...].astype(F32) * has_next], axis=0)
        n = xf.shape[0]
        acc = jnp.zeros((tc, xf.shape[1]), F32) + cb_ref[:, off:off + xf.shape[1]]
        for j in range(SSD_CONV):
            sh = pltpu.roll(xf, (pad - j) % n, axis=0)[SSD_HALO:SSD_HALO + tc, :]
            acc = acc + sh * cw_ref[j:j + 1, off:off + xf.shape[1]]
        return _silu(acc)

    xo_ref[...] = conv(x_ref, xp_ref, xn_ref, 0).astype(BF16)
    bco_ref[...] = conv(bc_ref, bcp_ref, bcn_ref, BRANCH_W).astype(BF16)
    dt = _softplus(sm_ref[...] + dtb_ref[...])
    a = pltpu.roll(dt * arow_ref[...], LANE - 16, axis=1)
    lane = lax.broadcasted_iota(jnp.int32, (tc, LANE), 1)
    dto_ref[...] = jnp.where(lane < 16, a, jnp.where(lane < 32, dt, 0.0))


def _ssd_prep(p, small, conv_w, conv_b, dtb_row, a_row, B, N, tc=512):
    T = B * N
    nc = N // tc
    hb = tc // SSD_HALO
    W = BRANCH_W

    def main(col):
        return pl.BlockSpec((tc, W), lambda b, c: (b * nc + c, col))

    def prev(col):
        return pl.BlockSpec((SSD_HALO, W), lambda b, c: (jnp.maximum((b * nc + c) * hb - 1, 0), col))

    def nxt(col):
        return pl.BlockSpec((SSD_HALO, W), lambda b, c: (jnp.minimum((b * nc + c + 1) * hb, T // SSD_HALO - 1), col))

    cx, cbc = SS_X // W, SS_BC // W
    return pl.pallas_call(
        _ssd_prep_kernel,
        grid=(B, nc),
        in_specs=[
            main(cx), prev(cx), nxt(cx), main(cbc), prev(cbc), nxt(cbc),
            pl.BlockSpec((tc, SMALL_COLS), lambda b, c: (b * nc + c, 0)),
            pl.BlockSpec((SSD_CONV, 2 * W), lambda b, c: (0, 0)),
            pl.BlockSpec((1, 2 * W), lambda b, c: (0, 0)),
            pl.BlockSpec((1, LANE), lambda b, c: (0, 0)),
            pl.BlockSpec((1, LANE), lambda b, c: (0, 0)),
        ],
        out_specs=[
            pl.BlockSpec((tc, W), lambda b, c: (b * nc + c, 0)),
            pl.BlockSpec((tc, W), lambda b, c: (b * nc + c, 0)),
            pl.BlockSpec((tc, LANE), lambda b, c: (b * nc + c, 0)),
        ],
        out_shape=[
            jax.ShapeDtypeStruct((T, W), BF16),
            jax.ShapeDtypeStruct((T, W), BF16),
            jax.ShapeDtypeStruct((T, LANE), F32),
        ],
        compiler_params=_cparams(("parallel", "parallel")),
        name="ssd_prep",
    )(p, p, p, p, p, p, small, conv_w, conv_b, dtb_row, a_row)


def _ssd_kernel(*refs, reverse, final):
    if final:
        (x_ref, bc_ref, dta_ref, ea_ref, ed_ref, z_ref, yb_ref, dsk_ref, ng_ref,
         out_ref, hs_scr) = refs
    else:
        x_ref, bc_ref, dta_ref, ea_ref, ed_ref, out_ref, hs_scr = refs
    L = CHUNK
    S = SSD_STATE
    W = BRANCH_W
    GW = W // SSD_GROUPS

    @pl.when(pl.program_id(1) == 0)
    def _():
        hs_scr[...] = jnp.zeros_like(hs_scr)

    mask = _tri_masks(L, reverse)
    tri = mask.astype(F32)
    dta = dta_ref[...]
    acs = _dot_hi(tri, dta)
    acs_t = acs.T
    acs_e = _dot_hi(acs, ea_ref[...])
    dt_e = _dot_hi(dta, ed_ref[...])
    edge = 0 if reverse else L - 1
    a_off = 8 if reverse else 0
    acs_end = acs_e[edge:edge + 1, :]
    xf = x_ref[...].astype(F32)
    xdt = xf * dt_e
    xdt_b = xdt.astype(BF16)
    xdend = (xdt * jnp.exp(acs_end - acs_e)).astype(BF16)
    e_acs = jnp.exp(acs_e)
    cdec = jnp.exp(acs_end)
    lane = lax.broadcasted_iota(jnp.int32, (L, LANE), 1)
    lo = lane < SSD_HD
    zero = jnp.zeros((L, LANE), BF16)

    ys = []
    for g in range(SSD_GROUPS):
        bg = bc_ref[:, g * S:(g + 1) * S]
        cg = bc_ref[:, SSD_GROUPS * S + g * S:SSD_GROUPS * S + (g + 1) * S]
        cb = _dot_nt(cg, bg)
        hs = hs_scr[g]
        y_off = _dot(cg, hs.astype(BF16)) * e_acs[:, g * GW:(g + 1) * GW]
        hs_scr[g] = cdec[:, g * GW:(g + 1) * GW] * hs + _dot_tn(bg, xdend[:, g * GW:(g + 1) * GW])
        for pr in range(GW // LANE):
            col = g * GW + pr * LANE
            xp = xdt_b[:, col:col + LANE]
            yd = jnp.zeros((L, LANE), F32)
            for hh in range(2):
                hd = (col // SSD_HD) + hh
                a_col = acs[:, a_off + hd:a_off + hd + 1]
                a_row = acs_t[a_off + hd:a_off + hd + 1, :]
                decay = jnp.exp(jnp.where(mask, a_col - a_row, -jnp.inf))
                wm = (cb * decay).astype(BF16)
                xm = jnp.where(lo, xp, zero) if hh == 0 else jnp.where(lo, zero, xp)
                yd = yd + _dot(wm, xm)
            ys.append(yd + y_off[:, pr * LANE:(pr + 1) * LANE])
    y = jnp.concatenate(ys, axis=1)

    if not final:
        out_ref[...] = y
    else:
        y = y + yb_ref[...] + dsk_ref[...] * xf
        y = y * _silu(z_ref[...].astype(F32))
        outs = []
        for g in range(SSD_GROUPS):
            yg = y[:, g * GW:(g + 1) * GW]
            ms = jnp.mean(yg * yg, axis=1, keepdims=True)
            outs.append(yg * lax.rsqrt(ms + NORM_EPS))
        out_ref[...] = (jnp.concatenate(outs, axis=1) * ng_ref[...]).astype(BF16)


def _ssd_dir(xs, bc, dta, ea, ed, B, N, reverse, p=None, yb=None, dskip_row=None, norm_g=None):
    L = CHUNK
    nc = N // L
    T = B * N
    W = BRANCH_W
    final = yb is not None

    def row(b, c):
        return b * nc + ((nc - 1 - c) if reverse else c)

    in_specs = [
        pl.BlockSpec((L, W), lambda b, c: (row(b, c), 0)),
        pl.BlockSpec((L, W), lambda b, c: (row(b, c), 0)),
        pl.BlockSpec((L, LANE), lambda b, c: (row(b, c), 0)),
        pl.BlockSpec((LANE, W), lambda b, c: (0, 0)),
        pl.BlockSpec((LANE, W), lambda b, c: (0, 0)),
    ]
    args = [xs, bc, dta, ea, ed]
    if final:
        in_specs += [
            pl.BlockSpec((L, W), lambda b, c: (row(b, c), SS_Z // W)),
            pl.BlockSpec((L, W), lambda b, c: (row(b, c), 0)),
            pl.BlockSpec((1, W), lambda b, c: (0, 0)),
            pl.BlockSpec((1, W), lambda b, c: (0, 0)),
        ]
        args += [p, yb, dskip_row, norm_g]
    return pl.pallas_call(
        functools.partial(_ssd_kernel, reverse=reverse, final=final),
        grid=(B, nc),
        in_specs=in_specs,
        out_specs=pl.BlockSpec((L, W), lambda b, c: (row(b, c), 0)),
        out_shape=jax.ShapeDtypeStruct((T, W), BF16 if final else F32),
        scratch_shapes=[pltpu.VMEM((SSD_GROUPS, SSD_STATE, W // SSD_GROUPS), F32)],
        compiler_params=_cparams(("parallel", "arbitrary")),
        name="ssd_bwd" if reverse else "ssd_fwd",
    )(*args)


def _ssd_expand_mats():
    ea_f = np.zeros((LANE, BRANCH_W), np.float32)
    ea_b = np.zeros((LANE, BRANCH_W), np.float32)
    ed_f = np.zeros((LANE, BRANCH_W), np.float32)
    ed_b = np.zeros((LANE, BRANCH_W), np.float32)
    for h in range(SSD_HEADS):
        ea_f[h, h * SSD_HD:(h + 1) * SSD_HD] = 1.0
        ea_b[8 + h, h * SSD_HD:(h + 1) * SSD_HD] = 1.0
        ed_f[16 + h, h * SSD_HD:(h + 1) * SSD_HD] = 1.0
        ed_b[24 + h, h * SSD_HD:(h + 1) * SSD_HD] = 1.0
    return ea_f, ea_b, ed_f, ed_b


_EA_F, _EA_B, _ED_F, _ED_B = _ssd_expand_mats()


def _ssd_branch(p, small, conv_w, conv_b, dtb_row, a_row, dskip_row, norm_g, B, N):
    xs, bc, dta = _ssd_prep(p, small, conv_w, conv_b, dtb_row, a_row, B, N)
    yb = _ssd_dir(xs, bc, dta, jnp.asarray(_EA_B), jnp.asarray(_ED_B), B, N, reverse=True)
    return _ssd_dir(xs, bc, dta, jnp.asarray(_EA_F), jnp.asarray(_ED_F), B, N, reverse=False,
                    p=p, yb=yb, dskip_row=dskip_row, norm_g=norm_g)


def _merge_kernel(h_ref, y0_ref, y1_ref, y2_ref, y3_ref, wg_ref, wb_ref, o_ref):
    h = h_ref[...]
    acc = None
    for i, y_ref in enumerate((y0_ref, y1_ref, y2_ref, y3_ref)):
        gate = _sigmoid(_dot(h, wg_ref[i]))
        term = gate * _dot(y_ref[...], wb_ref[i])
        acc = term if acc is None else acc + term
    o_ref[...] = acc.astype(BF16)


def _merge(h, ys, wg, wb, tm=1024, tn=256):
    T = h.shape[0]
    return pl.pallas_call(
        _merge_kernel,
        grid=(T // tm, D_MODEL // tn),
        in_specs=[pl.BlockSpec((tm, D_MODEL), lambda i, j: (i, 0))]
        + [pl.BlockSpec((tm, BRANCH_W), lambda i, j: (i, 0))] * 4
        + [
            pl.BlockSpec((4, D_MODEL, tn), lambda i, j: (0, 0, j)),
            pl.BlockSpec((4, BRANCH_W, tn), lambda i, j: (0, 0, j)),
        ],
        out_specs=pl.BlockSpec((tm, tn), lambda i, j: (i, j)),
        out_shape=jax.ShapeDtypeStruct((T, D_MODEL), BF16),
        compiler_params=_cparams(("parallel", "arbitrary")),
        name="merge",
    )(h, *ys, wg, wb)


def _outproj_kernel(x_ref, m_ref, wo_ref, g_ref, wr_ref, xo_ref, h2_ref, r_ref):
    x = x_ref[...] + _dot(m_ref[...], wo_ref[...])
    xo_ref[...] = x
    ms = jnp.mean(x * x, axis=-1, keepdims=True)
    h2 = x * lax.rsqrt(ms + NORM_EPS) * g_ref[...]
    h2_ref[...] = h2.astype(BF16)
    logits = _dot_hi(h2, wr_ref[...])
    tm = x.shape[0]
    lane = lax.broadcasted_iota(jnp.int32, (tm, LANE), 1).astype(F32)
    neg = -jnp.inf
    big = float(LANE)
    gl = jnp.where(lane < MOE_GROUPS, logits, neg)
    gmax = jnp.max(gl, axis=1, keepdims=True)
    g_idx = jnp.min(jnp.where(gl == gmax, lane, big), axis=1, keepdims=True)
    g_w = 1.0 / jnp.sum(jnp.exp(gl - gmax), axis=1, keepdims=True)
    e_lo = MOE_GROUPS + g_idx * MOE_EPG
    el = jnp.where(lane >= e_lo, jnp.where(lane < e_lo + MOE_EPG, logits, neg), neg)
    m1 = jnp.max(el, axis=1, keepdims=True)
    i1 = jnp.min(jnp.where(el == m1, lane, big), axis=1, keepdims=True)
    el2 = jnp.where(lane == i1, neg, el)
    m2 = jnp.max(el2, axis=1, keepdims=True)
    i2 = jnp.min(jnp.where(el2 == m2, lane, big), axis=1, keepdims=True)
    den = jnp.sum(jnp.exp(el - m1), axis=1, keepdims=True)
    p1 = 1.0 / den
    p2 = jnp.exp(m2 - m1) / den
    w1 = p1 / (p1 + p2) * g_w
    w2 = p2 / (p1 + p2) * g_w
    e1 = i1 - MOE_GROUPS
    e2 = i2 - MOE_GROUPS
    r_ref[...] = jnp.where(lane == 0, e1, jnp.where(lane == 1, e2, jnp.where(lane == 2, w1, jnp.where(lane == 3, w2, 0.0))))


def _out_proj(x, merged, wo, g, wr, tm=512):
    T = x.shape[0]
    return pl.pallas_call(
        _outproj_kernel,
        grid=(T // tm,),
        in_specs=[
            pl.BlockSpec((tm, D_MODEL), lambda i: (i, 0)),
            pl.BlockSpec((tm, D_MODEL), lambda i: (i, 0)),
            pl.BlockSpec((D_MODEL, D_MODEL), lambda i: (0, 0)),
            pl.BlockSpec((1, D_MODEL), lambda i: (0, 0)),
            pl.BlockSpec((D_MODEL, LANE), lambda i: (0, 0)),
        ],
        out_specs=[
            pl.BlockSpec((tm, D_MODEL), lambda i: (i, 0)),
            pl.BlockSpec((tm, D_MODEL), lambda i: (i, 0)),
            pl.BlockSpec((tm, LANE), lambda i: (i, 0)),
        ],
        out_shape=[
            jax.ShapeDtypeStruct((T, D_MODEL), F32),
            jax.ShapeDtypeStruct((T, D_MODEL), BF16),
            jax.ShapeDtypeStruct((T, LANE), F32),
        ],
        compiler_params=_cparams(("parallel",)),
        name="out_proj_router",
    )(x, merged, wo, g, wr)


def _moe_ffn_kernel(be_ref, nu_ref, xs_ref, sw_ref, w1_ref, w3_ref, w2_ref, ys_ref):
    @pl.when(pl.program_id(0) < nu_ref[0])
    def _():
        xb = xs_ref[...]
        a = _dot(xb, w1_ref[0])
        u = _dot(xb, w3_ref[0])
        hmid = (_silu(a) * u).astype(BF16)
        ys_ref[...] = (_dot(hmid, w2_ref[0]) * sw_ref[...]).astype(BF16)


def _moe_ffn(xs, sw, blk_e, nused, w1, w3, w2):
    R = xs.shape[0]
    nblk = R // MOE_BLK
    return pl.pallas_call(
        _moe_ffn_kernel,
        grid_spec=pltpu.PrefetchScalarGridSpec(
            num_scalar_prefetch=2,
            grid=(nblk,),
            in_specs=[
                pl.BlockSpec((MOE_BLK, D_MODEL), lambda b, be, nu: (b, 0)),
                pl.BlockSpec((MOE_BLK, 1), lambda b, be, nu: (b, 0)),
                pl.BlockSpec((1, D_MODEL, MOE_FF), lambda b, be, nu: (be[b], 0, 0)),
                pl.BlockSpec((1, D_MODEL, MOE_FF), lambda b, be, nu: (be[b], 0, 0)),
                pl.BlockSpec((1, MOE_FF, D_MODEL), lambda b, be, nu: (be[b], 0, 0)),
            ],
            out_specs=pl.BlockSpec((MOE_BLK, D_MODEL), lambda b, be, nu: (b, 0)),
        ),
        out_shape=jax.ShapeDtypeStruct((R, D_MODEL), BF16),
        compiler_params=_cparams(("arbitrary",)),
        name="moe_ffn",
    )(blk_e, nused, xs, sw, w1, w3, w2)


def _combine_kernel(x_ref, a_ref, b_ref, g_ref, o_ref, *, final):
    x = x_ref[...] + (a_ref[...].astype(F32) + b_ref[...].astype(F32))
    if final:
        ms = jnp.mean(x * x, axis=-1, keepdims=True)
        x = x * lax.rsqrt(ms + NORM_EPS) * g_ref[...]
    o_ref[...] = x


def _combine(x, a, b, g, final, tm=512):
    T = x.shape[0]
    return pl.pallas_call(
        functools.partial(_combine_kernel, final=final),
        grid=(T // tm,),
        in_specs=[pl.BlockSpec((tm, D_MODEL), lambda i: (i, 0))] * 3
        + [pl.BlockSpec((1, D_MODEL), lambda i: (0, 0))],
        out_specs=pl.BlockSpec((tm, D_MODEL), lambda i: (i, 0)),
        out_shape=jax.ShapeDtypeStruct((T, D_MODEL), F32),
        compiler_params=_cparams(("parallel",)),
        name="moe_combine",
    )(x, a, b, g)


def _moe(x, h2, route, w1, w3, w2, g_final, final):
    T = x.shape[0]
    K = 2
    e_idx = route[:, 0:K].astype(jnp.int32)
    e_w = route[:, K:2 * K]
    flat_e = e_idx.reshape(-1)
    flat_w = e_w.reshape(-1)
    onehot = (flat_e[:, None] == jnp.arange(MOE_EXPERTS)[None, :]).astype(jnp.int32)
    csum = jnp.cumsum(onehot, axis=0)
    rank = jnp.take_along_axis(csum, flat_e[:, None], axis=1)[:, 0] - 1
    counts = csum[-1]
    pcounts = (counts + MOE_BLK - 1) // MOE_BLK * MOE_BLK
    pends = jnp.cumsum(pcounts)
    pstarts = pends - pcounts
    dest = pstarts[flat_e] + rank
    R = T * K + MOE_EXPERTS * MOE_BLK
    nblk = R // MOE_BLK
    row_src = jnp.zeros((R,), jnp.int32).at[dest].set(jnp.arange(T * K, dtype=jnp.int32) // K)
    row_w = jnp.zeros((R,), F32).at[dest].set(flat_w)
    nused = (pends[-1] // MOE_BLK).astype(jnp.int32)
    blk_start = jnp.arange(nblk, dtype=jnp.int32) * MOE_BLK
    blk_e = jnp.minimum(jnp.sum((pends[None, :] <= blk_start[:, None]).astype(jnp.int32), axis=1), MOE_EXPERTS - 1)
    last_e = blk_e[jnp.maximum(nused - 1, 0)]
    blk_e = jnp.where(jnp.arange(nblk) < nused, blk_e, last_e)
    xs = jnp.take(h2, row_src, axis=0)
    ys = _moe_ffn(xs, row_w[:, None], blk_e, nused.reshape(1), w1, w3, w2)
    pos = dest.reshape(T, K)
    ga = jnp.take(ys, pos[:, 0], axis=0)
    gb = jnp.take(ys, pos[:, 1], axis=0)
    return _combine(x, ga, gb, g_final, final)


def _prep_layer(l, w_in, mlstm_gate_bias, ssd_conv_w, ssd_conv_b, ssd_dt_bias, ssd_a_log, ssd_d,
                gqa_q_norm, gqa_k_norm, w_router_group, w_router_expert):
    w = w_in[l]
    wm = jnp.take(w, jnp.asarray(_MAIN_IDX), axis=1)
    dscale = jnp.ones((P_COLS,), F32).at[DF_Q:DF_Q + BRANCH_W].set((DIFF_HD ** -0.5) * LOG2E)
    wm = jnp.pad((wm * dscale[None, :]).astype(BF16), ((0, 0), (0, P_PAD - P_COLS)))
    ws = jnp.zeros((D_MODEL, SMALL_COLS), F32).at[:, :32].set(jnp.take(w, jnp.asarray(_SMALL_IDX), axis=1)).astype(BF16)
    gb_row = jnp.zeros((1, SMALL_COLS), F32).at[0, :16].set(mlstm_gate_bias[l])
    dtb_row = jnp.zeros((1, LANE), F32).at[0, 16:32].set(ssd_dt_bias[l].reshape(-1))
    a_row = jnp.zeros((1, LANE), F32).at[0, 16:32].set((-jnp.exp(ssd_a_log[l])).reshape(-1))
    dskip_row = jnp.repeat(ssd_d[l], SSD_HD)[None, :]
    qg = jnp.tile(gqa_q_norm[l][jnp.asarray(_DEINT64)], LANE // GQA_HD)[None, :]
    kg = jnp.tile(gqa_k_norm[l][jnp.asarray(_DEINT64)], LANE // GQA_HD)[None, :]
    wr = jnp.zeros((D_MODEL, LANE), F32).at[:, :MOE_GROUPS].set(w_router_group[l])
    wr = wr.at[:, MOE_GROUPS:MOE_GROUPS + MOE_EXPERTS].set(w_router_expert[l])
    return dict(wm=wm, ws=ws, gb_row=gb_row, dtb_row=dtb_row, a_row=a_row, dskip_row=dskip_row,
                qg=qg, kg=kg, wr=wr, conv_w=ssd_conv_w[l], conv_b=ssd_conv_b[l][None, :])


def _bd_ones():
    i = np.arange(LANE)
    return (i[:, None] // GQA_HD == i[None, :] // GQA_HD).astype(np.float32)


def _trunk(x3, prm, L):
    B, N, _ = x3.shape
    x = x3.reshape(B * N, D_MODEL)
    cos_t, sin_t = _rope_tables(N)
    bd = jnp.asarray(_bd_ones()).astype(BF16)
    ktab, qtab = _alibi_tables(N)
    for l in range(DEPTH):
        lp = L[l]
        lambda_init = 0.8 - 0.6 * math.exp(-0.3 * l)
        p, h, small = _in_proj(x, prm["norm_mix"][l][None, :], lp["wm"], lp["ws"])
        y0 = _mlstm_branch(p, small, lp["gb_row"], prm["mlstm_norm"][l][None, :], B, N)
        y1 = _gqa_branch(p, cos_t, sin_t, lp["qg"], lp["kg"], bd, B, N)
        y2 = _ssd_branch(p, small, lp["conv_w"], lp["conv_b"], lp["dtb_row"], lp["a_row"], lp["dskip_row"],
                         prm["ssd_norm"][l][None, :], B, N)
        y3 = _diff_branch(p, ktab, qtab, prm["diff_lambda"][l], prm["diff_norm"][l][None, :], lambda_init, B, N)
        merged = _merge(h, (y0, y1, y2, y3), lp["wg"], lp["wb"])
        x, h2, route = _out_proj(x, merged, lp["wo"], prm["norm_ffn"][l][None, :], lp["wr"])
        x = _moe(x, h2, route, lp["w1"], lp["w3"], lp["w2"], prm["norm_final"][None, :], final=(l == DEPTH - 1))
    return x.reshape(B, N, D_MODEL)


def kernel(x_prompt, x_sample, norm_mix, w_in, mlstm_gate_bias, mlstm_norm, gqa_q_norm, gqa_k_norm, ssd_conv_w, ssd_conv_b, ssd_dt_bias, ssd_a_log, ssd_d, ssd_norm, diff_lambda, diff_norm, w_branch, w_gate, w_out, norm_ffn, w_router_group, w_router_expert, moe_w_gate, moe_w_up, moe_w_down, norm_final):
    prm = dict(norm_mix=norm_mix, mlstm_norm=mlstm_norm, ssd_norm=ssd_norm, diff_lambda=diff_lambda,
               diff_norm=diff_norm, norm_ffn=norm_ffn, norm_final=norm_final)
    layers = []
    for l in range(DEPTH):
        lp = _prep_layer(l, w_in, mlstm_gate_bias, ssd_conv_w, ssd_conv_b, ssd_dt_bias, ssd_a_log, ssd_d,
                         gqa_q_norm, gqa_k_norm, w_router_group, w_router_expert)
        lp["wg"] = w_gate[l].astype(BF16)
        lp["wb"] = w_branch[l].astype(BF16)
        lp["wo"] = w_out[l].astype(BF16)
        lp["w1"] = moe_w_gate[l].astype(BF16)
        lp["w3"] = moe_w_up[l].astype(BF16)
        lp["w2"] = moe_w_down[l].astype(BF16)
        layers.append(lp)
    return (_trunk(x_prompt, prm, layers), _trunk(x_sample, prm, layers))
```

```python
import functools
import math

import jax
import jax.numpy as jnp
import numpy as np
from jax import lax
from jax.experimental import pallas as pl
from jax.experimental.pallas import tpu as pltpu

F32 = jnp.float32
BF16 = jnp.bfloat16

D_MODEL = 1024
DEPTH = 2
GRID_W = 64
BRANCH_W = 512
NORM_EPS = 1e-6
MLSTM_HEADS = 4
MLSTM_HD = 128
GQA_HD = 64
GQA_HEADS = 8
GQA_KV_HEADS = 2
ROPE_THETA = 10000.0
SSD_HD = 64
SSD_HEADS = 8
SSD_GROUPS = 2
SSD_STATE = 128
SSD_CONV = 5
DIFF_HD = 64
DIFF_HEADS = 4
MOE_GROUPS = 4
MOE_EPG = 8
MOE_EXPERTS = 32
MOE_FF = 512
CHUNK = 128
LOG2E = 1.4426950408889634

_MLSTM_COLS = 4 * BRANCH_W + 4 * MLSTM_HEADS
_GQA_COLS = GQA_HEADS * GQA_HD + 2 * GQA_KV_HEADS * GQA_HD
_SSD_XBC = BRANCH_W + 2 * SSD_GROUPS * SSD_STATE
_SSD_COLS = BRANCH_W + _SSD_XBC + 2 * SSD_HEADS
_O0 = _MLSTM_COLS
_O1 = _O0 + _GQA_COLS
_O2 = _O1 + _SSD_COLS

ML_Q, ML_K, ML_V, ML_O = 0, 512, 1024, 1536
GQ_Q = 2048
SS_Z = 2560
DF_Q, DF_K, DF_V = 3072, 3584, 4096
SS_X, SS_BC = 4608, 5120
GQ_KV = 5632
P_COLS = 5888
P_PAD = 6144
SMALL_COLS = 128

LANE = 128
VMEM_LIMIT = 48 * 1024 * 1024

MOE_BLK = 256


def _cparams(sem):
    return pltpu.CompilerParams(dimension_semantics=sem, vmem_limit_bytes=VMEM_LIMIT)


def _dot(a, b):
    return jnp.dot(a, b, preferred_element_type=F32)


def _dot_nt(a, b):
    return lax.dot_general(a, b, (((1,), (1,)), ((), ())), preferred_element_type=F32)


def _dot_tn(a, b):
    return lax.dot_general(a, b, (((0,), (0,)), ((), ())), preferred_element_type=F32)


def _dot_hi(a, b):
    return jnp.dot(a, b, preferred_element_type=F32, precision=lax.Precision.HIGHEST)


def _sigmoid(x):
    return 1.0 / (1.0 + jnp.exp(-x))


def _silu(x):
    return x * _sigmoid(x)


def _log_sigmoid(x):
    return jnp.minimum(x, 0.0) - jnp.log(1.0 + jnp.exp(-jnp.abs(x)))


def _softplus(x):
    return jnp.maximum(x, 0.0) + jnp.log(1.0 + jnp.exp(-jnp.abs(x)))


def _main_col_index():
    gq = np.arange(GQA_HEADS * GQA_HD).reshape(GQA_HEADS, GQA_HD // 2, 2)
    gq = np.concatenate([gq[..., 0], gq[..., 1]], axis=-1).reshape(-1)
    gk = np.arange(GQA_KV_HEADS * GQA_HD).reshape(GQA_KV_HEADS, GQA_HD // 2, 2)
    gk = np.concatenate([gk[..., 0], gk[..., 1]], axis=-1).reshape(-1)
    segs = [
        np.arange(0, 4 * BRANCH_W),
        _O0 + gq,
        _O1 + np.arange(0, BRANCH_W),
        _O2 + np.arange(0, 3 * BRANCH_W),
        _O1 + BRANCH_W + np.arange(0, _SSD_XBC),
        _O0 + GQA_HEADS * GQA_HD + gk,
        _O0 + GQA_HEADS * GQA_HD + GQA_KV_HEADS * GQA_HD + np.arange(GQA_KV_HEADS * GQA_HD),
    ]
    idx = np.concatenate(segs)
    assert idx.shape[0] == P_COLS
    return idx


def _small_col_index():
    return np.concatenate([4 * BRANCH_W + np.arange(4 * MLSTM_HEADS),
                           _O1 + BRANCH_W + _SSD_XBC + np.arange(2 * SSD_HEADS)])


_MAIN_IDX = _main_col_index()
_SMALL_IDX = _small_col_index()
_DEINT64 = np.concatenate([np.arange(0, GQA_HD, 2), np.arange(1, GQA_HD, 2)])


def _inproj_kernel(x_ref, g_ref, w_ref, ws_ref, p_ref, h_ref, s_ref, h_scr):
    @pl.when(pl.program_id(1) == 0)
    def _():
        x = x_ref[...]
        ms = jnp.mean(x * x, axis=-1, keepdims=True)
        h = (x * lax.rsqrt(ms + NORM_EPS) * g_ref[...]).astype(BF16)
        h_scr[...] = h
        h_ref[...] = h
        s_ref[...] = _dot(h, ws_ref[...])

    p_ref[...] = _dot(h_scr[...], w_ref[...]).astype(BF16)


def _in_proj(x, g, w_main, w_small, tm=1024, tn=1024):
    T = x.shape[0]
    return pl.pallas_call(
        _inproj_kernel,
        grid=(T // tm, P_PAD // tn),
        in_specs=[
            pl.BlockSpec((tm, D_MODEL), lambda i, j: (i, 0)),
            pl.BlockSpec((1, D_MODEL), lambda i, j: (0, 0)),
            pl.BlockSpec((D_MODEL, tn), lambda i, j: (0, j)),
            pl.BlockSpec((D_MODEL, SMALL_COLS), lambda i, j: (0, 0)),
        ],
        out_specs=[
            pl.BlockSpec((tm, tn), lambda i, j: (i, j)),
            pl.BlockSpec((tm, D_MODEL), lambda i, j: (i, 0)),
            pl.BlockSpec((tm, SMALL_COLS), lambda i, j: (i, 0)),
        ],
        out_shape=[
            jax.ShapeDtypeStruct((T, P_PAD), BF16),
            jax.ShapeDtypeStruct((T, D_MODEL), BF16),
            jax.ShapeDtypeStruct((T, SMALL_COLS), F32),
        ],
        scratch_shapes=[pltpu.VMEM((tm, D_MODEL), BF16)],
        compiler_params=_cparams(("parallel", "arbitrary")),
        name="in_proj",
    )(x, g, w_main, w_small)


def _tri_masks(L, reverse):
    r = lax.broadcasted_iota(jnp.int32, (L, L), 0)
    c = lax.broadcasted_iota(jnp.int32, (L, L), 1)
    return (c >= r) if reverse else (c <= r)


def _mlstm_kernel(*refs, reverse, final):
    q_ref = refs[0]
    ct_scr, n_scr, m_scr = refs[-3:]

    @pl.when(pl.program_id(1) == 0)
    def _():
        ct_scr[...] = jnp.zeros_like(ct_scr)
        n_scr[...] = jnp.zeros_like(n_scr)
        m_scr[...] = jnp.zeros_like(m_scr)

    for bb in range(q_ref.shape[0]):
        _mlstm_chunk(bb, refs, reverse, final)


def _mlstm_chunk(bb, refs, reverse, final):
    if final:
        (q_ref, k_ref, v_ref, sm_ref, gb_ref, o_ref, hb_ref, ng_ref,
         out_ref, ct_scr, n_scr, m_scr) = refs
    else:
        q_ref, k_ref, v_ref, sm_ref, gb_ref, out_ref, ct_scr, n_scr, m_scr = refs
    L = CHUNK
    H = MLSTM_HEADS
    d = MLSTM_HD
    mask = _tri_masks(L, reverse)
    tri = mask.astype(F32)
    gates = sm_ref[bb] + gb_ref[...]
    logf = _log_sigmoid(gates)
    bcum = _dot_hi(tri, logf)
    gates_t = gates.T
    bcum_t = bcum.T
    i_off = 8 if reverse else 0
    f_off = i_off + 4
    edge = 0 if reverse else L - 1
    scale = d ** -0.5
    m_all = m_scr[bb]

    hs = []
    for h in range(H):
        qh = q_ref[bb, :, h * d:(h + 1) * d]
        ks = (k_ref[bb, :, h * d:(h + 1) * d].astype(F32) * scale).astype(BF16)
        vh = v_ref[bb, :, h * d:(h + 1) * d]
        b_col = bcum[:, f_off + h:f_off + h + 1]
        i_col = gates[:, i_off + h:i_off + h + 1]
        a_row = gates_t[i_off + h:i_off + h + 1, :] - bcum_t[f_off + h:f_off + h + 1, :]
        m_prev = m_all[:, h:h + 1]
        dmat = jnp.where(mask, b_col + a_row, -jnp.inf)
        inter = b_col + m_prev
        m_t = jnp.maximum(inter, jnp.max(dmat, axis=1, keepdims=True))
        w_intra = jnp.exp(dmat - m_t)
        w_inter = jnp.exp(inter - m_t)
        s = _dot_nt(qh, ks) * w_intra
        ct = ct_scr[bb * H + h]
        n_row = n_scr[bb * H + h]
        num = _dot(s.astype(BF16), vh) + w_inter * _dot(qh, ct.astype(BF16))
        qn = jnp.sum(qh.astype(F32) * n_row, axis=1, keepdims=True)
        den = jnp.sum(s, axis=1, keepdims=True) + w_inter * qn
        hs.append(num / jnp.maximum(jnp.abs(den), jnp.exp(-m_t)))
        btot = b_col[edge:edge + 1, :]
        dec = btot - b_col + i_col
        m_new = jnp.maximum(btot + m_prev, jnp.max(dec, axis=0, keepdims=True))
        ws = jnp.exp(dec - m_new)
        wc = jnp.exp(btot + m_prev - m_new)
        wsv = (ws * vh.astype(F32)).astype(BF16)
        ct_scr[bb * H + h] = wc * ct + _dot_tn(ks, wsv)
        n_scr[bb * H + h] = wc * n_row + jnp.sum(ws * ks.astype(F32), axis=0, keepdims=True)
        lane = lax.broadcasted_iota(jnp.int32, (1, LANE), 1)
        m_all = jnp.where(lane == h, m_new, m_all)
    m_scr[bb] = m_all

    if not final:
        for h in range(H):
            out_ref[bb, :, h * d:(h + 1) * d] = hs[h]
    else:
        for h in range(H):
            hsum = hs[h] + hb_ref[bb, :, h * d:(h + 1) * d]
            ms = jnp.mean(hsum * hsum, axis=1, keepdims=True)
            y = hsum * lax.rsqrt(ms + NORM_EPS) * ng_ref[:, h * d:(h + 1) * d]
            o = o_ref[bb, :, h * d:(h + 1) * d].astype(F32)
            out_ref[bb, :, h * d:(h + 1) * d] = (_sigmoid(o) * y).astype(BF16)


SCAN_NBAT = 2


def _mlstm_dir(p3, small3, gate_bias_row, B, N, reverse, hb=None, norm_g=None):
    L = CHUNK
    nc = N // L
    nbat = SCAN_NBAT
    final = hb is not None

    def crow(c):
        return (nc - 1 - c) if reverse else c

    W = BRANCH_W
    in_specs = [
        pl.BlockSpec((nbat, L, W), lambda b, c: (b, crow(c), ML_Q // W)),
        pl.BlockSpec((nbat, L, W), lambda b, c: (b, crow(c), ML_K // W)),
        pl.BlockSpec((nbat, L, W), lambda b, c: (b, crow(c), ML_V // W)),
        pl.BlockSpec((nbat, L, SMALL_COLS), lambda b, c: (b, crow(c), 0)),
        pl.BlockSpec((1, SMALL_COLS), lambda b, c: (0, 0)),
    ]
    args = [p3, p3, p3, small3, gate_bias_row]
    if final:
        in_specs += [
            pl.BlockSpec((nbat, L, W), lambda b, c: (b, crow(c), ML_O // W)),
            pl.BlockSpec((nbat, L, W), lambda b, c: (b, crow(c), 0)),
            pl.BlockSpec((1, W), lambda b, c: (0, 0)),
        ]
        args += [p3, hb, norm_g]
    return pl.pallas_call(
        functools.partial(_mlstm_kernel, reverse=reverse, final=final),
        grid=(B // nbat, nc),
        in_specs=in_specs,
        out_specs=pl.BlockSpec((nbat, L, W), lambda b, c: (b, crow(c), 0)),
        out_shape=jax.ShapeDtypeStruct((B, N, W), BF16 if final else F32),
        scratch_shapes=[
            pltpu.VMEM((nbat * MLSTM_HEADS, MLSTM_HD, MLSTM_HD), F32),
            pltpu.VMEM((nbat * MLSTM_HEADS, 1, MLSTM_HD), F32),
            pltpu.VMEM((nbat, 1, LANE), F32),
        ],
        compiler_params=_cparams(("parallel", "arbitrary")),
        name="mlstm_bwd" if reverse else "mlstm_fwd",
    )(*args)


def _mlstm_branch(p, small, gate_bias_row, norm_g, B, N):
    p3 = p.reshape(B, N, P_PAD)
    small3 = small.reshape(B, N, SMALL_COLS)
    hb = _mlstm_dir(p3, small3, gate_bias_row, B, N, reverse=True)
    y = _mlstm_dir(p3, small3, gate_bias_row, B, N, reverse=False, hb=hb, norm_g=norm_g)
    return y.reshape(B * N, BRANCH_W)


def _gqa_prep_kernel(q_ref, kv_ref, cos_ref, sin_ref, qg_ref, kg_ref, bd_ref, qo_ref, ko_ref, vo_ref):
    tm = q_ref.shape[0]
    bd = bd_ref[...]
    cos = cos_ref[...]
    sin = sin_ref[...]
    lane = lax.broadcasted_iota(jnp.int32, (tm, LANE), 1)
    first = (lane % GQA_HD) < (GQA_HD // 2)
    lo = lane < GQA_HD

    def norm_rope(x, g):
        ss = _dot((x * x).astype(BF16), bd)
        xn = x * lax.rsqrt(ss * (1.0 / GQA_HD) + NORM_EPS) * g
        partner = jnp.where(first, pltpu.roll(xn, LANE - GQA_HD // 2, axis=1),
                            pltpu.roll(xn, GQA_HD // 2, axis=1))
        return xn * cos + partner * sin

    qscale = (GQA_HD ** -0.5) * LOG2E
    for a in range(GQA_HEADS * GQA_HD // LANE):
        x = q_ref[:, a * LANE:(a + 1) * LANE].astype(F32)
        qo_ref[:, a * LANE:(a + 1) * LANE] = (norm_rope(x, qg_ref[...]) * qscale).astype(BF16)
    k = norm_rope(kv_ref[:, 0:LANE].astype(F32), kg_ref[...])
    k_sw = pltpu.roll(k, GQA_HD, axis=1)
    ko_ref[0] = jnp.where(lo, k, k_sw).astype(BF16)
    ko_ref[1] = jnp.where(lo, k_sw, k).astype(BF16)
    v = kv_ref[:, LANE:2 * LANE].astype(F32)
    v_sw = pltpu.roll(v, GQA_HD, axis=1)
    vo_ref[0] = jnp.where(lo, v, 1.0).astype(BF16)
    vo_ref[1] = jnp.where(lo, v_sw, 1.0).astype(BF16)


def _gqa_prep(p, cos_t, sin_t, qg, kg, bd, B, N, tm=512):
    T = B * N
    nb = N // tm
    return pl.pallas_call(
        _gqa_prep_kernel,
        grid=(T // tm,),
        in_specs=[
            pl.BlockSpec((tm, 512), lambda i: (i, GQ_Q // 512)),
            pl.BlockSpec((tm, 256), lambda i: (i, GQ_KV // 256)),
            pl.BlockSpec((tm, LANE), lambda i: (i % nb, 0)),
            pl.BlockSpec((tm, LANE), lambda i: (i % nb, 0)),
            pl.BlockSpec((1, LANE), lambda i: (0, 0)),
            pl.BlockSpec((1, LANE), lambda i: (0, 0)),
            pl.BlockSpec((LANE, LANE), lambda i: (0, 0)),
        ],
        out_specs=[
            pl.BlockSpec((tm, 512), lambda i: (i, 0)),
            pl.BlockSpec((GQA_KV_HEADS, tm, LANE), lambda i: (0, i, 0)),
            pl.BlockSpec((GQA_KV_HEADS, tm, LANE), lambda i: (0, i, 0)),
        ],
        out_shape=[
            jax.ShapeDtypeStruct((T, 512), BF16),
            jax.ShapeDtypeStruct((GQA_KV_HEADS, T, LANE), BF16),
            jax.ShapeDtypeStruct((GQA_KV_HEADS, T, LANE), BF16),
        ],
        compiler_params=_cparams(("parallel",)),
        name="gqa_prep",
    )(p, p, cos_t, sin_t, qg, kg, bd)


FLASH_TKB = 512


def _softmax_keys(s_ref, p_ref, m_scr, al_scr, l_scr, M):
    tkb = s_ref.shape[0]
    for c in range(M // LANE):
        cols = slice(c * LANE, (c + 1) * LANE)
        mx = jnp.max(jnp.max(s_ref[:, cols].reshape(tkb // 8, 8, LANE), axis=0), axis=0, keepdims=True)
        m_prev = m_scr[:, cols]
        m_new = jnp.maximum(m_prev, mx)
        alpha = jnp.exp2(m_prev - m_new)
        p = jnp.exp2(s_ref[:, cols] - m_new)
        p_ref[:, cols] = p.astype(BF16)
        if l_scr is not None:
            lsum = jnp.sum(jnp.sum(p.reshape(tkb // 8, 8, LANE), axis=0), axis=0, keepdims=True)
            l_scr[:, cols] = alpha * l_scr[:, cols] + lsum
        al_scr[:, cols] = alpha
        m_scr[:, cols] = m_new


def _flash_blocks(nb, qk, pv, softmax, fix=None):
    qk(0)
    for i in range(nb):
        if i + 1 < nb:
            qk(i + 1)
        if fix is not None:
            fix(i)
        softmax(i)
        pv(i)


def _gqa_flash_kernel(q_ref, k_ref, v_ref, o_ref, qs_scr, m_scr, al_scr, acc_scr, s_scr, p_scr):
    tq = q_ref.shape[0]
    M = 4 * tq
    tkb = s_scr.shape[1]
    nb = k_ref.shape[0] // tkb
    kj = pl.program_id(3)

    @pl.when(kj == 0)
    def _():
        lane = lax.broadcasted_iota(jnp.int32, (tq, LANE), 1)
        lo = lane < GQA_HD
        zero = jnp.zeros((tq, LANE), BF16)
        for a in range(2):
            qa = q_ref[:, a * LANE:(a + 1) * LANE]
            qs_scr[(2 * a) * tq:(2 * a + 1) * tq, :] = jnp.where(lo, qa, zero)
            qs_scr[(2 * a + 1) * tq:(2 * a + 2) * tq, :] = jnp.where(lo, zero, qa)
        m_scr[...] = jnp.full_like(m_scr, -jnp.inf)
        acc_scr[...] = jnp.zeros_like(acc_scr)

    def qk(i):
        s_scr[i % 2] = _dot_nt(k_ref[i * tkb:(i + 1) * tkb, :], qs_scr[...])

    def softmax(i):
        _softmax_keys(s_scr.at[i % 2], p_scr.at[i % 2], m_scr, al_scr, None, M)

    def pv(i):
        acc_scr[...] = al_scr[...] * acc_scr[...] + _dot_tn(v_ref[i * tkb:(i + 1) * tkb, :], p_scr[i % 2])

    _flash_blocks(nb, qk, pv, softmax)

    @pl.when(kj == pl.num_programs(3) - 1)
    def _():
        inv = 1.0 / acc_scr[GQA_HD:GQA_HD + 1, :]
        ot = jnp.concatenate([acc_scr[0:GQA_HD, h * tq:(h + 1) * tq] * inv[:, h * tq:(h + 1) * tq]
                              for h in range(4)], axis=0)
        o_ref[...] = ot.T.astype(BF16)


def _gqa_flash(qn, k2, va, B, N, tq=256, tk=4096):
    T = B * N
    tk = min(tk, N)
    nq, nk = N // tq, N // tk
    M = 4 * tq
    tkb = FLASH_TKB
    return pl.pallas_call(
        _gqa_flash_kernel,
        grid=(B, GQA_KV_HEADS, nq, nk),
        in_specs=[
            pl.BlockSpec((tq, 256), lambda b, g, i, j: (b * nq + i, g)),
            pl.BlockSpec((None, tk, LANE), lambda b, g, i, j: (g, b * nk + j, 0)),
            pl.BlockSpec((None, tk, LANE), lambda b, g, i, j: (g, b * nk + j, 0)),
        ],
        out_specs=pl.BlockSpec((tq, 256), lambda b, g, i, j: (b * nq + i, g)),
        out_shape=jax.ShapeDtypeStruct((T, 512), BF16),
        scratch_shapes=[
            pltpu.VMEM((M, LANE), BF16),
            pltpu.VMEM((1, M), F32),
            pltpu.VMEM((1, M), F32),
            pltpu.VMEM((LANE, M), F32),
            pltpu.VMEM((2, tkb, M), F32),
            pltpu.VMEM((2, tkb, M), BF16),
        ],
        compiler_params=_cparams(("parallel", "parallel", "parallel", "arbitrary")),
        name="gqa_flash",
    )(qn, k2, va)


def _rope_tables(N):
    rows = N // GRID_W
    row = jnp.repeat(jnp.arange(rows, dtype=F32), GRID_W)
    col = (jnp.arange(N) % GRID_W).astype(F32)
    half = GQA_HD // 2
    inv = 1.0 / (ROPE_THETA ** (jnp.arange(0, half, 2, dtype=F32) / half))
    ang = jnp.concatenate([row[:, None] * inv, col[:, None] * inv], axis=-1)
    cos, sin = jnp.cos(ang), jnp.sin(ang)
    cos_h = jnp.concatenate([cos, cos], axis=-1)
    sin_h = jnp.concatenate([-sin, sin], axis=-1)
    return jnp.tile(cos_h, (1, LANE // GQA_HD)), jnp.tile(sin_h, (1, LANE // GQA_HD))


def _gqa_branch(p, cos_t, sin_t, qg, kg, bd, B, N):
    qn, k2, va = _gqa_prep(p, cos_t, sin_t, qg, kg, bd, B, N)
    return _gqa_flash(qn, k2, va, B, N)


DIFF_SKIP_LOG2 = 40.0


def _alibi_slope_log2(h):
    return (2.0 ** (-8.0 * (h + 1) / DIFF_HEADS)) * LOG2E


def _alibi_tables(N):
    pos = np.arange(N)
    hi = ((pos // 128) * 128).astype(np.float32)
    lo = (pos % 128).astype(np.float32)
    kt = np.zeros((DIFF_HEADS, N, LANE), np.float32)
    qt = np.zeros((DIFF_HEADS, N, LANE), np.float32)
    for h in range(DIFF_HEADS):
        s = np.float32(_alibi_slope_log2(h))
        s1 = np.float32(s).astype(BF16).astype(np.float32)
        s2 = np.float32(s - s1).astype(BF16).astype(np.float32)
        for c, val in enumerate((hi, hi, lo, lo)):
            kt[h, :, c] = val
            qt[h, :, 4 + c] = -val
        for c, val in enumerate((s1, s2, s1, s2)):
            kt[h, :, 4 + c] = val
            qt[h, :, c] = val
    return (jnp.asarray(kt.reshape(DIFF_HEADS * N, LANE), BF16), jnp.asarray(qt.reshape(DIFF_HEADS * N, LANE), BF16))


def _diff_flash_kernel(lo_ref, hi_ref, q_ref, qa_ref, k_ref, ka_ref, v_ref, lp_ref, ng_ref, o_ref,
                       qs_scr, m_scr, l_scr, al_scr, acc_scr, s_scr, p_scr, *, lambda_init):
    tq = q_ref.shape[0]
    tk = k_ref.shape[0]
    M = 2 * tq
    tkb = s_scr.shape[1]
    nb = tk // tkb
    b, h, qi, kj = pl.program_id(0), pl.program_id(1), pl.program_id(2), pl.program_id(3)
    idx = (b * pl.num_programs(1) + h) * pl.num_programs(2) + qi
    first, last = lo_ref[idx], hi_ref[idx]

    @pl.when(kj == first)
    def _():
        lane = lax.broadcasted_iota(jnp.int32, (tq, LANE), 1)
        lo = lane < DIFF_HD
        zero = jnp.zeros((tq, LANE), BF16)
        q = q_ref[...]
        qa = qa_ref[...]
        for var, aug in enumerate((qa, -qa)):
            qs_scr[var, 0:tq, 0:LANE] = jnp.where(lo, q, zero)
            qs_scr[var, tq:M, 0:LANE] = jnp.where(lo, zero, q)
            qs_scr[var, 0:tq, LANE:2 * LANE] = aug
            qs_scr[var, tq:M, LANE:2 * LANE] = aug
        m_scr[...] = jnp.full_like(m_scr, -jnp.inf)
        l_scr[...] = jnp.zeros_like(l_scr)
        acc_scr[...] = jnp.zeros_like(acc_scr)

    is_left = (kj + 1) * tk <= qi * tq
    is_right = kj * tk >= (qi + 1) * tq
    active = jnp.logical_and(kj >= first, kj <= last)
    sel = jnp.where(is_right, 1, 0)

    def qk(i):
        rows = slice(i * tkb, (i + 1) * tkb)
        kk = jnp.concatenate([k_ref[rows, :], ka_ref[rows, :]], axis=1)
        s_scr[i % 2] = _dot_nt(kk, qs_scr[sel])

    def fix(i):
        slope = jnp.exp2(-2.0 * (jnp.full((1, 1), h, jnp.int32).astype(F32) + 1.0)) * LOG2E
        j = kj * tk + i * tkb + lax.broadcasted_iota(jnp.int32, (tkb, 1), 0)
        col = lax.broadcasted_iota(jnp.int32, (1, M), 1)
        iq = qi * tq + col - jnp.where(col >= tq, tq, 0)
        d = jnp.maximum(j - iq, 0).astype(F32)
        s_scr[i % 2] = s_scr[i % 2] - (2.0 * slope) * d

    def softmax(i):
        _softmax_keys(s_scr.at[i % 2], p_scr.at[i % 2], m_scr, al_scr, l_scr, M)

    def pv(i):
        acc_scr[...] = al_scr[...] * acc_scr[...] + _dot_tn(v_ref[i * tkb:(i + 1) * tkb, :], p_scr[i % 2])

    @pl.when(jnp.logical_and(active, jnp.logical_or(is_left, is_right)))
    def _():
        _flash_blocks(nb, qk, pv, softmax)

    @pl.when(jnp.logical_and(active, jnp.logical_not(jnp.logical_or(is_left, is_right))))
    def _():
        _flash_blocks(nb, qk, pv, softmax, fix=fix)

    @pl.when(kj == last)
    def _():
        lp = lp_ref[...]
        s01 = jnp.sum(jnp.sum(lp[0:1] * lp[1:2], axis=1, keepdims=True), axis=0, keepdims=True)
        s23 = jnp.sum(jnp.sum(lp[2:3] * lp[3:4], axis=1, keepdims=True), axis=0, keepdims=True)
        lam = jnp.exp(s01) - jnp.exp(s23) + lambda_init
        inv = 1.0 / l_scr[...]
        ot = acc_scr[:, 0:tq] * inv[:, 0:tq] - lam * (acc_scr[:, tq:M] * inv[:, tq:M])
        o = ot.T
        ms = jnp.mean(o * o, axis=1, keepdims=True)
        o_ref[...] = (o * lax.rsqrt(ms + NORM_EPS) * ng_ref[...] * (1.0 - lambda_init)).astype(BF16)


def _diff_prep_kernel(q_ref, k_ref, v_ref, bd_ref, ko_ref, vo_ref, nrm_ref):
    tm = q_ref.shape[0]
    lane = lax.broadcasted_iota(jnp.int32, (tm, LANE), 1)
    nrm = jnp.zeros((tm, LANE), F32)
    for h in range(DIFF_HEADS):
        cols = slice(h * LANE, (h + 1) * LANE)
        ko_ref[h] = k_ref[:, cols]
        vo_ref[h] = v_ref[:, cols]
        for src, off in ((q_ref, 0), (k_ref, DIFF_HEADS)):
            x = src[:, cols].astype(F32)
            ss = _dot_hi(x * x, bd_ref[...])
            n = jnp.sqrt(jnp.maximum(ss, pltpu.roll(ss, DIFF_HD, axis=1)))
            nrm = jnp.where(lane == off + h, n, nrm)
    nrm_ref[...] = nrm


def _diff_prep(p, bd32, B, N, tm=512):
    T = B * N
    W = BRANCH_W
    return pl.pallas_call(
        _diff_prep_kernel,
        grid=(T // tm,),
        in_specs=[
            pl.BlockSpec((tm, W), lambda i: (i, DF_Q // W)),
            pl.BlockSpec((tm, W), lambda i: (i, DF_K // W)),
            pl.BlockSpec((tm, W), lambda i: (i, DF_V // W)),
            pl.BlockSpec((LANE, LANE), lambda i: (0, 0)),
        ],
        out_specs=[
            pl.BlockSpec((DIFF_HEADS, tm, LANE), lambda i: (0, i, 0)),
            pl.BlockSpec((DIFF_HEADS, tm, LANE), lambda i: (0, i, 0)),
            pl.BlockSpec((tm, LANE), lambda i: (i, 0)),
        ],
        out_shape=[
            jax.ShapeDtypeStruct((DIFF_HEADS, T, LANE), BF16),
            jax.ShapeDtypeStruct((DIFF_HEADS, T, LANE), BF16),
            jax.ShapeDtypeStruct((T, LANE), F32),
        ],
        compiler_params=_cparams(("parallel",)),
        name="diff_prep",
    )(p, p, p, bd32)


def _diff_bands(nrm, B, N, tq, tk):
    nq, nkt = N // tq, N // tk
    qn = nrm[:, 0:DIFF_HEADS].reshape(B, N, DIFF_HEADS)
    kn = nrm[:, DIFF_HEADS:2 * DIFF_HEADS].reshape(B, N, DIFF_HEADS)
    qmax = jnp.max(qn.reshape(B, nq, tq, DIFF_HEADS), axis=2)
    kself = jnp.max(kn.reshape(B, nq, tq, DIFF_HEADS), axis=2)
    kmax = jnp.max(kn.reshape(B, nkt, tk, DIFF_HEADS), axis=2)
    bound = qmax[:, :, None, :] * (kmax[:, None, :, :] + kself[:, :, None, :]) * 1.001 + DIFF_SKIP_LOG2
    q0 = np.arange(nq)[:, None] * tq
    k0 = np.arange(nkt)[None, :] * tk
    dist = np.maximum(0, np.maximum(q0 - (k0 + tk - 1), k0 - (q0 + tq - 1))).astype(np.float32)
    slopes = np.array([_alibi_slope_log2(h) for h in range(DIFF_HEADS)], np.float32)
    keep = bound > jnp.asarray(dist)[None, :, :, None] * jnp.asarray(slopes)[None, None, None, :]
    keep = jnp.transpose(keep, (0, 3, 1, 2))
    first = jnp.argmax(keep, axis=-1).astype(jnp.int32)
    last = (nkt - 1 - jnp.argmax(keep[..., ::-1], axis=-1)).astype(jnp.int32)
    return first.reshape(-1), last.reshape(-1)


def _diff_branch(p, ktab, qtab, bd32, lam_params, norm_g, lambda_init, B, N, tq=512, tk=2048):
    T = B * N
    nq, nk = N // tq, N // tk
    M = 2 * tq
    tkb = FLASH_TKB
    kd, vd, nrm = _diff_prep(p, bd32, B, N)
    first, last = _diff_bands(nrm, B, N, tq, tk)

    def kstep(b, h, i, j, lo, hi):
        idx = (b * DIFF_HEADS + h) * nq + i
        return jnp.minimum(jnp.maximum(j, lo[idx]), hi[idx])

    return pl.pallas_call(
        functools.partial(_diff_flash_kernel, lambda_init=lambda_init),
        grid_spec=pltpu.PrefetchScalarGridSpec(
            num_scalar_prefetch=2,
            grid=(B, DIFF_HEADS, nq, nk),
            in_specs=[
                pl.BlockSpec((tq, LANE), lambda b, h, i, j, lo, hi: (b * nq + i, DF_Q // LANE + h)),
                pl.BlockSpec((tq, LANE), lambda b, h, i, j, lo, hi: (h * nq + i, 0)),
                pl.BlockSpec((None, tk, LANE), lambda b, h, i, j, lo, hi: (h, b * nk + kstep(b, h, i, j, lo, hi), 0)),
                pl.BlockSpec((tk, LANE), lambda b, h, i, j, lo, hi: (h * nk + kstep(b, h, i, j, lo, hi), 0)),
                pl.BlockSpec((None, tk, LANE), lambda b, h, i, j, lo, hi: (h, b * nk + kstep(b, h, i, j, lo, hi), 0)),
                pl.BlockSpec((4, DIFF_HD), lambda b, h, i, j, lo, hi: (0, 0)),
                pl.BlockSpec((1, LANE), lambda b, h, i, j, lo, hi: (0, h)),
            ],
            out_specs=pl.BlockSpec((tq, LANE), lambda b, h, i, j, lo, hi: (b * nq + i, h)),
            scratch_shapes=[
                pltpu.VMEM((2, M, 2 * LANE), BF16),
                pltpu.VMEM((1, M), F32),
                pltpu.VMEM((1, M), F32),
                pltpu.VMEM((1, M), F32),
                pltpu.VMEM((LANE, M), F32),
                pltpu.VMEM((2, tkb, M), F32),
                pltpu.VMEM((2, tkb, M), BF16),
            ],
        ),
        out_shape=jax.ShapeDtypeStruct((T, BRANCH_W), BF16),
        compiler_params=_cparams(("parallel", "parallel", "parallel", "arbitrary")),
        name="diff_flash",
    )(first, last, p, qtab, kd, ktab, vd, lam_params, norm_g)


SSD_HALO = 16


def _ssd_prep_kernel(x_ref, xp_ref, xn_ref, bc_ref, bcp_ref, bcn_ref, sm_ref, cw_ref, cb_ref,
                     dtb_ref, arow_ref, xo_ref, bco_ref, dto_ref):
    tc = x_ref.shape[0]
    c = pl.program_id(1)
    nc = pl.num_programs(1)
    has_prev = jnp.where(c > 0, 1.0, 0.0)
    has_next = jnp.where(c < nc - 1, 1.0, 0.0)
    pad = SSD_CONV // 2

    def conv(main_ref, prev_ref, next_ref, off):
        xf = jnp.concatenate([prev_ref[...].astype(F32) * has_prev, main_ref[...].astype(F32),
                              next_ref[...].astype(F32) * has_next], axis=0)
        n = xf.shape[0]
        acc = jnp.zeros((tc, xf.shape[1]), F32) + cb_ref[:, off:off + xf.shape[1]]
        for j in range(SSD_CONV):
            sh = pltpu.roll(xf, (pad - j) % n, axis=0)[SSD_HALO:SSD_HALO + tc, :]
            acc = acc + sh * cw_ref[j:j + 1, off:off + xf.shape[1]]
        return _silu(acc)

    xo_ref[...] = conv(x_ref, xp_ref, xn_ref, 0).astype(BF16)
    bco_ref[...] = conv(bc_ref, bcp_ref, bcn_ref, BRANCH_W).astype(BF16)
    dt = _softplus(sm_ref[...] + dtb_ref[...])
    a = pltpu.roll(dt * arow_ref[...], LANE - 16, axis=1)
    lane = lax.broadcasted_iota(jnp.int32, (tc, LANE), 1)
    dto_ref[...] = jnp.where(lane < 16, a, jnp.where(lane < 32, dt, 0.0))


def _ssd_prep(p, small, conv_w, conv_b, dtb_row, a_row, B, N, tc=512):
    T = B * N
    nc = N // tc
    hb = tc // SSD_HALO
    W = BRANCH_W

    def main(col):
        return pl.BlockSpec((tc, W), lambda b, c: (b * nc + c, col))

    def prev(col):
        return pl.BlockSpec((SSD_HALO, W), lambda b, c: (jnp.maximum((b * nc + c) * hb - 1, 0), col))

    def nxt(col):
        return pl.BlockSpec((SSD_HALO, W), lambda b, c: (jnp.minimum((b * nc + c + 1) * hb, T // SSD_HALO - 1), col))

    cx, cbc = SS_X // W, SS_BC // W
    return pl.pallas_call(
        _ssd_prep_kernel,
        grid=(B, nc),
        in_specs=[
            main(cx), prev(cx), nxt(cx), main(cbc), prev(cbc), nxt(cbc),
            pl.BlockSpec((tc, SMALL_COLS), lambda b, c: (b * nc + c, 0)),
            pl.BlockSpec((SSD_CONV, 2 * W), lambda b, c: (0, 0)),
            pl.BlockSpec((1, 2 * W), lambda b, c: (0, 0)),
            pl.BlockSpec((1, LANE), lambda b, c: (0, 0)),
            pl.BlockSpec((1, LANE), lambda b, c: (0, 0)),
        ],
        out_specs=[
            pl.BlockSpec((tc, W), lambda b, c: (b * nc + c, 0)),
            pl.BlockSpec((tc, W), lambda b, c: (b * nc + c, 0)),
            pl.BlockSpec((tc, LANE), lambda b, c: (b * nc + c, 0)),
        ],
        out_shape=[
            jax.ShapeDtypeStruct((T, W), BF16),
            jax.ShapeDtypeStruct((T, W), BF16),
            jax.ShapeDtypeStruct((T, LANE), F32),
        ],
        compiler_params=_cparams(("parallel", "parallel")),
        name="ssd_prep",
    )(p, p, p, p, p, p, small, conv_w, conv_b, dtb_row, a_row)


def _ssd_kernel(*refs, reverse, final):
    x_ref = refs[0]
    hs_scr = refs[-1]

    @pl.when(pl.program_id(1) == 0)
    def _():
        hs_scr[...] = jnp.zeros_like(hs_scr)

    for bb in range(x_ref.shape[0]):
        _ssd_chunk(bb, refs, reverse, final)


def _ssd_chunk(bb, refs, reverse, final):
    if final:
        (x_ref, bc_ref, dta_ref, ea_ref, ed_ref, z_ref, yb_ref, dsk_ref, ng_ref,
         out_ref, hs_scr) = refs
    else:
        x_ref, bc_ref, dta_ref, ea_ref, ed_ref, out_ref, hs_scr = refs
    L = CHUNK
    S = SSD_STATE
    W = BRANCH_W
    GW = W // SSD_GROUPS
    mask = _tri_masks(L, reverse)
    tri = mask.astype(F32)
    dta = dta_ref[bb]
    acs = _dot_hi(tri, dta)
    acs_t = acs.T
    acs_e = _dot_hi(acs, ea_ref[...])
    dt_e = _dot_hi(dta, ed_ref[...])
    edge = 0 if reverse else L - 1
    a_off = 8 if reverse else 0
    acs_end = acs_e[edge:edge + 1, :]
    xf = x_ref[bb].astype(F32)
    xdt = xf * dt_e
    xdt_b = xdt.astype(BF16)
    xdend = (xdt * jnp.exp(acs_end - acs_e)).astype(BF16)
    e_acs = jnp.exp(acs_e)
    cdec = jnp.exp(acs_end)
    lane = lax.broadcasted_iota(jnp.int32, (L, LANE), 1)
    lo = lane < SSD_HD
    zero = jnp.zeros((L, LANE), BF16)

    ys = []
    for g in range(SSD_GROUPS):
        bg = bc_ref[bb, :, g * S:(g + 1) * S]
        cg = bc_ref[bb, :, SSD_GROUPS * S + g * S:SSD_GROUPS * S + (g + 1) * S]
        cb = _dot_nt(cg, bg)
        hs = hs_scr[bb * SSD_GROUPS + g]
        y_off = _dot(cg, hs.astype(BF16)) * e_acs[:, g * GW:(g + 1) * GW]
        hs_scr[bb * SSD_GROUPS + g] = (cdec[:, g * GW:(g + 1) * GW] * hs
                                       + _dot_tn(bg, xdend[:, g * GW:(g + 1) * GW]))
        for pr in range(GW // LANE):
            col = g * GW + pr * LANE
            xp = xdt_b[:, col:col + LANE]
            yd = jnp.zeros((L, LANE), F32)
            for hh in range(2):
                hd = (col // SSD_HD) + hh
                a_col = acs[:, a_off + hd:a_off + hd + 1]
                a_row = acs_t[a_off + hd:a_off + hd + 1, :]
                decay = jnp.exp(jnp.where(mask, a_col - a_row, -jnp.inf))
                wm = (cb * decay).astype(BF16)
                xm = jnp.where(lo, xp, zero) if hh == 0 else jnp.where(lo, zero, xp)
                yd = yd + _dot(wm, xm)
            ys.append(yd + y_off[:, pr * LANE:(pr + 1) * LANE])
    y = jnp.concatenate(ys, axis=1)

    if not final:
        out_ref[bb] = y
    else:
        y = y + yb_ref[bb] + dsk_ref[...] * xf
        y = y * _silu(z_ref[bb].astype(F32))
        outs = []
        for g in range(SSD_GROUPS):
            yg = y[:, g * GW:(g + 1) * GW]
            ms = jnp.mean(yg * yg, axis=1, keepdims=True)
            outs.append(yg * lax.rsqrt(ms + NORM_EPS))
        out_ref[bb] = (jnp.concatenate(outs, axis=1) * ng_ref[...]).astype(BF16)


def _ssd_dir(xs, bc, dta, ea, ed, B, N, reverse, p3=None, yb=None, dskip_row=None, norm_g=None):
    L = CHUNK
    nc = N // L
    nbat = SCAN_NBAT
    W = BRANCH_W
    final = yb is not None

    def crow(c):
        return (nc - 1 - c) if reverse else c

    in_specs = [
        pl.BlockSpec((nbat, L, W), lambda b, c: (b, crow(c), 0)),
        pl.BlockSpec((nbat, L, W), lambda b, c: (b, crow(c), 0)),
        pl.BlockSpec((nbat, L, LANE), lambda b, c: (b, crow(c), 0)),
        pl.BlockSpec((LANE, W), lambda b, c: (0, 0)),
        pl.BlockSpec((LANE, W), lambda b, c: (0, 0)),
    ]
    args = [xs, bc, dta, ea, ed]
    if final:
        in_specs += [
            pl.BlockSpec((nbat, L, W), lambda b, c: (b, crow(c), SS_Z // W)),
            pl.BlockSpec((nbat, L, W), lambda b, c: (b, crow(c), 0)),
            pl.BlockSpec((1, W), lambda b, c: (0, 0)),
            pl.BlockSpec((1, W), lambda b, c: (0, 0)),
        ]
        args += [p3, yb, dskip_row, norm_g]
    return pl.pallas_call(
        functools.partial(_ssd_kernel, reverse=reverse, final=final),
        grid=(B // nbat, nc),
        in_specs=in_specs,
        out_specs=pl.BlockSpec((nbat, L, W), lambda b, c: (b, crow(c), 0)),
        out_shape=jax.ShapeDtypeStruct((B, N, W), BF16 if final else F32),
        scratch_shapes=[pltpu.VMEM((nbat * SSD_GROUPS, SSD_STATE, W // SSD_GROUPS), F32)],
        compiler_params=_cparams(("parallel", "arbitrary")),
        name="ssd_bwd" if reverse else "ssd_fwd",
    )(*args)


def _ssd_expand_mats():
    ea_f = np.zeros((LANE, BRANCH_W), np.float32)
    ea_b = np.zeros((LANE, BRANCH_W), np.float32)
    ed_f = np.zeros((LANE, BRANCH_W), np.float32)
    ed_b = np.zeros((LANE, BRANCH_W), np.float32)
    for h in range(SSD_HEADS):
        ea_f[h, h * SSD_HD:(h + 1) * SSD_HD] = 1.0
        ea_b[8 + h, h * SSD_HD:(h + 1) * SSD_HD] = 1.0
        ed_f[16 + h, h * SSD_HD:(h + 1) * SSD_HD] = 1.0
        ed_b[24 + h, h * SSD_HD:(h + 1) * SSD_HD] = 1.0
    return ea_f, ea_b, ed_f, ed_b


_EA_F, _EA_B, _ED_F, _ED_B = _ssd_expand_mats()


def _ssd_branch(p, small, conv_w, conv_b, dtb_row, a_row, dskip_row, norm_g, B, N):
    xs, bc, dta = _ssd_prep(p, small, conv_w, conv_b, dtb_row, a_row, B, N)
    xs, bc, dta = (a.reshape(B, N, a.shape[-1]) for a in (xs, bc, dta))
    yb = _ssd_dir(xs, bc, dta, jnp.asarray(_EA_B), jnp.asarray(_ED_B), B, N, reverse=True)
    y = _ssd_dir(xs, bc, dta, jnp.asarray(_EA_F), jnp.asarray(_ED_F), B, N, reverse=False,
                 p3=p.reshape(B, N, P_PAD), yb=yb, dskip_row=dskip_row, norm_g=norm_g)
    return y.reshape(B * N, BRANCH_W)


def _merge_kernel(h_ref, y0_ref, y1_ref, y2_ref, y3_ref, wg_ref, wb_ref, o_ref):
    h = h_ref[...]
    acc = None
    for i, y_ref in enumerate((y0_ref, y1_ref, y2_ref, y3_ref)):
        gate = _sigmoid(_dot(h, wg_ref[i]))
        term = gate * _dot(y_ref[...], wb_ref[i])
        acc = term if acc is None else acc + term
    o_ref[...] = acc.astype(BF16)


def _merge(h, ys, wg, wb, tm=1024, tn=256):
    T = h.shape[0]
    return pl.pallas_call(
        _merge_kernel,
        grid=(T // tm, D_MODEL // tn),
        in_specs=[pl.BlockSpec((tm, D_MODEL), lambda i, j: (i, 0))]
        + [pl.BlockSpec((tm, BRANCH_W), lambda i, j: (i, 0))] * 4
        + [
            pl.BlockSpec((4, D_MODEL, tn), lambda i, j: (0, 0, j)),
            pl.BlockSpec((4, BRANCH_W, tn), lambda i, j: (0, 0, j)),
        ],
        out_specs=pl.BlockSpec((tm, tn), lambda i, j: (i, j)),
        out_shape=jax.ShapeDtypeStruct((T, D_MODEL), BF16),
        compiler_params=_cparams(("parallel", "arbitrary")),
        name="merge",
    )(h, *ys, wg, wb)


def _outproj_kernel(x_ref, m_ref, wo_ref, g_ref, wr_ref, xo_ref, h2_ref, r_ref):
    x = x_ref[...] + _dot(m_ref[...], wo_ref[...])
    xo_ref[...] = x
    ms = jnp.mean(x * x, axis=-1, keepdims=True)
    h2 = x * lax.rsqrt(ms + NORM_EPS) * g_ref[...]
    h2_ref[...] = h2.astype(BF16)
    logits = _dot_hi(h2, wr_ref[...])
    tm = x.shape[0]
    lane = lax.broadcasted_iota(jnp.int32, (tm, LANE), 1).astype(F32)
    neg = -jnp.inf
    big = float(LANE)
    gl = jnp.where(lane < MOE_GROUPS, logits, neg)
    gmax = jnp.max(gl, axis=1, keepdims=True)
    g_idx = jnp.min(jnp.where(gl == gmax, lane, big), axis=1, keepdims=True)
    g_w = 1.0 / jnp.sum(jnp.exp(gl - gmax), axis=1, keepdims=True)
    e_lo = MOE_GROUPS + g_idx * MOE_EPG
    el = jnp.where(lane >= e_lo, jnp.where(lane < e_lo + MOE_EPG, logits, neg), neg)
    m1 = jnp.max(el, axis=1, keepdims=True)
    i1 = jnp.min(jnp.where(el == m1, lane, big), axis=1, keepdims=True)
    el2 = jnp.where(lane == i1, neg, el)
    m2 = jnp.max(el2, axis=1, keepdims=True)
    i2 = jnp.min(jnp.where(el2 == m2, lane, big), axis=1, keepdims=True)
    den = jnp.sum(jnp.exp(el - m1), axis=1, keepdims=True)
    p1 = 1.0 / den
    p2 = jnp.exp(m2 - m1) / den
    w1 = p1 / (p1 + p2) * g_w
    w2 = p2 / (p1 + p2) * g_w
    e1 = i1 - MOE_GROUPS
    e2 = i2 - MOE_GROUPS
    r_ref[...] = jnp.where(lane == 0, e1, jnp.where(lane == 1, e2, jnp.where(lane == 2, w1, jnp.where(lane == 3, w2, 0.0))))


def _out_proj(x, merged, wo, g, wr, tm=512):
    T = x.shape[0]
    return pl.pallas_call(
        _outproj_kernel,
        grid=(T // tm,),
        in_specs=[
            pl.BlockSpec((tm, D_MODEL), lambda i: (i, 0)),
            pl.BlockSpec((tm, D_MODEL), lambda i: (i, 0)),
            pl.BlockSpec((D_MODEL, D_MODEL), lambda i: (0, 0)),
            pl.BlockSpec((1, D_MODEL), lambda i: (0, 0)),
            pl.BlockSpec((D_MODEL, LANE), lambda i: (0, 0)),
        ],
        out_specs=[
            pl.BlockSpec((tm, D_MODEL), lambda i: (i, 0)),
            pl.BlockSpec((tm, D_MODEL), lambda i: (i, 0)),
            pl.BlockSpec((tm, LANE), lambda i: (i, 0)),
        ],
        out_shape=[
            jax.ShapeDtypeStruct((T, D_MODEL), F32),
            jax.ShapeDtypeStruct((T, D_MODEL), BF16),
            jax.ShapeDtypeStruct((T, LANE), F32),
        ],
        compiler_params=_cparams(("parallel",)),
        name="out_proj_router",
    )(x, merged, wo, g, wr)


def _moe_ffn_kernel(be_ref, nu_ref, xs_ref, w1_ref, w3_ref, w2_ref, ys_ref):
    used = pl.program_id(0) < nu_ref[0]

    @pl.when(used)
    def _():
        xb = xs_ref[...]
        a = _dot(xb, w1_ref[0])
        u = _dot(xb, w3_ref[0])
        hmid = (_silu(a) * u).astype(BF16)
        ys_ref[...] = _dot(hmid, w2_ref[0]).astype(BF16)

    @pl.when(jnp.logical_not(used))
    def _():
        ys_ref[...] = jnp.zeros_like(ys_ref)


def _moe_ffn(xs, blk_e, nused, w1, w3, w2):
    R = xs.shape[0]
    nblk = R // MOE_BLK
    return pl.pallas_call(
        _moe_ffn_kernel,
        grid_spec=pltpu.PrefetchScalarGridSpec(
            num_scalar_prefetch=2,
            grid=(nblk,),
            in_specs=[
                pl.BlockSpec((MOE_BLK, D_MODEL), lambda b, be, nu: (b, 0)),
                pl.BlockSpec((1, D_MODEL, MOE_FF), lambda b, be, nu: (be[b], 0, 0)),
                pl.BlockSpec((1, D_MODEL, MOE_FF), lambda b, be, nu: (be[b], 0, 0)),
                pl.BlockSpec((1, MOE_FF, D_MODEL), lambda b, be, nu: (be[b], 0, 0)),
            ],
            out_specs=pl.BlockSpec((MOE_BLK, D_MODEL), lambda b, be, nu: (b, 0)),
        ),
        out_shape=jax.ShapeDtypeStruct((R, D_MODEL), BF16),
        compiler_params=_cparams(("arbitrary",)),
        name="moe_ffn",
    )(blk_e, nused, xs, w1, w3, w2)


def _combine_kernel(x_ref, a_ref, b_ref, r_ref, g_ref, o_ref, *, final):
    r = r_ref[...]
    x = x_ref[...] + (r[:, 2:3] * a_ref[...].astype(F32) + r[:, 3:4] * b_ref[...].astype(F32))
    if final:
        ms = jnp.mean(x * x, axis=-1, keepdims=True)
        x = x * lax.rsqrt(ms + NORM_EPS) * g_ref[...]
    o_ref[...] = x


def _combine(x, a, b, route, g, final, tm=512):
    T = x.shape[0]
    return pl.pallas_call(
        functools.partial(_combine_kernel, final=final),
        grid=(T // tm,),
        in_specs=[pl.BlockSpec((tm, D_MODEL), lambda i: (i, 0))] * 3
        + [pl.BlockSpec((tm, LANE), lambda i: (i, 0)), pl.BlockSpec((1, D_MODEL), lambda i: (0, 0))],
        out_specs=pl.BlockSpec((tm, D_MODEL), lambda i: (i, 0)),
        out_shape=jax.ShapeDtypeStruct((T, D_MODEL), F32),
        compiler_params=_cparams(("parallel",)),
        name="moe_combine",
    )(x, a, b, route, g)


def _moe(x, h2, route, w1, w3, w2, g_final, final):
    T = x.shape[0]
    K = 2
    e_idx = route[:, 0:K].astype(jnp.int32)
    flat_e = e_idx.reshape(-1)
    onehot = (flat_e[:, None] == jnp.arange(MOE_EXPERTS)[None, :]).astype(jnp.int32)
    csum = jnp.cumsum(onehot, axis=0)
    rank = jnp.take_along_axis(csum, flat_e[:, None], axis=1)[:, 0] - 1
    counts = csum[-1]
    pcounts = (counts + MOE_BLK - 1) // MOE_BLK * MOE_BLK
    pends = jnp.cumsum(pcounts)
    pstarts = pends - pcounts
    dest = pstarts[flat_e] + rank
    R = T * K + MOE_EXPERTS * MOE_BLK
    nblk = R // MOE_BLK
    row_src = jnp.zeros((R,), jnp.int32).at[dest].set(jnp.arange(T * K, dtype=jnp.int32) // K)
    nused = (pends[-1] // MOE_BLK).astype(jnp.int32)
    blk_start = jnp.arange(nblk, dtype=jnp.int32) * MOE_BLK
    blk_e = jnp.minimum(jnp.sum((pends[None, :] <= blk_start[:, None]).astype(jnp.int32), axis=1), MOE_EXPERTS - 1)
    last_e = blk_e[jnp.maximum(nused - 1, 0)]
    blk_e = jnp.where(jnp.arange(nblk) < nused, blk_e, last_e)
    xs = jnp.take(h2, row_src, axis=0)
    ys = _moe_ffn(xs, blk_e, nused.reshape(1), w1, w3, w2)
    pos = dest.reshape(T, K)
    ga = jnp.take(ys, pos[:, 0], axis=0)
    gb = jnp.take(ys, pos[:, 1], axis=0)
    return _combine(x, ga, gb, route, g_final, final)


def _prep_layer(l, w_in, mlstm_gate_bias, ssd_conv_w, ssd_conv_b, ssd_dt_bias, ssd_a_log, ssd_d,
                gqa_q_norm, gqa_k_norm, w_router_group, w_router_expert):
    w = w_in[l]
    wm = jnp.take(w, jnp.asarray(_MAIN_IDX), axis=1)
    dscale = jnp.ones((P_COLS,), F32).at[DF_Q:DF_Q + BRANCH_W].set((DIFF_HD ** -0.5) * LOG2E)
    wm = jnp.pad((wm * dscale[None, :]).astype(BF16), ((0, 0), (0, P_PAD - P_COLS)))
    ws = jnp.zeros((D_MODEL, SMALL_COLS), F32).at[:, :32].set(jnp.take(w, jnp.asarray(_SMALL_IDX), axis=1)).astype(BF16)
    gb_row = jnp.zeros((1, SMALL_COLS), F32).at[0, :16].set(mlstm_gate_bias[l])
    dtb_row = jnp.zeros((1, LANE), F32).at[0, 16:32].set(ssd_dt_bias[l].reshape(-1))
    a_row = jnp.zeros((1, LANE), F32).at[0, 16:32].set((-jnp.exp(ssd_a_log[l])).reshape(-1))
    dskip_row = jnp.repeat(ssd_d[l], SSD_HD)[None, :]
    qg = jnp.tile(gqa_q_norm[l][jnp.asarray(_DEINT64)], LANE // GQA_HD)[None, :]
    kg = jnp.tile(gqa_k_norm[l][jnp.asarray(_DEINT64)], LANE // GQA_HD)[None, :]
    wr = jnp.zeros((D_MODEL, LANE), F32).at[:, :MOE_GROUPS].set(w_router_group[l])
    wr = wr.at[:, MOE_GROUPS:MOE_GROUPS + MOE_EXPERTS].set(w_router_expert[l])
    return dict(wm=wm, ws=ws, gb_row=gb_row, dtb_row=dtb_row, a_row=a_row, dskip_row=dskip_row,
                qg=qg, kg=kg, wr=wr, conv_w=ssd_conv_w[l], conv_b=ssd_conv_b[l][None, :])


def _bd_ones():
    i = np.arange(LANE)
    return (i[:, None] // GQA_HD == i[None, :] // GQA_HD).astype(np.float32)


def _trunk(x3, prm, L):
    B, N, _ = x3.shape
    x = x3.reshape(B * N, D_MODEL)
    cos_t, sin_t = _rope_tables(N)
    bd32 = jnp.asarray(_bd_ones())
    bd = bd32.astype(BF16)
    ktab, qtab = _alibi_tables(N)
    for l in range(DEPTH):
        lp = L[l]
        lambda_init = 0.8 - 0.6 * math.exp(-0.3 * l)
        p, h, small = _in_proj(x, prm["norm_mix"][l][None, :], lp["wm"], lp["ws"])
        y0 = _mlstm_branch(p, small, lp["gb_row"], prm["mlstm_norm"][l][None, :], B, N)
        y1 = _gqa_branch(p, cos_t, sin_t, lp["qg"], lp["kg"], bd, B, N)
        y2 = _ssd_branch(p, small, lp["conv_w"], lp["conv_b"], lp["dtb_row"], lp["a_row"], lp["dskip_row"],
                         prm["ssd_norm"][l][None, :], B, N)
        y3 = _diff_branch(p, ktab, qtab, bd32, prm["diff_lambda"][l], prm["diff_norm"][l][None, :], lambda_init,
                          B, N)
        merged = _merge(h, (y0, y1, y2, y3), lp["wg"], lp["wb"])
        x, h2, route = _out_proj(x, merged, lp["wo"], prm["norm_ffn"][l][None, :], lp["wr"])
        x = _moe(x, h2, route, lp["w1"], lp["w3"], lp["w2"], prm["norm_final"][None, :], final=(l == DEPTH - 1))
    return x.reshape(B, N, D_MODEL)


def kernel(x_prompt, x_sample, norm_mix, w_in, mlstm_gate_bias, mlstm_norm, gqa_q_norm, gqa_k_norm, ssd_conv_w, ssd_conv_b, ssd_dt_bias, ssd_a_log, ssd_d, ssd_norm, diff_lambda, diff_norm, w_branch, w_gate, w_out, norm_ffn, w_router_group, w_router_expert, moe_w_gate, moe_w_up, moe_w_down, norm_final):
    prm = dict(norm_mix=norm_mix, mlstm_norm=mlstm_norm, ssd_norm=ssd_norm, diff_lambda=diff_lambda,
               diff_norm=diff_norm, norm_ffn=norm_ffn, norm_final=norm_final)
    layers = []
    for l in range(DEPTH):
        lp = _prep_layer(l, w_in, mlstm_gate_bias, ssd_conv_w, ssd_conv_b, ssd_dt_bias, ssd_a_log, ssd_d,
                         gqa_q_norm, gqa_k_norm, w_router_group, w_router_expert)
        lp["wg"] = w_gate[l].astype(BF16)
        lp["wb"] = w_branch[l].astype(BF16)
        lp["wo"] = w_out[l].astype(BF16)
        lp["w1"] = moe_w_gate[l].astype(BF16)
        lp["w3"] = moe_w_up[l].astype(BF16)
        lp["w2"] = moe_w_down[l].astype(BF16)
        layers.append(lp)
    return (_trunk(x_prompt, prm, layers), _trunk(x_sample, prm, layers))
```

```python
import functools
import math

import jax
import jax.numpy as jnp
import numpy as np
from jax import lax
from jax.experimental import pallas as pl
from jax.experimental.pallas import tpu as pltpu

F32 = jnp.float32
BF16 = jnp.bfloat16

D_MODEL = 1024
DEPTH = 2
GRID_W = 64
BRANCH_W = 512
NORM_EPS = 1e-6
MLSTM_HEADS = 4
MLSTM_HD = 128
GQA_HD = 64
GQA_HEADS = 8
GQA_KV_HEADS = 2
ROPE_THETA = 10000.0
SSD_HD = 64
SSD_HEADS = 8
SSD_GROUPS = 2
SSD_STATE = 128
SSD_CONV = 5
DIFF_HD = 64
DIFF_HEADS = 4
MOE_GROUPS = 4
MOE_EPG = 8
MOE_EXPERTS = 32
MOE_FF = 512
CHUNK = 128
LOG2E = 1.4426950408889634

_MLSTM_COLS = 4 * BRANCH_W + 4 * MLSTM_HEADS
_GQA_COLS = GQA_HEADS * GQA_HD + 2 * GQA_KV_HEADS * GQA_HD
_SSD_XBC = BRANCH_W + 2 * SSD_GROUPS * SSD_STATE
_SSD_COLS = BRANCH_W + _SSD_XBC + 2 * SSD_HEADS
_O0 = _MLSTM_COLS
_O1 = _O0 + _GQA_COLS
_O2 = _O1 + _SSD_COLS

ML_Q, ML_K, ML_V, ML_O = 0, 512, 1024, 1536
GQ_Q = 2048
SS_Z = 2560
DF_Q, DF_K, DF_V = 3072, 3584, 4096
SS_X, SS_BC = 4608, 5120
GQ_KV = 5632
P_COLS = 5888
P_PAD = 6144
SMALL_COLS = 128

LANE = 128
VMEM_LIMIT = 48 * 1024 * 1024

MOE_BLK = 256


def _cparams(sem):
    return pltpu.CompilerParams(dimension_semantics=sem, vmem_limit_bytes=VMEM_LIMIT)


def _dot(a, b):
    return jnp.dot(a, b, preferred_element_type=F32)


def _dot_nt(a, b):
    return lax.dot_general(a, b, (((1,), (1,)), ((), ())), preferred_element_type=F32)


def _dot_tn(a, b):
    return lax.dot_general(a, b, (((0,), (0,)), ((), ())), preferred_element_type=F32)


def _dot_hi(a, b):
    return jnp.dot(a, b, preferred_element_type=F32, precision=lax.Precision.HIGHEST)


def _sigmoid(x):
    return 1.0 / (1.0 + jnp.exp(-x))


def _silu(x):
    return x * _sigmoid(x)


def _log_sigmoid(x):
    return jnp.minimum(x, 0.0) - jnp.log(1.0 + jnp.exp(-jnp.abs(x)))


def _softplus(x):
    return jnp.maximum(x, 0.0) + jnp.log(1.0 + jnp.exp(-jnp.abs(x)))


def _main_col_index():
    gq = np.arange(GQA_HEADS * GQA_HD).reshape(GQA_HEADS, GQA_HD // 2, 2)
    gq = np.concatenate([gq[..., 0], gq[..., 1]], axis=-1).reshape(-1)
    gk = np.arange(GQA_KV_HEADS * GQA_HD).reshape(GQA_KV_HEADS, GQA_HD // 2, 2)
    gk = np.concatenate([gk[..., 0], gk[..., 1]], axis=-1).reshape(-1)
    segs = [
        np.arange(0, 4 * BRANCH_W),
        _O0 + gq,
        _O1 + np.arange(0, BRANCH_W),
        _O2 + np.arange(0, 3 * BRANCH_W),
        _O1 + BRANCH_W + np.arange(0, _SSD_XBC),
        _O0 + GQA_HEADS * GQA_HD + gk,
        _O0 + GQA_HEADS * GQA_HD + GQA_KV_HEADS * GQA_HD + np.arange(GQA_KV_HEADS * GQA_HD),
    ]
    idx = np.concatenate(segs)
    assert idx.shape[0] == P_COLS
    return idx


def _small_col_index():
    return np.concatenate([4 * BRANCH_W + np.arange(4 * MLSTM_HEADS),
                           _O1 + BRANCH_W + _SSD_XBC + np.arange(2 * SSD_HEADS)])


_MAIN_IDX = _main_col_index()
_SMALL_IDX = _small_col_index()
_DEINT64 = np.concatenate([np.arange(0, GQA_HD, 2), np.arange(1, GQA_HD, 2)])


def _inproj_kernel(x_ref, g_ref, w_ref, ws_ref, p_ref, h_ref, s_ref, h_scr):
    @pl.when(pl.program_id(1) == 0)
    def _():
        x = x_ref[...]
        ms = jnp.mean(x * x, axis=-1, keepdims=True)
        h = (x * lax.rsqrt(ms + NORM_EPS) * g_ref[...]).astype(BF16)
        h_scr[...] = h
        h_ref[...] = h
        s_ref[...] = _dot(h, ws_ref[...])

    p_ref[...] = _dot(h_scr[...], w_ref[...]).astype(BF16)


def _in_proj(x, g, w_main, w_small, tm=1024, tn=1024):
    T = x.shape[0]
    return pl.pallas_call(
        _inproj_kernel,
        grid=(T // tm, P_PAD // tn),
        in_specs=[
            pl.BlockSpec((tm, D_MODEL), lambda i, j: (i, 0)),
            pl.BlockSpec((1, D_MODEL), lambda i, j: (0, 0)),
            pl.BlockSpec((D_MODEL, tn), lambda i, j: (0, j)),
            pl.BlockSpec((D_MODEL, SMALL_COLS), lambda i, j: (0, 0)),
        ],
        out_specs=[
            pl.BlockSpec((tm, tn), lambda i, j: (i, j)),
            pl.BlockSpec((tm, D_MODEL), lambda i, j: (i, 0)),
            pl.BlockSpec((tm, SMALL_COLS), lambda i, j: (i, 0)),
        ],
        out_shape=[
            jax.ShapeDtypeStruct((T, P_PAD), BF16),
            jax.ShapeDtypeStruct((T, D_MODEL), BF16),
            jax.ShapeDtypeStruct((T, SMALL_COLS), F32),
        ],
        scratch_shapes=[pltpu.VMEM((tm, D_MODEL), BF16)],
        compiler_params=_cparams(("parallel", "arbitrary")),
        name="in_proj",
    )(x, g, w_main, w_small)


def _tri_masks(L, reverse):
    r = lax.broadcasted_iota(jnp.int32, (L, L), 0)
    c = lax.broadcasted_iota(jnp.int32, (L, L), 1)
    return (c >= r) if reverse else (c <= r)


def _mlstm_kernel(*refs, reverse, final):
    q_ref = refs[0]
    ct_scr, n_scr, m_scr = refs[-3:]

    @pl.when(pl.program_id(1) == 0)
    def _():
        ct_scr[...] = jnp.zeros_like(ct_scr)
        n_scr[...] = jnp.zeros_like(n_scr)
        m_scr[...] = jnp.zeros_like(m_scr)

    for bb in range(q_ref.shape[0]):
        _mlstm_chunk(bb, refs, reverse, final)


def _mlstm_chunk(bb, refs, reverse, final):
    if final:
        (q_ref, k_ref, v_ref, sm_ref, gb_ref, o_ref, hb_ref, ng_ref,
         out_ref, ct_scr, n_scr, m_scr) = refs
    else:
        q_ref, k_ref, v_ref, sm_ref, gb_ref, out_ref, ct_scr, n_scr, m_scr = refs
    L = CHUNK
    H = MLSTM_HEADS
    d = MLSTM_HD
    mask = _tri_masks(L, reverse)
    tri = mask.astype(F32)
    gates = sm_ref[bb] + gb_ref[...]
    logf = _log_sigmoid(gates)
    bcum = _dot_hi(tri, logf)
    gates_t = gates.T
    bcum_t = bcum.T
    i_off = 8 if reverse else 0
    f_off = i_off + 4
    edge = 0 if reverse else L - 1
    scale = d ** -0.5
    m_all = m_scr[bb]

    hs = []
    for h in range(H):
        qh = q_ref[bb, :, h * d:(h + 1) * d]
        ks = (k_ref[bb, :, h * d:(h + 1) * d].astype(F32) * scale).astype(BF16)
        vh = v_ref[bb, :, h * d:(h + 1) * d]
        b_col = bcum[:, f_off + h:f_off + h + 1]
        i_col = gates[:, i_off + h:i_off + h + 1]
        a_row = gates_t[i_off + h:i_off + h + 1, :] - bcum_t[f_off + h:f_off + h + 1, :]
        m_prev = m_all[:, h:h + 1]
        dmat = jnp.where(mask, b_col + a_row, -jnp.inf)
        inter = b_col + m_prev
        m_t = jnp.maximum(inter, jnp.max(dmat, axis=1, keepdims=True))
        w_intra = jnp.exp(dmat - m_t)
        w_inter = jnp.exp(inter - m_t)
        s = _dot_nt(qh, ks) * w_intra
        ct = ct_scr[bb * H + h]
        n_row = n_scr[bb * H + h]
        num = _dot(s.astype(BF16), vh) + w_inter * _dot(qh, ct.astype(BF16))
        qn = jnp.sum(qh.astype(F32) * n_row, axis=1, keepdims=True)
        den = jnp.sum(s, axis=1, keepdims=True) + w_inter * qn
        hs.append(num / jnp.maximum(jnp.abs(den), jnp.exp(-m_t)))
        btot = b_col[edge:edge + 1, :]
        dec = btot - b_col + i_col
        m_new = jnp.maximum(btot + m_prev, jnp.max(dec, axis=0, keepdims=True))
        ws = jnp.exp(dec - m_new)
        wc = jnp.exp(btot + m_prev - m_new)
        wsv = (ws * vh.astype(F32)).astype(BF16)
        ct_scr[bb * H + h] = wc * ct + _dot_tn(ks, wsv)
        n_scr[bb * H + h] = wc * n_row + jnp.sum(ws * ks.astype(F32), axis=0, keepdims=True)
        lane = lax.broadcasted_iota(jnp.int32, (1, LANE), 1)
        m_all = jnp.where(lane == h, m_new, m_all)
    m_scr[bb] = m_all

    if not final:
        for h in range(H):
            out_ref[bb, :, h * d:(h + 1) * d] = hs[h]
    else:
        for h in range(H):
            hsum = hs[h] + hb_ref[bb, :, h * d:(h + 1) * d]
            ms = jnp.mean(hsum * hsum, axis=1, keepdims=True)
            y = hsum * lax.rsqrt(ms + NORM_EPS) * ng_ref[:, h * d:(h + 1) * d]
            o = o_ref[bb, :, h * d:(h + 1) * d].astype(F32)
            out_ref[bb, :, h * d:(h + 1) * d] = (_sigmoid(o) * y).astype(BF16)


SCAN_NBAT = 2


def _mlstm_dir(p3, small3, gate_bias_row, B, N, reverse, hb=None, norm_g=None):
    L = CHUNK
    nc = N // L
    nbat = SCAN_NBAT
    final = hb is not None

    def crow(c):
        return (nc - 1 - c) if reverse else c

    W = BRANCH_W
    in_specs = [
        pl.BlockSpec((nbat, L, W), lambda b, c: (b, crow(c), ML_Q // W)),
        pl.BlockSpec((nbat, L, W), lambda b, c: (b, crow(c), ML_K // W)),
        pl.BlockSpec((nbat, L, W), lambda b, c: (b, crow(c), ML_V // W)),
        pl.BlockSpec((nbat, L, SMALL_COLS), lambda b, c: (b, crow(c), 0)),
        pl.BlockSpec((1, SMALL_COLS), lambda b, c: (0, 0)),
    ]
    args = [p3, p3, p3, small3, gate_bias_row]
    if final:
        in_specs += [
            pl.BlockSpec((nbat, L, W), lambda b, c: (b, crow(c), ML_O // W)),
            pl.BlockSpec((nbat, L, W), lambda b, c: (b, crow(c), 0)),
            pl.BlockSpec((1, W), lambda b, c: (0, 0)),
        ]
        args += [p3, hb, norm_g]
    return pl.pallas_call(
        functools.partial(_mlstm_kernel, reverse=reverse, final=final),
        grid=(B // nbat, nc),
        in_specs=in_specs,
        out_specs=pl.BlockSpec((nbat, L, W), lambda b, c: (b, crow(c), 0)),
        out_shape=jax.ShapeDtypeStruct((B, N, W), BF16 if final else F32),
        scratch_shapes=[
            pltpu.VMEM((nbat * MLSTM_HEADS, MLSTM_HD, MLSTM_HD), F32),
            pltpu.VMEM((nbat * MLSTM_HEADS, 1, MLSTM_HD), F32),
            pltpu.VMEM((nbat, 1, LANE), F32),
        ],
        compiler_params=_cparams(("parallel", "arbitrary")),
        name="mlstm_bwd" if reverse else "mlstm_fwd",
    )(*args)


def _mlstm_branch(p, small, gate_bias_row, norm_g, B, N):
    p3 = p.reshape(B, N, P_PAD)
    small3 = small.reshape(B, N, SMALL_COLS)
    hb = _mlstm_dir(p3, small3, gate_bias_row, B, N, reverse=True)
    y = _mlstm_dir(p3, small3, gate_bias_row, B, N, reverse=False, hb=hb, norm_g=norm_g)
    return y.reshape(B * N, BRANCH_W)


def _gqa_prep_kernel(q_ref, kv_ref, cos_ref, sin_ref, qg_ref, kg_ref, bd_ref, qo_ref, ko_ref, vo_ref):
    tm = q_ref.shape[0]
    bd = bd_ref[...]
    cos = cos_ref[...]
    sin = sin_ref[...]
    lane = lax.broadcasted_iota(jnp.int32, (tm, LANE), 1)
    first = (lane % GQA_HD) < (GQA_HD // 2)
    lo = lane < GQA_HD

    def norm_rope(x, g):
        ss = _dot((x * x).astype(BF16), bd)
        xn = x * lax.rsqrt(ss * (1.0 / GQA_HD) + NORM_EPS) * g
        partner = jnp.where(first, pltpu.roll(xn, LANE - GQA_HD // 2, axis=1),
                            pltpu.roll(xn, GQA_HD // 2, axis=1))
        return xn * cos + partner * sin

    qscale = (GQA_HD ** -0.5) * LOG2E
    for a in range(GQA_HEADS * GQA_HD // LANE):
        x = q_ref[:, a * LANE:(a + 1) * LANE].astype(F32)
        qo_ref[:, a * LANE:(a + 1) * LANE] = (norm_rope(x, qg_ref[...]) * qscale).astype(BF16)
    k = norm_rope(kv_ref[:, 0:LANE].astype(F32), kg_ref[...])
    k_sw = pltpu.roll(k, GQA_HD, axis=1)
    ko_ref[0] = jnp.where(lo, k, k_sw).astype(BF16)
    ko_ref[1] = jnp.where(lo, k_sw, k).astype(BF16)
    v = kv_ref[:, LANE:2 * LANE].astype(F32)
    v_sw = pltpu.roll(v, GQA_HD, axis=1)
    vo_ref[0] = jnp.where(lo, v, 1.0).astype(BF16)
    vo_ref[1] = jnp.where(lo, v_sw, 1.0).astype(BF16)


def _gqa_prep(p, cos_t, sin_t, qg, kg, bd, B, N, tm=512):
    T = B * N
    nb = N // tm
    return pl.pallas_call(
        _gqa_prep_kernel,
        grid=(T // tm,),
        in_specs=[
            pl.BlockSpec((tm, 512), lambda i: (i, GQ_Q // 512)),
            pl.BlockSpec((tm, 256), lambda i: (i, GQ_KV // 256)),
            pl.BlockSpec((tm, LANE), lambda i: (i % nb, 0)),
            pl.BlockSpec((tm, LANE), lambda i: (i % nb, 0)),
            pl.BlockSpec((1, LANE), lambda i: (0, 0)),
            pl.BlockSpec((1, LANE), lambda i: (0, 0)),
            pl.BlockSpec((LANE, LANE), lambda i: (0, 0)),
        ],
        out_specs=[
            pl.BlockSpec((tm, 512), lambda i: (i, 0)),
            pl.BlockSpec((GQA_KV_HEADS, tm, LANE), lambda i: (0, i, 0)),
            pl.BlockSpec((GQA_KV_HEADS, tm, LANE), lambda i: (0, i, 0)),
        ],
        out_shape=[
            jax.ShapeDtypeStruct((T, 512), BF16),
            jax.ShapeDtypeStruct((GQA_KV_HEADS, T, LANE), BF16),
            jax.ShapeDtypeStruct((GQA_KV_HEADS, T, LANE), BF16),
        ],
        compiler_params=_cparams(("parallel",)),
        name="gqa_prep",
    )(p, p, cos_t, sin_t, qg, kg, bd)


FLASH_TKB = 512


def _softmax_keys(s_ref, p_ref, m_scr, al_scr, l_scr, M):
    tkb = s_ref.shape[0]
    for c in range(M // LANE):
        cols = slice(c * LANE, (c + 1) * LANE)
        mx = jnp.max(jnp.max(s_ref[:, cols].reshape(tkb // 8, 8, LANE), axis=0), axis=0, keepdims=True)
        m_prev = m_scr[:, cols]
        m_new = jnp.maximum(m_prev, mx)
        alpha = jnp.exp2(m_prev - m_new)
        p = jnp.exp2(s_ref[:, cols] - m_new)
        p_ref[:, cols] = p.astype(BF16)
        if l_scr is not None:
            lsum = jnp.sum(jnp.sum(p.reshape(tkb // 8, 8, LANE), axis=0), axis=0, keepdims=True)
            l_scr[:, cols] = alpha * l_scr[:, cols] + lsum
        al_scr[:, cols] = alpha
        m_scr[:, cols] = m_new


def _flash_blocks(nb, qk, pv, softmax, fix=None):
    qk(0)
    for i in range(nb):
        if i + 1 < nb:
            qk(i + 1)
        if fix is not None:
            fix(i)
        softmax(i)
        pv(i)


FLASH_SAFE_LOG2 = 120.0


def _flash_one_pass(nb, scores, values, m_scr, l_scr, acc_scr, p_scr):
    for i in range(nb):
        s = scores(i)
        tkb, M = s.shape
        m_used = m_scr[...]
        p = jnp.exp2(s - m_used)
        p_scr[i % 2] = p.astype(BF16)
        mx = jnp.max(jnp.max(s.reshape(tkb // 8, 8, M), axis=0), axis=0, keepdims=True)
        m_new = jnp.maximum(m_used, mx)
        alpha = jnp.exp2(m_used - m_new)
        if l_scr is not None:
            lsum = jnp.sum(jnp.sum(p.reshape(tkb // 8, 8, M), axis=0), axis=0, keepdims=True)
            l_scr[...] = alpha * (l_scr[...] + lsum)
        acc_scr[...] = alpha * (acc_scr[...] + _dot_tn(values(i), p_scr[i % 2]))
        m_scr[...] = m_new


def _flash_score_bound(qs, kmax):
    qf = qs.astype(F32)
    n2 = _dot_nt(jnp.ones((8, LANE), BF16), (qf * qf).astype(BF16))[0:1, :]
    return jnp.sqrt(n2) * kmax * 1.01 + 1.0


def _flash_is_safe(u_scr, m_scr):
    return jnp.where(jnp.max(u_scr[...] - m_scr[...]) <= FLASH_SAFE_LOG2, 1, 0)


def _gqa_flash_kernel(q_ref, k_ref, v_ref, kmax_ref, o_ref, qs_scr, m_scr, al_scr, acc_scr, s_scr, p_scr,
                      u_scr, safe_ref):
    tq = q_ref.shape[0]
    M = 4 * tq
    tkb = s_scr.shape[1]
    nb = k_ref.shape[0] // tkb
    kj = pl.program_id(3)

    @pl.when(kj == 0)
    def _():
        lane = lax.broadcasted_iota(jnp.int32, (tq, LANE), 1)
        lo = lane < GQA_HD
        zero = jnp.zeros((tq, LANE), BF16)
        for a in range(2):
            qa = q_ref[:, a * LANE:(a + 1) * LANE]
            qs_scr[(2 * a) * tq:(2 * a + 1) * tq, :] = jnp.where(lo, qa, zero)
            qs_scr[(2 * a + 1) * tq:(2 * a + 2) * tq, :] = jnp.where(lo, zero, qa)
        m_scr[...] = jnp.full_like(m_scr, -jnp.inf)
        acc_scr[...] = jnp.zeros_like(acc_scr)
        u_scr[...] = _flash_score_bound(qs_scr[...], kmax_ref[:, 0:1])
        safe_ref[0] = 0

    def scores(i):
        return _dot_nt(k_ref[i * tkb:(i + 1) * tkb, :], qs_scr[...])

    def values(i):
        return v_ref[i * tkb:(i + 1) * tkb, :]

    def qk(i):
        s_scr[i % 2] = scores(i)

    def softmax(i):
        _softmax_keys(s_scr.at[i % 2], p_scr.at[i % 2], m_scr, al_scr, None, M)

    def pv(i):
        acc_scr[...] = al_scr[...] * acc_scr[...] + _dot_tn(values(i), p_scr[i % 2])

    one_pass = safe_ref[0] == 1

    @pl.when(jnp.logical_not(one_pass))
    def _():
        _flash_blocks(nb, qk, pv, softmax)

    @pl.when(one_pass)
    def _():
        _flash_one_pass(nb, scores, values, m_scr, None, acc_scr, p_scr)

    @pl.when(kj == 0)
    def _():
        safe_ref[0] = _flash_is_safe(u_scr, m_scr)

    @pl.when(kj == pl.num_programs(3) - 1)
    def _():
        inv = 1.0 / acc_scr[GQA_HD:GQA_HD + 1, :]
        ot = jnp.concatenate([acc_scr[0:GQA_HD, h * tq:(h + 1) * tq] * inv[:, h * tq:(h + 1) * tq]
                              for h in range(4)], axis=0)
        o_ref[...] = ot.T.astype(BF16)


def _gqa_flash(qn, k2, va, B, N, tq=256, tk=2048):
    T = B * N
    tk = min(tk, N)
    nq, nk = N // tq, N // tk
    M = 4 * tq
    tkb = FLASH_TKB
    kf = k2[:, :, 0:GQA_HD].astype(F32).reshape(GQA_KV_HEADS, B, N, GQA_HD)
    kmax = jnp.max(jnp.sqrt(jnp.sum(kf * kf, axis=-1)), axis=-1)
    kmax = jnp.broadcast_to(kmax.T.reshape(B * GQA_KV_HEADS, 1, 1), (B * GQA_KV_HEADS, 1, LANE))
    return pl.pallas_call(
        _gqa_flash_kernel,
        grid=(B, GQA_KV_HEADS, nq, nk),
        in_specs=[
            pl.BlockSpec((tq, 256), lambda b, g, i, j: (b * nq + i, g)),
            pl.BlockSpec((None, tk, LANE), lambda b, g, i, j: (g, b * nk + j, 0)),
            pl.BlockSpec((None, tk, LANE), lambda b, g, i, j: (g, b * nk + j, 0)),
            pl.BlockSpec((None, 1, LANE), lambda b, g, i, j: (b * GQA_KV_HEADS + g, 0, 0)),
        ],
        out_specs=pl.BlockSpec((tq, 256), lambda b, g, i, j: (b * nq + i, g)),
        out_shape=jax.ShapeDtypeStruct((T, 512), BF16),
        scratch_shapes=[
            pltpu.VMEM((M, LANE), BF16),
            pltpu.VMEM((1, M), F32),
            pltpu.VMEM((1, M), F32),
            pltpu.VMEM((LANE, M), F32),
            pltpu.VMEM((2, tkb, M), F32),
            pltpu.VMEM((2, tkb, M), BF16),
            pltpu.VMEM((1, M), F32),
            pltpu.SMEM((1,), jnp.int32),
        ],
        compiler_params=_cparams(("parallel", "parallel", "parallel", "arbitrary")),
        name="gqa_flash",
    )(qn, k2, va, kmax)


def _rope_tables(N):
    rows = N // GRID_W
    row = jnp.repeat(jnp.arange(rows, dtype=F32), GRID_W)
    col = (jnp.arange(N) % GRID_W).astype(F32)
    half = GQA_HD // 2
    inv = 1.0 / (ROPE_THETA ** (jnp.arange(0, half, 2, dtype=F32) / half))
    ang = jnp.concatenate([row[:, None] * inv, col[:, None] * inv], axis=-1)
    cos, sin = jnp.cos(ang), jnp.sin(ang)
    cos_h = jnp.concatenate([cos, cos], axis=-1)
    sin_h = jnp.concatenate([-sin, sin], axis=-1)
    return jnp.tile(cos_h, (1, LANE // GQA_HD)), jnp.tile(sin_h, (1, LANE // GQA_HD))


def _gqa_branch(p, cos_t, sin_t, qg, kg, bd, B, N):
    qn, k2, va = _gqa_prep(p, cos_t, sin_t, qg, kg, bd, B, N)
    return _gqa_flash(qn, k2, va, B, N)


DIFF_SKIP_LOG2 = 40.0


def _alibi_slope_log2(h):
    return (2.0 ** (-8.0 * (h + 1) / DIFF_HEADS)) * LOG2E


def _alibi_tables(N):
    pos = np.arange(N)
    hi = ((pos // 128) * 128).astype(np.float32)
    lo = (pos % 128).astype(np.float32)
    kt = np.zeros((DIFF_HEADS, N, LANE), np.float32)
    qt = np.zeros((DIFF_HEADS, N, LANE), np.float32)
    for h in range(DIFF_HEADS):
        s = np.float32(_alibi_slope_log2(h))
        s1 = np.float32(s).astype(BF16).astype(np.float32)
        s2 = np.float32(s - s1).astype(BF16).astype(np.float32)
        for c, val in enumerate((hi, hi, lo, lo)):
            kt[h, :, c] = val
            qt[h, :, 4 + c] = -val
        for c, val in enumerate((s1, s2, s1, s2)):
            kt[h, :, 4 + c] = val
            qt[h, :, c] = val
    return (jnp.asarray(kt.reshape(DIFF_HEADS * N, LANE), BF16), jnp.asarray(qt.reshape(DIFF_HEADS * N, LANE), BF16))


def _diff_flash_kernel(lo_ref, hi_ref, q_ref, qa_ref, k_ref, ka_ref, v_ref, kmax_ref, lp_ref, ng_ref, o_ref,
                       qs_scr, m_scr, l_scr, al_scr, acc_scr, s_scr, p_scr, u_scr, safe_ref, *, lambda_init):
    tq = q_ref.shape[0]
    tk = k_ref.shape[0]
    M = 2 * tq
    tkb = s_scr.shape[1]
    nb = tk // tkb
    b, h, qi, kj = pl.program_id(0), pl.program_id(1), pl.program_id(2), pl.program_id(3)
    idx = (b * pl.num_programs(1) + h) * pl.num_programs(2) + qi
    first, last = lo_ref[idx], hi_ref[idx]

    @pl.when(kj == first)
    def _():
        lane = lax.broadcasted_iota(jnp.int32, (tq, LANE), 1)
        lo = lane < DIFF_HD
        zero = jnp.zeros((tq, LANE), BF16)
        q = q_ref[...]
        qa = qa_ref[...]
        for var, aug in enumerate((qa, -qa)):
            qs_scr[var, 0:tq, 0:LANE] = jnp.where(lo, q, zero)
            qs_scr[var, tq:M, 0:LANE] = jnp.where(lo, zero, q)
            qs_scr[var, 0:tq, LANE:2 * LANE] = aug
            qs_scr[var, tq:M, LANE:2 * LANE] = aug
        m_scr[...] = jnp.full_like(m_scr, -jnp.inf)
        l_scr[...] = jnp.zeros_like(l_scr)
        acc_scr[...] = jnp.zeros_like(acc_scr)
        u_scr[...] = _flash_score_bound(qs_scr[0, :, 0:LANE], kmax_ref[:, 0:1])
        safe_ref[0] = 0

    is_left = (kj + 1) * tk <= qi * tq
    is_right = kj * tk >= (qi + 1) * tq
    pure = jnp.logical_or(is_left, is_right)
    active = jnp.logical_and(kj >= first, kj <= last)
    sel = jnp.where(is_right, 1, 0)

    def scores(i):
        rows = slice(i * tkb, (i + 1) * tkb)
        kk = jnp.concatenate([k_ref[rows, :], ka_ref[rows, :]], axis=1)
        return _dot_nt(kk, qs_scr[sel])

    def values(i):
        return v_ref[i * tkb:(i + 1) * tkb, :]

    def qk(i):
        s_scr[i % 2] = scores(i)

    def fix(i):
        slope = jnp.exp2(-2.0 * (jnp.full((1, 1), h, jnp.int32).astype(F32) + 1.0)) * LOG2E
        j = kj * tk + i * tkb + lax.broadcasted_iota(jnp.int32, (tkb, 1), 0)
        col = lax.broadcasted_iota(jnp.int32, (1, M), 1)
        iq = qi * tq + col - jnp.where(col >= tq, tq, 0)
        d = jnp.maximum(j - iq, 0).astype(F32)
        s_scr[i % 2] = s_scr[i % 2] - (2.0 * slope) * d

    def softmax(i):
        _softmax_keys(s_scr.at[i % 2], p_scr.at[i % 2], m_scr, al_scr, l_scr, M)

    def pv(i):
        acc_scr[...] = al_scr[...] * acc_scr[...] + _dot_tn(values(i), p_scr[i % 2])

    @pl.when(jnp.logical_and(active, pure))
    def _():
        one_pass = jnp.logical_and(safe_ref[0] == 1, kj != first)

        @pl.when(jnp.logical_not(one_pass))
        def _():
            _flash_blocks(nb, qk, pv, softmax)

        @pl.when(one_pass)
        def _():
            _flash_one_pass(nb, scores, values, m_scr, l_scr, acc_scr, p_scr)

    @pl.when(jnp.logical_and(active, jnp.logical_not(pure)))
    def _():
        _flash_blocks(nb, qk, pv, softmax, fix=fix)

    @pl.when(kj == first)
    def _():
        safe_ref[0] = _flash_is_safe(u_scr, m_scr)

    @pl.when(kj == last)
    def _():
        lp = lp_ref[...]
        s01 = jnp.sum(jnp.sum(lp[0:1] * lp[1:2], axis=1, keepdims=True), axis=0, keepdims=True)
        s23 = jnp.sum(jnp.sum(lp[2:3] * lp[3:4], axis=1, keepdims=True), axis=0, keepdims=True)
        lam = jnp.exp(s01) - jnp.exp(s23) + lambda_init
        inv = 1.0 / l_scr[...]
        ot = acc_scr[:, 0:tq] * inv[:, 0:tq] - lam * (acc_scr[:, tq:M] * inv[:, tq:M])
        o = ot.T
        ms = jnp.mean(o * o, axis=1, keepdims=True)
        o_ref[...] = (o * lax.rsqrt(ms + NORM_EPS) * ng_ref[...] * (1.0 - lambda_init)).astype(BF16)


def _diff_prep_kernel(q_ref, k_ref, v_ref, bd_ref, ko_ref, vo_ref, nrm_ref):
    tm = q_ref.shape[0]
    lane = lax.broadcasted_iota(jnp.int32, (tm, LANE), 1)
    nrm = jnp.zeros((tm, LANE), F32)
    for h in range(DIFF_HEADS):
        cols = slice(h * LANE, (h + 1) * LANE)
        ko_ref[h] = k_ref[:, cols]
        vo_ref[h] = v_ref[:, cols]
        for src, off in ((q_ref, 0), (k_ref, DIFF_HEADS)):
            x = src[:, cols].astype(F32)
            ss = _dot_hi(x * x, bd_ref[...])
            n = jnp.sqrt(jnp.maximum(ss, pltpu.roll(ss, DIFF_HD, axis=1)))
            nrm = jnp.where(lane == off + h, n, nrm)
    nrm_ref[...] = nrm


def _diff_prep(p, bd32, B, N, tm=512):
    T = B * N
    W = BRANCH_W
    return pl.pallas_call(
        _diff_prep_kernel,
        grid=(T // tm,),
        in_specs=[
            pl.BlockSpec((tm, W), lambda i: (i, DF_Q // W)),
            pl.BlockSpec((tm, W), lambda i: (i, DF_K // W)),
            pl.BlockSpec((tm, W), lambda i: (i, DF_V // W)),
            pl.BlockSpec((LANE, LANE), lambda i: (0, 0)),
        ],
        out_specs=[
            pl.BlockSpec((DIFF_HEADS, tm, LANE), lambda i: (0, i, 0)),
            pl.BlockSpec((DIFF_HEADS, tm, LANE), lambda i: (0, i, 0)),
            pl.BlockSpec((tm, LANE), lambda i: (i, 0)),
        ],
        out_shape=[
            jax.ShapeDtypeStruct((DIFF_HEADS, T, LANE), BF16),
            jax.ShapeDtypeStruct((DIFF_HEADS, T, LANE), BF16),
            jax.ShapeDtypeStruct((T, LANE), F32),
        ],
        compiler_params=_cparams(("parallel",)),
        name="diff_prep",
    )(p, p, p, bd32)


def _diff_bands(nrm, B, N, tq, tk):
    nq, nkt = N // tq, N // tk
    qn = nrm[:, 0:DIFF_HEADS].reshape(B, N, DIFF_HEADS)
    kn = nrm[:, DIFF_HEADS:2 * DIFF_HEADS].reshape(B, N, DIFF_HEADS)
    qmax = jnp.max(qn.reshape(B, nq, tq, DIFF_HEADS), axis=2)
    kself = jnp.max(kn.reshape(B, nq, tq, DIFF_HEADS), axis=2)
    kmax = jnp.max(kn.reshape(B, nkt, tk, DIFF_HEADS), axis=2)
    bound = qmax[:, :, None, :] * (kmax[:, None, :, :] + kself[:, :, None, :]) * 1.001 + DIFF_SKIP_LOG2
    q0 = np.arange(nq)[:, None] * tq
    k0 = np.arange(nkt)[None, :] * tk
    dist = np.maximum(0, np.maximum(q0 - (k0 + tk - 1), k0 - (q0 + tq - 1))).astype(np.float32)
    slopes = np.array([_alibi_slope_log2(h) for h in range(DIFF_HEADS)], np.float32)
    keep = bound > jnp.asarray(dist)[None, :, :, None] * jnp.asarray(slopes)[None, None, None, :]
    keep = jnp.transpose(keep, (0, 3, 1, 2))
    first = jnp.argmax(keep, axis=-1).astype(jnp.int32)
    last = (nkt - 1 - jnp.argmax(keep[..., ::-1], axis=-1)).astype(jnp.int32)
    return first.reshape(-1), last.reshape(-1)


def _diff_branch(p, ktab, qtab, bd32, lam_params, norm_g, lambda_init, B, N, tq=512, tk=2048):
    T = B * N
    nq, nk = N // tq, N // tk
    M = 2 * tq
    tkb = FLASH_TKB
    kd, vd, nrm = _diff_prep(p, bd32, B, N)
    first, last = _diff_bands(nrm, B, N, tq, tk)
    kmax = jnp.max(nrm[:, DIFF_HEADS:2 * DIFF_HEADS].reshape(B, N, DIFF_HEADS), axis=1)
    kmax = jnp.broadcast_to(kmax.reshape(B * DIFF_HEADS, 1, 1), (B * DIFF_HEADS, 1, LANE))

    def kstep(b, h, i, j, lo, hi):
        idx = (b * DIFF_HEADS + h) * nq + i
        return jnp.minimum(jnp.maximum(j, lo[idx]), hi[idx])

    return pl.pallas_call(
        functools.partial(_diff_flash_kernel, lambda_init=lambda_init),
        grid_spec=pltpu.PrefetchScalarGridSpec(
            num_scalar_prefetch=2,
            grid=(B, DIFF_HEADS, nq, nk),
            in_specs=[
                pl.BlockSpec((tq, LANE), lambda b, h, i, j, lo, hi: (b * nq + i, DF_Q // LANE + h)),
                pl.BlockSpec((tq, LANE), lambda b, h, i, j, lo, hi: (h * nq + i, 0)),
                pl.BlockSpec((None, tk, LANE), lambda b, h, i, j, lo, hi: (h, b * nk + kstep(b, h, i, j, lo, hi), 0)),
                pl.BlockSpec((tk, LANE), lambda b, h, i, j, lo, hi: (h * nk + kstep(b, h, i, j, lo, hi), 0)),
                pl.BlockSpec((None, tk, LANE), lambda b, h, i, j, lo, hi: (h, b * nk + kstep(b, h, i, j, lo, hi), 0)),
                pl.BlockSpec((None, 1, LANE), lambda b, h, i, j, lo, hi: (b * DIFF_HEADS + h, 0, 0)),
                pl.BlockSpec((4, DIFF_HD), lambda b, h, i, j, lo, hi: (0, 0)),
                pl.BlockSpec((1, LANE), lambda b, h, i, j, lo, hi: (0, h)),
            ],
            out_specs=pl.BlockSpec((tq, LANE), lambda b, h, i, j, lo, hi: (b * nq + i, h)),
            scratch_shapes=[
                pltpu.VMEM((2, M, 2 * LANE), BF16),
                pltpu.VMEM((1, M), F32),
                pltpu.VMEM((1, M), F32),
                pltpu.VMEM((1, M), F32),
                pltpu.VMEM((LANE, M), F32),
                pltpu.VMEM((2, tkb, M), F32),
                pltpu.VMEM((2, tkb, M), BF16),
                pltpu.VMEM((1, M), F32),
                pltpu.SMEM((1,), jnp.int32),
            ],
        ),
        out_shape=jax.ShapeDtypeStruct((T, BRANCH_W), BF16),
        compiler_params=_cparams(("parallel", "parallel", "parallel", "arbitrary")),
        name="diff_flash",
    )(first, last, p, qtab, kd, ktab, vd, kmax, lam_params, norm_g)


SSD_HALO = 16


def _ssd_prep_kernel(x_ref, xp_ref, xn_ref, bc_ref, bcp_ref, bcn_ref, sm_ref, cw_ref, cb_ref,
                     dtb_ref, arow_ref, xo_ref, bco_ref, dto_ref):
    tc = x_ref.shape[0]
    c = pl.program_id(1)
    nc = pl.num_programs(1)
    has_prev = jnp.where(c > 0, 1.0, 0.0)
    has_next = jnp.where(c < nc - 1, 1.0, 0.0)
    pad = SSD_CONV // 2

    def conv(main_ref, prev_ref, next_ref, off):
        xf = jnp.concatenate([prev_ref[...].astype(F32) * has_prev, main_ref[...].astype(F32),
                              next_ref[...].astype(F32) * has_next], axis=0)
        n = xf.shape[0]
        acc = jnp.zeros((tc, xf.shape[1]), F32) + cb_ref[:, off:off + xf.shape[1]]
        for j in range(SSD_CONV):
            sh = pltpu.roll(xf, (pad - j) % n, axis=0)[SSD_HALO:SSD_HALO + tc, :]
            acc = acc + sh * cw_ref[j:j + 1, off:off + xf.shape[1]]
        return _silu(acc)

    xo_ref[...] = conv(x_ref, xp_ref, xn_ref, 0).astype(BF16)
    bco_ref[...] = conv(bc_ref, bcp_ref, bcn_ref, BRANCH_W).astype(BF16)
    dt = _softplus(sm_ref[...] + dtb_ref[...])
    a = pltpu.roll(dt * arow_ref[...], LANE - 16, axis=1)
    lane = lax.broadcasted_iota(jnp.int32, (tc, LANE), 1)
    dto_ref[...] = jnp.where(lane < 16, a, jnp.where(lane < 32, dt, 0.0))


def _ssd_prep(p, small, conv_w, conv_b, dtb_row, a_row, B, N, tc=512):
    T = B * N
    nc = N // tc
    hb = tc // SSD_HALO
    W = BRANCH_W

    def main(col):
        return pl.BlockSpec((tc, W), lambda b, c: (b * nc + c, col))

    def prev(col):
        return pl.BlockSpec((SSD_HALO, W), lambda b, c: (jnp.maximum((b * nc + c) * hb - 1, 0), col))

    def nxt(col):
        return pl.BlockSpec((SSD_HALO, W), lambda b, c: (jnp.minimum((b * nc + c + 1) * hb, T // SSD_HALO - 1), col))

    cx, cbc = SS_X // W, SS_BC // W
    return pl.pallas_call(
        _ssd_prep_kernel,
        grid=(B, nc),
        in_specs=[
            main(cx), prev(cx), nxt(cx), main(cbc), prev(cbc), nxt(cbc),
            pl.BlockSpec((tc, SMALL_COLS), lambda b, c: (b * nc + c, 0)),
            pl.BlockSpec((SSD_CONV, 2 * W), lambda b, c: (0, 0)),
            pl.BlockSpec((1, 2 * W), lambda b, c: (0, 0)),
            pl.BlockSpec((1, LANE), lambda b, c: (0, 0)),
            pl.BlockSpec((1, LANE), lambda b, c: (0, 0)),
        ],
        out_specs=[
            pl.BlockSpec((tc, W), lambda b, c: (b * nc + c, 0)),
            pl.BlockSpec((tc, W), lambda b, c: (b * nc + c, 0)),
            pl.BlockSpec((tc, LANE), lambda b, c: (b * nc + c, 0)),
        ],
        out_shape=[
            jax.ShapeDtypeStruct((T, W), BF16),
            jax.ShapeDtypeStruct((T, W), BF16),
            jax.ShapeDtypeStruct((T, LANE), F32),
        ],
        compiler_params=_cparams(("parallel", "parallel")),
        name="ssd_prep",
    )(p, p, p, p, p, p, small, conv_w, conv_b, dtb_row, a_row)


def _ssd_kernel(*refs, reverse, final):
    x_ref = refs[0]
    hs_scr = refs[-1]

    @pl.when(pl.program_id(1) == 0)
    def _():
        hs_scr[...] = jnp.zeros_like(hs_scr)

    for bb in range(x_ref.shape[0]):
        _ssd_chunk(bb, refs, reverse, final)


def _ssd_chunk(bb, refs, reverse, final):
    if final:
        (x_ref, bc_ref, dta_ref, ea_ref, ed_ref, z_ref, yb_ref, dsk_ref, ng_ref,
         out_ref, hs_scr) = refs
    else:
        x_ref, bc_ref, dta_ref, ea_ref, ed_ref, out_ref, hs_scr = refs
    L = CHUNK
    S = SSD_STATE
    W = BRANCH_W
    GW = W // SSD_GROUPS
    mask = _tri_masks(L, reverse)
    tri = mask.astype(F32)
    dta = dta_ref[bb]
    acs = _dot_hi(tri, dta)
    acs_t = acs.T
    acs_e = _dot_hi(acs, ea_ref[...])
    dt_e = _dot_hi(dta, ed_ref[...])
    edge = 0 if reverse else L - 1
    a_off = 8 if reverse else 0
    acs_end = acs_e[edge:edge + 1, :]
    xf = x_ref[bb].astype(F32)
    xdt = xf * dt_e
    xdt_b = xdt.astype(BF16)
    xdend = (xdt * jnp.exp(acs_end - acs_e)).astype(BF16)
    e_acs = jnp.exp(acs_e)
    cdec = jnp.exp(acs_end)
    lane = lax.broadcasted_iota(jnp.int32, (L, LANE), 1)
    lo = lane < SSD_HD
    zero = jnp.zeros((L, LANE), BF16)

    ys = []
    for g in range(SSD_GROUPS):
        bg = bc_ref[bb, :, g * S:(g + 1) * S]
        cg = bc_ref[bb, :, SSD_GROUPS * S + g * S:SSD_GROUPS * S + (g + 1) * S]
        cb = _dot_nt(cg, bg)
        hs = hs_scr[bb * SSD_GROUPS + g]
        y_off = _dot(cg, hs.astype(BF16)) * e_acs[:, g * GW:(g + 1) * GW]
        hs_scr[bb * SSD_GROUPS + g] = (cdec[:, g * GW:(g + 1) * GW] * hs
                                       + _dot_tn(bg, xdend[:, g * GW:(g + 1) * GW]))
        for pr in range(GW // LANE):
            col = g * GW + pr * LANE
            xp = xdt_b[:, col:col + LANE]
            yd = jnp.zeros((L, LANE), F32)
            for hh in range(2):
                hd = (col // SSD_HD) + hh
                a_col = acs[:, a_off + hd:a_off + hd + 1]
                a_row = acs_t[a_off + hd:a_off + hd + 1, :]
                decay = jnp.exp(jnp.where(mask, a_col - a_row, -jnp.inf))
                wm = (cb * decay).astype(BF16)
                xm = jnp.where(lo, xp, zero) if hh == 0 else jnp.where(lo, zero, xp)
                yd = yd + _dot(wm, xm)
            ys.append(yd + y_off[:, pr * LANE:(pr + 1) * LANE])
    y = jnp.concatenate(ys, axis=1)

    if not final:
        out_ref[bb] = y
    else:
        y = y + yb_ref[bb] + dsk_ref[...] * xf
        y = y * _silu(z_ref[bb].astype(F32))
        outs = []
        for g in range(SSD_GROUPS):
            yg = y[:, g * GW:(g + 1) * GW]
            ms = jnp.mean(yg * yg, axis=1, keepdims=True)
            outs.append(yg * lax.rsqrt(ms + NORM_EPS))
        out_ref[bb] = (jnp.concatenate(outs, axis=1) * ng_ref[...]).astype(BF16)


def _ssd_dir(xs, bc, dta, ea, ed, B, N, reverse, p3=None, yb=None, dskip_row=None, norm_g=None):
    L = CHUNK
    nc = N // L
    nbat = SCAN_NBAT
    W = BRANCH_W
    final = yb is not None

    def crow(c):
        return (nc - 1 - c) if reverse else c

    in_specs = [
        pl.BlockSpec((nbat, L, W), lambda b, c: (b, crow(c), 0)),
        pl.BlockSpec((nbat, L, W), lambda b, c: (b, crow(c), 0)),
        pl.BlockSpec((nbat, L, LANE), lambda b, c: (b, crow(c), 0)),
        pl.BlockSpec((LANE, W), lambda b, c: (0, 0)),
        pl.BlockSpec((LANE, W), lambda b, c: (0, 0)),
    ]
    args = [xs, bc, dta, ea, ed]
    if final:
        in_specs += [
            pl.BlockSpec((nbat, L, W), lambda b, c: (b, crow(c), SS_Z // W)),
            pl.BlockSpec((nbat, L, W), lambda b, c: (b, crow(c), 0)),
            pl.BlockSpec((1, W), lambda b, c: (0, 0)),
            pl.BlockSpec((1, W), lambda b, c: (0, 0)),
        ]
        args += [p3, yb, dskip_row, norm_g]
    return pl.pallas_call(
        functools.partial(_ssd_kernel, reverse=reverse, final=final),
        grid=(B // nbat, nc),
        in_specs=in_specs,
        out_specs=pl.BlockSpec((nbat, L, W), lambda b, c: (b, crow(c), 0)),
        out_shape=jax.ShapeDtypeStruct((B, N, W), BF16 if final else F32),
        scratch_shapes=[pltpu.VMEM((nbat * SSD_GROUPS, SSD_STATE, W // SSD_GROUPS), F32)],
        compiler_params=_cparams(("parallel", "arbitrary")),
        name="ssd_bwd" if reverse else "ssd_fwd",
    )(*args)


def _ssd_expand_mats():
    ea_f = np.zeros((LANE, BRANCH_W), np.float32)
    ea_b = np.zeros((LANE, BRANCH_W), np.float32)
    ed_f = np.zeros((LANE, BRANCH_W), np.float32)
    ed_b = np.zeros((LANE, BRANCH_W), np.float32)
    for h in range(SSD_HEADS):
        ea_f[h, h * SSD_HD:(h + 1) * SSD_HD] = 1.0
        ea_b[8 + h, h * SSD_HD:(h + 1) * SSD_HD] = 1.0
        ed_f[16 + h, h * SSD_HD:(h + 1) * SSD_HD] = 1.0
        ed_b[24 + h, h * SSD_HD:(h + 1) * SSD_HD] = 1.0
    return ea_f, ea_b, ed_f, ed_b


_EA_F, _EA_B, _ED_F, _ED_B = _ssd_expand_mats()


def _ssd_branch(p, small, conv_w, conv_b, dtb_row, a_row, dskip_row, norm_g, B, N):
    xs, bc, dta = _ssd_prep(p, small, conv_w, conv_b, dtb_row, a_row, B, N)
    xs, bc, dta = (a.reshape(B, N, a.shape[-1]) for a in (xs, bc, dta))
    yb = _ssd_dir(xs, bc, dta, jnp.asarray(_EA_B), jnp.asarray(_ED_B), B, N, reverse=True)
    y = _ssd_dir(xs, bc, dta, jnp.asarray(_EA_F), jnp.asarray(_ED_F), B, N, reverse=False,
                 p3=p.reshape(B, N, P_PAD), yb=yb, dskip_row=dskip_row, norm_g=norm_g)
    return y.reshape(B * N, BRANCH_W)


def _merge_kernel(h_ref, y0_ref, y1_ref, y2_ref, y3_ref, wg_ref, wb_ref, o_ref):
    h = h_ref[...]
    acc = None
    for i, y_ref in enumerate((y0_ref, y1_ref, y2_ref, y3_ref)):
        gate = _sigmoid(_dot(h, wg_ref[i]))
        term = gate * _dot(y_ref[...], wb_ref[i])
        acc = term if acc is None else acc + term
    o_ref[...] = acc.astype(BF16)


def _merge(h, ys, wg, wb, tm=1024, tn=256):
    T = h.shape[0]
    return pl.pallas_call(
        _merge_kernel,
        grid=(T // tm, D_MODEL // tn),
        in_specs=[pl.BlockSpec((tm, D_MODEL), lambda i, j: (i, 0))]
        + [pl.BlockSpec((tm, BRANCH_W), lambda i, j: (i, 0))] * 4
        + [
            pl.BlockSpec((4, D_MODEL, tn), lambda i, j: (0, 0, j)),
            pl.BlockSpec((4, BRANCH_W, tn), lambda i, j: (0, 0, j)),
        ],
        out_specs=pl.BlockSpec((tm, tn), lambda i, j: (i, j)),
        out_shape=jax.ShapeDtypeStruct((T, D_MODEL), BF16),
        compiler_params=_cparams(("parallel", "arbitrary")),
        name="merge",
    )(h, *ys, wg, wb)


def _outproj_kernel(x_ref, m_ref, wo_ref, g_ref, wr_ref, xo_ref, h2_ref, r_ref):
    x = x_ref[...] + _dot(m_ref[...], wo_ref[...])
    xo_ref[...] = x
    ms = jnp.mean(x * x, axis=-1, keepdims=True)
    h2 = x * lax.rsqrt(ms + NORM_EPS) * g_ref[...]
    h2_ref[...] = h2.astype(BF16)
    logits = _dot_hi(h2, wr_ref[...])
    tm = x.shape[0]
    lane = lax.broadcasted_iota(jnp.int32, (tm, LANE), 1).astype(F32)
    neg = -jnp.inf
    big = float(LANE)
    gl = jnp.where(lane < MOE_GROUPS, logits, neg)
    gmax = jnp.max(gl, axis=1, keepdims=True)
    g_idx = jnp.min(jnp.where(gl == gmax, lane, big), axis=1, keepdims=True)
    g_w = 1.0 / jnp.sum(jnp.exp(gl - gmax), axis=1, keepdims=True)
    e_lo = MOE_GROUPS + g_idx * MOE_EPG
    el = jnp.where(lane >= e_lo, jnp.where(lane < e_lo + MOE_EPG, logits, neg), neg)
    m1 = jnp.max(el, axis=1, keepdims=True)
    i1 = jnp.min(jnp.where(el == m1, lane, big), axis=1, keepdims=True)
    el2 = jnp.where(lane == i1, neg, el)
    m2 = jnp.max(el2, axis=1, keepdims=True)
    i2 = jnp.min(jnp.where(el2 == m2, lane, big), axis=1, keepdims=True)
    den = jnp.sum(jnp.exp(el - m1), axis=1, keepdims=True)
    p1 = 1.0 / den
    p2 = jnp.exp(m2 - m1) / den
    w1 = p1 / (p1 + p2) * g_w
    w2 = p2 / (p1 + p2) * g_w
    e1 = i1 - MOE_GROUPS
    e2 = i2 - MOE_GROUPS
    r_ref[...] = jnp.where(lane == 0, e1, jnp.where(lane == 1, e2, jnp.where(lane == 2, w1, jnp.where(lane == 3, w2, 0.0))))


def _out_proj(x, merged, wo, g, wr, tm=512):
    T = x.shape[0]
    return pl.pallas_call(
        _outproj_kernel,
        grid=(T // tm,),
        in_specs=[
            pl.BlockSpec((tm, D_MODEL), lambda i: (i, 0)),
            pl.BlockSpec((tm, D_MODEL), lambda i: (i, 0)),
            pl.BlockSpec((D_MODEL, D_MODEL), lambda i: (0, 0)),
            pl.BlockSpec((1, D_MODEL), lambda i: (0, 0)),
            pl.BlockSpec((D_MODEL, LANE), lambda i: (0, 0)),
        ],
        out_specs=[
            pl.BlockSpec((tm, D_MODEL), lambda i: (i, 0)),
            pl.BlockSpec((tm, D_MODEL), lambda i: (i, 0)),
            pl.BlockSpec((tm, LANE), lambda i: (i, 0)),
        ],
        out_shape=[
            jax.ShapeDtypeStruct((T, D_MODEL), F32),
            jax.ShapeDtypeStruct((T, D_MODEL), BF16),
            jax.ShapeDtypeStruct((T, LANE), F32),
        ],
        compiler_params=_cparams(("parallel",)),
        name="out_proj_router",
    )(x, merged, wo, g, wr)


def _moe_ffn_kernel(be_ref, nu_ref, xs_ref, w1_ref, w3_ref, w2_ref, ys_ref):
    used = pl.program_id(0) < nu_ref[0]

    @pl.when(used)
    def _():
        xb = xs_ref[...]
        a = _dot(xb, w1_ref[0])
        u = _dot(xb, w3_ref[0])
        hmid = (_silu(a) * u).astype(BF16)
        ys_ref[...] = _dot(hmid, w2_ref[0]).astype(BF16)

    @pl.when(jnp.logical_not(used))
    def _():
        ys_ref[...] = jnp.zeros_like(ys_ref)


def _moe_ffn(xs, blk_e, nused, w1, w3, w2):
    R = xs.shape[0]
    nblk = R // MOE_BLK
    return pl.pallas_call(
        _moe_ffn_kernel,
        grid_spec=pltpu.PrefetchScalarGridSpec(
            num_scalar_prefetch=2,
            grid=(nblk,),
            in_specs=[
                pl.BlockSpec((MOE_BLK, D_MODEL), lambda b, be, nu: (b, 0)),
                pl.BlockSpec((1, D_MODEL, MOE_FF), lambda b, be, nu: (be[b], 0, 0)),
                pl.BlockSpec((1, D_MODEL, MOE_FF), lambda b, be, nu: (be[b], 0, 0)),
                pl.BlockSpec((1, MOE_FF, D_MODEL), lambda b, be, nu: (be[b], 0, 0)),
            ],
            out_specs=pl.BlockSpec((MOE_BLK, D_MODEL), lambda b, be, nu: (b, 0)),
        ),
        out_shape=jax.ShapeDtypeStruct((R, D_MODEL), BF16),
        compiler_params=_cparams(("arbitrary",)),
        name="moe_ffn",
    )(blk_e, nused, xs, w1, w3, w2)


def _combine_kernel(x_ref, a_ref, b_ref, r_ref, g_ref, o_ref, *, final):
    r = r_ref[...]
    x = x_ref[...] + (r[:, 2:3] * a_ref[...].astype(F32) + r[:, 3:4] * b_ref[...].astype(F32))
    if final:
        ms = jnp.mean(x * x, axis=-1, keepdims=True)
        x = x * lax.rsqrt(ms + NORM_EPS) * g_ref[...]
    o_ref[...] = x


def _combine(x, a, b, route, g, final, tm=512):
    T = x.shape[0]
    return pl.pallas_call(
        functools.partial(_combine_kernel, final=final),
        grid=(T // tm,),
        in_specs=[pl.BlockSpec((tm, D_MODEL), lambda i: (i, 0))] * 3
        + [pl.BlockSpec((tm, LANE), lambda i: (i, 0)), pl.BlockSpec((1, D_MODEL), lambda i: (0, 0))],
        out_specs=pl.BlockSpec((tm, D_MODEL), lambda i: (i, 0)),
        out_shape=jax.ShapeDtypeStruct((T, D_MODEL), F32),
        compiler_params=_cparams(("parallel",)),
        name="moe_combine",
    )(x, a, b, route, g)


def _moe(x, h2, route, w1, w3, w2, g_final, final):
    T = x.shape[0]
    K = 2
    e_idx = route[:, 0:K].astype(jnp.int32)
    flat_e = e_idx.reshape(-1)
    onehot = (flat_e[:, None] == jnp.arange(MOE_EXPERTS)[None, :]).astype(jnp.int32)
    csum = jnp.cumsum(onehot, axis=0)
    rank = jnp.take_along_axis(csum, flat_e[:, None], axis=1)[:, 0] - 1
    counts = csum[-1]
    pcounts = (counts + MOE_BLK - 1) // MOE_BLK * MOE_BLK
    pends = jnp.cumsum(pcounts)
    pstarts = pends - pcounts
    dest = pstarts[flat_e] + rank
    R = T * K + MOE_EXPERTS * MOE_BLK
    nblk = R // MOE_BLK
    row_src = jnp.zeros((R,), jnp.int32).at[dest].set(jnp.arange(T * K, dtype=jnp.int32) // K)
    nused = (pends[-1] // MOE_BLK).astype(jnp.int32)
    blk_start = jnp.arange(nblk, dtype=jnp.int32) * MOE_BLK
    blk_e = jnp.minimum(jnp.sum((pends[None, :] <= blk_start[:, None]).astype(jnp.int32), axis=1), MOE_EXPERTS - 1)
    last_e = blk_e[jnp.maximum(nused - 1, 0)]
    blk_e = jnp.where(jnp.arange(nblk) < nused, blk_e, last_e)
    xs = jnp.take(h2, row_src, axis=0)
    ys = _moe_ffn(xs, blk_e, nused.reshape(1), w1, w3, w2)
    pos = dest.reshape(T, K)
    ga = jnp.take(ys, pos[:, 0], axis=0)
    gb = jnp.take(ys, pos[:, 1], axis=0)
    return _combine(x, ga, gb, route, g_final, final)


def _prep_layer(l, w_in, mlstm_gate_bias, ssd_conv_w, ssd_conv_b, ssd_dt_bias, ssd_a_log, ssd_d,
                gqa_q_norm, gqa_k_norm, w_router_group, w_router_expert):
    w = w_in[l]
    wm = jnp.take(w, jnp.asarray(_MAIN_IDX), axis=1)
    dscale = jnp.ones((P_COLS,), F32).at[DF_Q:DF_Q + BRANCH_W].set((DIFF_HD ** -0.5) * LOG2E)
    wm = jnp.pad((wm * dscale[None, :]).astype(BF16), ((0, 0), (0, P_PAD - P_COLS)))
    ws = jnp.zeros((D_MODEL, SMALL_COLS), F32).at[:, :32].set(jnp.take(w, jnp.asarray(_SMALL_IDX), axis=1)).astype(BF16)
    gb_row = jnp.zeros((1, SMALL_COLS), F32).at[0, :16].set(mlstm_gate_bias[l])
    dtb_row = jnp.zeros((1, LANE), F32).at[0, 16:32].set(ssd_dt_bias[l].reshape(-1))
    a_row = jnp.zeros((1, LANE), F32).at[0, 16:32].set((-jnp.exp(ssd_a_log[l])).reshape(-1))
    dskip_row = jnp.repeat(ssd_d[l], SSD_HD)[None, :]
    qg = jnp.tile(gqa_q_norm[l][jnp.asarray(_DEINT64)], LANE // GQA_HD)[None, :]
    kg = jnp.tile(gqa_k_norm[l][jnp.asarray(_DEINT64)], LANE // GQA_HD)[None, :]
    wr = jnp.zeros((D_MODEL, LANE), F32).at[:, :MOE_GROUPS].set(w_router_group[l])
    wr = wr.at[:, MOE_GROUPS:MOE_GROUPS + MOE_EXPERTS].set(w_router_expert[l])
    return dict(wm=wm, ws=ws, gb_row=gb_row, dtb_row=dtb_row, a_row=a_row, dskip_row=dskip_row,
                qg=qg, kg=kg, wr=wr, conv_w=ssd_conv_w[l], conv_b=ssd_conv_b[l][None, :])


def _bd_ones():
    i = np.arange(LANE)
    return (i[:, None] // GQA_HD == i[None, :] // GQA_HD).astype(np.float32)


def _trunk(x3, prm, L):
    B, N, _ = x3.shape
    x = x3.reshape(B * N, D_MODEL)
    cos_t, sin_t = _rope_tables(N)
    bd32 = jnp.asarray(_bd_ones())
    bd = bd32.astype(BF16)
    ktab, qtab = _alibi_tables(N)
    for l in range(DEPTH):
        lp = L[l]
        lambda_init = 0.8 - 0.6 * math.exp(-0.3 * l)
        p, h, small = _in_proj(x, prm["norm_mix"][l][None, :], lp["wm"], lp["ws"])
        y0 = _mlstm_branch(p, small, lp["gb_row"], prm["mlstm_norm"][l][None, :], B, N)
        y1 = _gqa_branch(p, cos_t, sin_t, lp["qg"], lp["kg"], bd, B, N)
        y2 = _ssd_branch(p, small, lp["conv_w"], lp["conv_b"], lp["dtb_row"], lp["a_row"], lp["dskip_row"],
                         prm["ssd_norm"][l][None, :], B, N)
        y3 = _diff_branch(p, ktab, qtab, bd32, prm["diff_lambda"][l], prm["diff_norm"][l][None, :], lambda_init,
                          B, N)
        merged = _merge(h, (y0, y1, y2, y3), lp["wg"], lp["wb"])
        x, h2, route = _out_proj(x, merged, lp["wo"], prm["norm_ffn"][l][None, :], lp["wr"])
        x = _moe(x, h2, route, lp["w1"], lp["w3"], lp["w2"], prm["norm_final"][None, :], final=(l == DEPTH - 1))
    return x.reshape(B, N, D_MODEL)


def kernel(x_prompt, x_sample, norm_mix, w_in, mlstm_gate_bias, mlstm_norm, gqa_q_norm, gqa_k_norm, ssd_conv_w, ssd_conv_b, ssd_dt_bias, ssd_a_log, ssd_d, ssd_norm, diff_lambda, diff_norm, w_branch, w_gate, w_out, norm_ffn, w_router_group, w_router_expert, moe_w_gate, moe_w_up, moe_w_down, norm_final):
    prm = dict(norm_mix=norm_mix, mlstm_norm=mlstm_norm, ssd_norm=ssd_norm, diff_lambda=diff_lambda,
               diff_norm=diff_norm, norm_ffn=norm_ffn, norm_final=norm_final)
    layers = []
    for l in range(DEPTH):
        lp = _prep_layer(l, w_in, mlstm_gate_bias, ssd_conv_w, ssd_conv_b, ssd_dt_bias, ssd_a_log, ssd_d,
                         gqa_q_norm, gqa_k_norm, w_router_group, w_router_expert)
        lp["wg"] = w_gate[l].astype(BF16)
        lp["wb"] = w_branch[l].astype(BF16)
        lp["wo"] = w_out[l].astype(BF16)
        lp["w1"] = moe_w_gate[l].astype(BF16)
        lp["w3"] = moe_w_up[l].astype(BF16)
        lp["w2"] = moe_w_down[l].astype(BF16)
        layers.append(lp)
    return (_trunk(x_prompt, prm, layers), _trunk(x_sample, prm, layers))
```

```python
import functools
import math

import jax
import jax.numpy as jnp
import numpy as np
from jax import lax
from jax.experimental import pallas as pl
from jax.experimental.pallas import tpu as pltpu

F32 = jnp.float32
BF16 = jnp.bfloat16

D_MODEL = 1024
DEPTH = 2
GRID_W = 64
BRANCH_W = 512
NORM_EPS = 1e-6
MLSTM_HEADS = 4
MLSTM_HD = 128
GQA_HD = 64
GQA_HEADS = 8
GQA_KV_HEADS = 2
ROPE_THETA = 10000.0
SSD_HD = 64
SSD_HEADS = 8
SSD_GROUPS = 2
SSD_STATE = 128
SSD_CONV = 5
DIFF_HD = 64
DIFF_HEADS = 4
MOE_GROUPS = 4
MOE_EPG = 8
MOE_EXPERTS = 32
MOE_FF = 512
CHUNK = 128
LOG2E = 1.4426950408889634

_MLSTM_COLS = 4 * BRANCH_W + 4 * MLSTM_HEADS
_GQA_COLS = GQA_HEADS * GQA_HD + 2 * GQA_KV_HEADS * GQA_HD
_SSD_XBC = BRANCH_W + 2 * SSD_GROUPS * SSD_STATE
_SSD_COLS = BRANCH_W + _SSD_XBC + 2 * SSD_HEADS
_O0 = _MLSTM_COLS
_O1 = _O0 + _GQA_COLS
_O2 = _O1 + _SSD_COLS

ML_Q, ML_K, ML_V, ML_O = 0, 512, 1024, 1536
GQ_Q = 2048
SS_Z = 2560
DF_Q, DF_K, DF_V = 3072, 3584, 4096
SS_X, SS_BC = 4608, 5120
GQ_KV = 5632
P_COLS = 5888
P_PAD = 6144
SMALL_COLS = 128

LANE = 128
VMEM_LIMIT = 48 * 1024 * 1024

MOE_BLK = 256


def _cparams(sem):
    return pltpu.CompilerParams(dimension_semantics=sem, vmem_limit_bytes=VMEM_LIMIT)


def _dot(a, b):
    return jnp.dot(a, b, preferred_element_type=F32)


def _dot_nt(a, b):
    return lax.dot_general(a, b, (((1,), (1,)), ((), ())), preferred_element_type=F32)


def _dot_tn(a, b):
    return lax.dot_general(a, b, (((0,), (0,)), ((), ())), preferred_element_type=F32)


def _dot_hi(a, b):
    return jnp.dot(a, b, preferred_element_type=F32, precision=lax.Precision.HIGHEST)


def _sigmoid(x):
    return 1.0 / (1.0 + jnp.exp(-x))


def _silu(x):
    return x * _sigmoid(x)


def _log_sigmoid(x):
    return jnp.minimum(x, 0.0) - jnp.log(1.0 + jnp.exp(-jnp.abs(x)))


def _softplus(x):
    return jnp.maximum(x, 0.0) + jnp.log(1.0 + jnp.exp(-jnp.abs(x)))


def _main_col_index():
    gq = np.arange(GQA_HEADS * GQA_HD).reshape(GQA_HEADS, GQA_HD // 2, 2)
    gq = np.concatenate([gq[..., 0], gq[..., 1]], axis=-1).reshape(-1)
    gk = np.arange(GQA_KV_HEADS * GQA_HD).reshape(GQA_KV_HEADS, GQA_HD // 2, 2)
    gk = np.concatenate([gk[..., 0], gk[..., 1]], axis=-1).reshape(-1)
    segs = [
        np.arange(0, 4 * BRANCH_W),
        _O0 + gq,
        _O1 + np.arange(0, BRANCH_W),
        _O2 + np.arange(0, 3 * BRANCH_W),
        _O1 + BRANCH_W + np.arange(0, _SSD_XBC),
        _O0 + GQA_HEADS * GQA_HD + gk,
        _O0 + GQA_HEADS * GQA_HD + GQA_KV_HEADS * GQA_HD + np.arange(GQA_KV_HEADS * GQA_HD),
    ]
    idx = np.concatenate(segs)
    assert idx.shape[0] == P_COLS
    return idx


def _small_col_index():
    return np.concatenate([4 * BRANCH_W + np.arange(4 * MLSTM_HEADS),
                           _O1 + BRANCH_W + _SSD_XBC + np.arange(2 * SSD_HEADS)])


_MAIN_IDX = _main_col_index()
_SMALL_IDX = _small_col_index()
_DEINT64 = np.concatenate([np.arange(0, GQA_HD, 2), np.arange(1, GQA_HD, 2)])


def _inproj_kernel(x_ref, g_ref, w_ref, ws_ref, p_ref, h_ref, s_ref, h_scr):
    @pl.when(pl.program_id(1) == 0)
    def _():
        x = x_ref[...]
        ms = jnp.mean(x * x, axis=-1, keepdims=True)
        h = (x * lax.rsqrt(ms + NORM_EPS) * g_ref[...]).astype(BF16)
        h_scr[...] = h
        h_ref[...] = h
        s_ref[...] = _dot(h, ws_ref[...])

    p_ref[...] = _dot(h_scr[...], w_ref[...]).astype(BF16)


def _in_proj(x, g, w_main, w_small, tm=1024, tn=1024):
    T = x.shape[0]
    return pl.pallas_call(
        _inproj_kernel,
        grid=(T // tm, P_PAD // tn),
        in_specs=[
            pl.BlockSpec((tm, D_MODEL), lambda i, j: (i, 0)),
            pl.BlockSpec((1, D_MODEL), lambda i, j: (0, 0)),
            pl.BlockSpec((D_MODEL, tn), lambda i, j: (0, j)),
            pl.BlockSpec((D_MODEL, SMALL_COLS), lambda i, j: (0, 0)),
        ],
        out_specs=[
            pl.BlockSpec((tm, tn), lambda i, j: (i, j)),
            pl.BlockSpec((tm, D_MODEL), lambda i, j: (i, 0)),
            pl.BlockSpec((tm, SMALL_COLS), lambda i, j: (i, 0)),
        ],
        out_shape=[
            jax.ShapeDtypeStruct((T, P_PAD), BF16),
            jax.ShapeDtypeStruct((T, D_MODEL), BF16),
            jax.ShapeDtypeStruct((T, SMALL_COLS), F32),
        ],
        scratch_shapes=[pltpu.VMEM((tm, D_MODEL), BF16)],
        compiler_params=_cparams(("parallel", "arbitrary")),
        name="in_proj",
    )(x, g, w_main, w_small)


def _tri_masks(L, reverse):
    r = lax.broadcasted_iota(jnp.int32, (L, L), 0)
    c = lax.broadcasted_iota(jnp.int32, (L, L), 1)
    return (c >= r) if reverse else (c <= r)


def _mlstm_kernel(*refs, reverse, final):
    q_ref = refs[0]
    ct_scr, n_scr, m_scr = refs[-3:]

    @pl.when(pl.program_id(1) == 0)
    def _():
        ct_scr[...] = jnp.zeros_like(ct_scr)
        n_scr[...] = jnp.zeros_like(n_scr)
        m_scr[...] = jnp.zeros_like(m_scr)

    _mlstm_step(refs, reverse, final)


def _mlstm_step(refs, reverse, final):
    if final:
        (q_ref, k_ref, v_ref, sm_ref, gb_ref, o_ref, hb_ref, ng_ref,
         out_ref, ct_scr, n_scr, m_scr) = refs
    else:
        q_ref, k_ref, v_ref, sm_ref, gb_ref, out_ref, ct_scr, n_scr, m_scr = refs
    L = CHUNK
    H = MLSTM_HEADS
    d = MLSTM_HD
    nbat = q_ref.shape[0]
    pairs = [(bb, h) for bb in range(nbat) for h in range(H)]
    mask = _tri_masks(L, reverse)
    tri = mask.astype(F32)
    i_off = 8 if reverse else 0
    f_off = i_off + 4
    edge = 0 if reverse else L - 1
    scale = d ** -0.5
    lane = lax.broadcasted_iota(jnp.int32, (1, LANE), 1)

    gates = [sm_ref[bb] + gb_ref[...] for bb in range(nbat)]
    bcum = [_dot_hi(tri, _log_sigmoid(g)) for g in gates]
    gates_t = [g.T for g in gates]
    bcum_t = [b.T for b in bcum]
    m_all = [m_scr[bb] for bb in range(nbat)]

    def cols(h):
        return slice(h * d, (h + 1) * d)

    b_col, i_col, m_prev, m_t, w_intra, w_inter = {}, {}, {}, {}, {}, {}
    for pr in pairs:
        bb, h = pr
        b_col[pr] = bcum[bb][:, f_off + h:f_off + h + 1]
        i_col[pr] = gates[bb][:, i_off + h:i_off + h + 1]
        a_row = gates_t[bb][i_off + h:i_off + h + 1, :] - bcum_t[bb][f_off + h:f_off + h + 1, :]
        m_prev[pr] = m_all[bb][:, h:h + 1]
        dmat = jnp.where(mask, b_col[pr] + a_row, -jnp.inf)
        inter = b_col[pr] + m_prev[pr]
        m_t[pr] = jnp.maximum(inter, jnp.max(dmat, axis=1, keepdims=True))
        w_intra[pr] = jnp.exp(dmat - m_t[pr])
        w_inter[pr] = jnp.exp(inter - m_t[pr])

    ks, s = {}, {}
    for pr in pairs:
        bb, h = pr
        ks[pr] = (k_ref[bb, :, cols(h)].astype(F32) * scale).astype(BF16)
        s[pr] = _dot_nt(q_ref[bb, :, cols(h)], ks[pr]) * w_intra[pr]

    hs = {}
    for pr in pairs:
        bb, h = pr
        qh = q_ref[bb, :, cols(h)]
        ct = ct_scr[bb * H + h]
        n_row = n_scr[bb * H + h]
        num = _dot(s[pr].astype(BF16), v_ref[bb, :, cols(h)]) + w_inter[pr] * _dot(qh, ct.astype(BF16))
        qn = jnp.sum(qh.astype(F32) * n_row, axis=1, keepdims=True)
        den = jnp.sum(s[pr], axis=1, keepdims=True) + w_inter[pr] * qn
        hs[pr] = num / jnp.maximum(jnp.abs(den), jnp.exp(-m_t[pr]))

    for pr in pairs:
        bb, h = pr
        vh = v_ref[bb, :, cols(h)]
        btot = b_col[pr][edge:edge + 1, :]
        dec = btot - b_col[pr] + i_col[pr]
        m_new = jnp.maximum(btot + m_prev[pr], jnp.max(dec, axis=0, keepdims=True))
        ws = jnp.exp(dec - m_new)
        wc = jnp.exp(btot + m_prev[pr] - m_new)
        wsv = (ws * vh.astype(F32)).astype(BF16)
        ct_scr[bb * H + h] = wc * ct_scr[bb * H + h] + _dot_tn(ks[pr], wsv)
        n_scr[bb * H + h] = wc * n_scr[bb * H + h] + jnp.sum(ws * ks[pr].astype(F32), axis=0, keepdims=True)
        m_all[bb] = jnp.where(lane == h, m_new, m_all[bb])
    for bb in range(nbat):
        m_scr[bb] = m_all[bb]

    for pr in pairs:
        bb, h = pr
        if not final:
            out_ref[bb, :, cols(h)] = hs[pr]
        else:
            hsum = hs[pr] + hb_ref[bb, :, cols(h)]
            ms = jnp.mean(hsum * hsum, axis=1, keepdims=True)
            y = hsum * lax.rsqrt(ms + NORM_EPS) * ng_ref[:, cols(h)]
            o = o_ref[bb, :, cols(h)].astype(F32)
            out_ref[bb, :, cols(h)] = (_sigmoid(o) * y).astype(BF16)


SCAN_NBAT = 2


def _mlstm_dir(p3, small3, gate_bias_row, B, N, reverse, hb=None, norm_g=None):
    L = CHUNK
    nc = N // L
    nbat = SCAN_NBAT
    final = hb is not None

    def crow(c):
        return (nc - 1 - c) if reverse else c

    W = BRANCH_W
    in_specs = [
        pl.BlockSpec((nbat, L, W), lambda b, c: (b, crow(c), ML_Q // W)),
        pl.BlockSpec((nbat, L, W), lambda b, c: (b, crow(c), ML_K // W)),
        pl.BlockSpec((nbat, L, W), lambda b, c: (b, crow(c), ML_V // W)),
        pl.BlockSpec((nbat, L, SMALL_COLS), lambda b, c: (b, crow(c), 0)),
        pl.BlockSpec((1, SMALL_COLS), lambda b, c: (0, 0)),
    ]
    args = [p3, p3, p3, small3, gate_bias_row]
    if final:
        in_specs += [
            pl.BlockSpec((nbat, L, W), lambda b, c: (b, crow(c), ML_O // W)),
            pl.BlockSpec((nbat, L, W), lambda b, c: (b, crow(c), 0)),
            pl.BlockSpec((1, W), lambda b, c: (0, 0)),
        ]
        args += [p3, hb, norm_g]
    return pl.pallas_call(
        functools.partial(_mlstm_kernel, reverse=reverse, final=final),
        grid=(B // nbat, nc),
        in_specs=in_specs,
        out_specs=pl.BlockSpec((nbat, L, W), lambda b, c: (b, crow(c), 0)),
        out_shape=jax.ShapeDtypeStruct((B, N, W), BF16 if final else F32),
        scratch_shapes=[
            pltpu.VMEM((nbat * MLSTM_HEADS, MLSTM_HD, MLSTM_HD), F32),
            pltpu.VMEM((nbat * MLSTM_HEADS, 1, MLSTM_HD), F32),
            pltpu.VMEM((nbat, 1, LANE), F32),
        ],
        compiler_params=_cparams(("parallel", "arbitrary")),
        name="mlstm_bwd" if reverse else "mlstm_fwd",
    )(*args)


def _mlstm_branch(p, small, gate_bias_row, norm_g, B, N):
    p3 = p.reshape(B, N, P_PAD)
    small3 = small.reshape(B, N, SMALL_COLS)
    hb = _mlstm_dir(p3, small3, gate_bias_row, B, N, reverse=True)
    y = _mlstm_dir(p3, small3, gate_bias_row, B, N, reverse=False, hb=hb, norm_g=norm_g)
    return y.reshape(B * N, BRANCH_W)


def _gqa_prep_kernel(q_ref, kv_ref, cos_ref, sin_ref, qg_ref, kg_ref, bd_ref, qo_ref, ko_ref, vo_ref):
    tm = q_ref.shape[0]
    bd = bd_ref[...]
    cos = cos_ref[...]
    sin = sin_ref[...]
    lane = lax.broadcasted_iota(jnp.int32, (tm, LANE), 1)
    first = (lane % GQA_HD) < (GQA_HD // 2)
    lo = lane < GQA_HD

    def norm_rope(x, g):
        ss = _dot((x * x).astype(BF16), bd)
        xn = x * lax.rsqrt(ss * (1.0 / GQA_HD) + NORM_EPS) * g
        partner = jnp.where(first, pltpu.roll(xn, LANE - GQA_HD // 2, axis=1),
                            pltpu.roll(xn, GQA_HD // 2, axis=1))
        return xn * cos + partner * sin

    qscale = (GQA_HD ** -0.5) * LOG2E
    for a in range(GQA_HEADS * GQA_HD // LANE):
        x = q_ref[:, a * LANE:(a + 1) * LANE].astype(F32)
        qo_ref[:, a * LANE:(a + 1) * LANE] = (norm_rope(x, qg_ref[...]) * qscale).astype(BF16)
    k = norm_rope(kv_ref[:, 0:LANE].astype(F32), kg_ref[...])
    k_sw = pltpu.roll(k, GQA_HD, axis=1)
    ko_ref[0] = jnp.where(lo, k, k_sw).astype(BF16)
    ko_ref[1] = jnp.where(lo, k_sw, k).astype(BF16)
    v = kv_ref[:, LANE:2 * LANE].astype(F32)
    v_sw = pltpu.roll(v, GQA_HD, axis=1)
    vo_ref[0] = jnp.where(lo, v, 1.0).astype(BF16)
    vo_ref[1] = jnp.where(lo, v_sw, 1.0).astype(BF16)


def _gqa_prep(p, cos_t, sin_t, qg, kg, bd, B, N, tm=512):
    T = B * N
    nb = N // tm
    return pl.pallas_call(
        _gqa_prep_kernel,
        grid=(T // tm,),
        in_specs=[
            pl.BlockSpec((tm, 512), lambda i: (i, GQ_Q // 512)),
            pl.BlockSpec((tm, 256), lambda i: (i, GQ_KV // 256)),
            pl.BlockSpec((tm, LANE), lambda i: (i % nb, 0)),
            pl.BlockSpec((tm, LANE), lambda i: (i % nb, 0)),
            pl.BlockSpec((1, LANE), lambda i: (0, 0)),
            pl.BlockSpec((1, LANE), lambda i: (0, 0)),
            pl.BlockSpec((LANE, LANE), lambda i: (0, 0)),
        ],
        out_specs=[
            pl.BlockSpec((tm, 512), lambda i: (i, 0)),
            pl.BlockSpec((GQA_KV_HEADS, tm, LANE), lambda i: (0, i, 0)),
            pl.BlockSpec((GQA_KV_HEADS, tm, LANE), lambda i: (0, i, 0)),
        ],
        out_shape=[
            jax.ShapeDtypeStruct((T, 512), BF16),
            jax.ShapeDtypeStruct((GQA_KV_HEADS, T, LANE), BF16),
            jax.ShapeDtypeStruct((GQA_KV_HEADS, T, LANE), BF16),
        ],
        compiler_params=_cparams(("parallel",)),
        name="gqa_prep",
    )(p, p, cos_t, sin_t, qg, kg, bd)


FLASH_TKB = 512


def _softmax_keys(s_ref, p_ref, m_scr, al_scr, l_scr, M):
    tkb = s_ref.shape[0]
    for c in range(M // LANE):
        cols = slice(c * LANE, (c + 1) * LANE)
        mx = jnp.max(jnp.max(s_ref[:, cols].reshape(tkb // 8, 8, LANE), axis=0), axis=0, keepdims=True)
        m_prev = m_scr[:, cols]
        m_new = jnp.maximum(m_prev, mx)
        alpha = jnp.exp2(m_prev - m_new)
        p = jnp.exp2(s_ref[:, cols] - m_new)
        p_ref[:, cols] = p.astype(BF16)
        if l_scr is not None:
            lsum = jnp.sum(jnp.sum(p.reshape(tkb // 8, 8, LANE), axis=0), axis=0, keepdims=True)
            l_scr[:, cols] = alpha * l_scr[:, cols] + lsum
        al_scr[:, cols] = alpha
        m_scr[:, cols] = m_new


def _flash_blocks(nb, qk, pv, softmax, fix=None):
    qk(0)
    for i in range(nb):
        if i + 1 < nb:
            qk(i + 1)
        if fix is not None:
            fix(i)
        softmax(i)
        pv(i)


FLASH_SAFE_LOG2 = 120.0


def _flash_one_pass(nb, scores, values, m_scr, l_scr, acc_scr, p_scr):
    for i in range(nb):
        s = scores(i)
        tkb, M = s.shape
        m_used = m_scr[...]
        p = jnp.exp2(s - m_used)
        p_scr[i % 2] = p.astype(BF16)
        mx = jnp.max(jnp.max(s.reshape(tkb // 8, 8, M), axis=0), axis=0, keepdims=True)
        m_new = jnp.maximum(m_used, mx)
        alpha = jnp.exp2(m_used - m_new)
        if l_scr is not None:
            lsum = jnp.sum(jnp.sum(p.reshape(tkb // 8, 8, M), axis=0), axis=0, keepdims=True)
            l_scr[...] = alpha * (l_scr[...] + lsum)
        acc_scr[...] = alpha * (acc_scr[...] + _dot_tn(values(i), p_scr[i % 2]))
        m_scr[...] = m_new


def _flash_score_bound(qs, kmax):
    qf = qs.astype(F32)
    n2 = _dot_nt(jnp.ones((8, LANE), BF16), (qf * qf).astype(BF16))[0:1, :]
    return jnp.sqrt(n2) * kmax * 1.01 + 1.0


def _flash_is_safe(u_scr, m_scr):
    return jnp.where(jnp.max(u_scr[...] - m_scr[...]) <= FLASH_SAFE_LOG2, 1, 0)


def _gqa_flash_kernel(q_ref, k_ref, v_ref, kmax_ref, o_ref, qs_scr, m_scr, al_scr, acc_scr, s_scr, p_scr,
                      u_scr, safe_ref):
    tq = q_ref.shape[0]
    M = 4 * tq
    tkb = s_scr.shape[1]
    nb = k_ref.shape[0] // tkb
    kj = pl.program_id(3)

    @pl.when(kj == 0)
    def _():
        lane = lax.broadcasted_iota(jnp.int32, (tq, LANE), 1)
        lo = lane < GQA_HD
        zero = jnp.zeros((tq, LANE), BF16)
        for a in range(2):
            qa = q_ref[:, a * LANE:(a + 1) * LANE]
            qs_scr[(2 * a) * tq:(2 * a + 1) * tq, :] = jnp.where(lo, qa, zero)
            qs_scr[(2 * a + 1) * tq:(2 * a + 2) * tq, :] = jnp.where(lo, zero, qa)
        m_scr[...] = jnp.full_like(m_scr, -jnp.inf)
        acc_scr[...] = jnp.zeros_like(acc_scr)
        u_scr[...] = _flash_score_bound(qs_scr[...], kmax_ref[:, 0:1])
        safe_ref[0] = 0

    def scores(i):
        return _dot_nt(k_ref[i * tkb:(i + 1) * tkb, :], qs_scr[...])

    def values(i):
        return v_ref[i * tkb:(i + 1) * tkb, :]

    def qk(i):
        s_scr[i % 2] = scores(i)

    def softmax(i):
        _softmax_keys(s_scr.at[i % 2], p_scr.at[i % 2], m_scr, al_scr, None, M)

    def pv(i):
        acc_scr[...] = al_scr[...] * acc_scr[...] + _dot_tn(values(i), p_scr[i % 2])

    one_pass = safe_ref[0] == 1

    @pl.when(jnp.logical_not(one_pass))
    def _():
        _flash_blocks(nb, qk, pv, softmax)

    @pl.when(one_pass)
    def _():
        _flash_one_pass(nb, scores, values, m_scr, None, acc_scr, p_scr)

    @pl.when(kj == 0)
    def _():
        safe_ref[0] = _flash_is_safe(u_scr, m_scr)

    @pl.when(kj == pl.num_programs(3) - 1)
    def _():
        inv = 1.0 / acc_scr[GQA_HD:GQA_HD + 1, :]
        ot = jnp.concatenate([acc_scr[0:GQA_HD, h * tq:(h + 1) * tq] * inv[:, h * tq:(h + 1) * tq]
                              for h in range(4)], axis=0)
        o_ref[...] = ot.T.astype(BF16)


def _gqa_flash(qn, k2, va, B, N, tq=256, tk=2048):
    T = B * N
    tk = min(tk, N)
    nq, nk = N // tq, N // tk
    M = 4 * tq
    tkb = FLASH_TKB
    kf = k2[:, :, 0:GQA_HD].astype(F32).reshape(GQA_KV_HEADS, B, N, GQA_HD)
    kmax = jnp.max(jnp.sqrt(jnp.sum(kf * kf, axis=-1)), axis=-1)
    kmax = jnp.broadcast_to(kmax.T.reshape(B * GQA_KV_HEADS, 1, 1), (B * GQA_KV_HEADS, 1, LANE))
    return pl.pallas_call(
        _gqa_flash_kernel,
        grid=(B, GQA_KV_HEADS, nq, nk),
        in_specs=[
            pl.BlockSpec((tq, 256), lambda b, g, i, j: (b * nq + i, g)),
            pl.BlockSpec((None, tk, LANE), lambda b, g, i, j: (g, b * nk + j, 0)),
            pl.BlockSpec((None, tk, LANE), lambda b, g, i, j: (g, b * nk + j, 0)),
            pl.BlockSpec((None, 1, LANE), lambda b, g, i, j: (b * GQA_KV_HEADS + g, 0, 0)),
        ],
        out_specs=pl.BlockSpec((tq, 256), lambda b, g, i, j: (b * nq + i, g)),
        out_shape=jax.ShapeDtypeStruct((T, 512), BF16),
        scratch_shapes=[
            pltpu.VMEM((M, LANE), BF16),
            pltpu.VMEM((1, M), F32),
            pltpu.VMEM((1, M), F32),
            pltpu.VMEM((LANE, M), F32),
            pltpu.VMEM((2, tkb, M), F32),
            pltpu.VMEM((2, tkb, M), BF16),
            pltpu.VMEM((1, M), F32),
            pltpu.SMEM((1,), jnp.int32),
        ],
        compiler_params=_cparams(("parallel", "parallel", "parallel", "arbitrary")),
        name="gqa_flash",
    )(qn, k2, va, kmax)


def _rope_tables(N):
    rows = N // GRID_W
    row = jnp.repeat(jnp.arange(rows, dtype=F32), GRID_W)
    col = (jnp.arange(N) % GRID_W).astype(F32)
    half = GQA_HD // 2
    inv = 1.0 / (ROPE_THETA ** (jnp.arange(0, half, 2, dtype=F32) / half))
    ang = jnp.concatenate([row[:, None] * inv, col[:, None] * inv], axis=-1)
    cos, sin = jnp.cos(ang), jnp.sin(ang)
    cos_h = jnp.concatenate([cos, cos], axis=-1)
    sin_h = jnp.concatenate([-sin, sin], axis=-1)
    return jnp.tile(cos_h, (1, LANE // GQA_HD)), jnp.tile(sin_h, (1, LANE // GQA_HD))


def _gqa_branch(p, cos_t, sin_t, qg, kg, bd, B, N):
    qn, k2, va = _gqa_prep(p, cos_t, sin_t, qg, kg, bd, B, N)
    return _gqa_flash(qn, k2, va, B, N)


DIFF_SKIP_LOG2 = 40.0


def _alibi_slope_log2(h):
    return (2.0 ** (-8.0 * (h + 1) / DIFF_HEADS)) * LOG2E


def _alibi_tables(N):
    pos = np.arange(N)
    hi = ((pos // 128) * 128).astype(np.float32)
    lo = (pos % 128).astype(np.float32)
    kt = np.zeros((DIFF_HEADS, N, LANE), np.float32)
    qt = np.zeros((DIFF_HEADS, N, LANE), np.float32)
    for h in range(DIFF_HEADS):
        s = np.float32(_alibi_slope_log2(h))
        s1 = np.float32(s).astype(BF16).astype(np.float32)
        s2 = np.float32(s - s1).astype(BF16).astype(np.float32)
        for c, val in enumerate((hi, hi, lo, lo)):
            kt[h, :, c] = val
            qt[h, :, 4 + c] = -val
        for c, val in enumerate((s1, s2, s1, s2)):
            kt[h, :, 4 + c] = val
            qt[h, :, c] = val
    return (jnp.asarray(kt.reshape(DIFF_HEADS * N, LANE), BF16), jnp.asarray(qt.reshape(DIFF_HEADS * N, LANE), BF16))


def _diff_flash_kernel(lo_ref, hi_ref, q_ref, qa_ref, k_ref, ka_ref, v_ref, kmax_ref, lp_ref, ng_ref, o_ref,
                       qs_scr, m_scr, l_scr, al_scr, acc_scr, s_scr, p_scr, u_scr, safe_ref, *, lambda_init):
    tq = q_ref.shape[0]
    tk = k_ref.shape[0]
    M = 2 * tq
    tkb = s_scr.shape[1]
    nb = tk // tkb
    b, h, qi, kj = pl.program_id(0), pl.program_id(1), pl.program_id(2), pl.program_id(3)
    idx = (b * pl.num_programs(1) + h) * pl.num_programs(2) + qi
    first, last = lo_ref[idx], hi_ref[idx]

    @pl.when(kj == first)
    def _():
        lane = lax.broadcasted_iota(jnp.int32, (tq, LANE), 1)
        lo = lane < DIFF_HD
        zero = jnp.zeros((tq, LANE), BF16)
        q = q_ref[...]
        qa = qa_ref[...]
        for var, aug in enumerate((qa, -qa)):
            qs_scr[var, 0:tq, 0:LANE] = jnp.where(lo, q, zero)
            qs_scr[var, tq:M, 0:LANE] = jnp.where(lo, zero, q)
            qs_scr[var, 0:tq, LANE:2 * LANE] = aug
            qs_scr[var, tq:M, LANE:2 * LANE] = aug
        m_scr[...] = jnp.full_like(m_scr, -jnp.inf)
        l_scr[...] = jnp.zeros_like(l_scr)
        acc_scr[...] = jnp.zeros_like(acc_scr)
        u_scr[...] = _flash_score_bound(qs_scr[0, :, 0:LANE], kmax_ref[:, 0:1])
        safe_ref[0] = 0

    is_left = (kj + 1) * tk <= qi * tq
    is_right = kj * tk >= (qi + 1) * tq
    pure = jnp.logical_or(is_left, is_right)
    active = jnp.logical_and(kj >= first, kj <= last)
    sel = jnp.where(is_right, 1, 0)

    def scores(i):
        rows = slice(i * tkb, (i + 1) * tkb)
        kk = jnp.concatenate([k_ref[rows, :], ka_ref[rows, :]], axis=1)
        return _dot_nt(kk, qs_scr[sel])

    def values(i):
        return v_ref[i * tkb:(i + 1) * tkb, :]

    def qk(i):
        s_scr[i % 2] = scores(i)

    def fix(i):
        slope = jnp.exp2(-2.0 * (jnp.full((1, 1), h, jnp.int32).astype(F32) + 1.0)) * LOG2E
        j = kj * tk + i * tkb + lax.broadcasted_iota(jnp.int32, (tkb, 1), 0)
        col = lax.broadcasted_iota(jnp.int32, (1, M), 1)
        iq = qi * tq + col - jnp.where(col >= tq, tq, 0)
        d = jnp.maximum(j - iq, 0).astype(F32)
        s_scr[i % 2] = s_scr[i % 2] - (2.0 * slope) * d

    def softmax(i):
        _softmax_keys(s_scr.at[i % 2], p_scr.at[i % 2], m_scr, al_scr, l_scr, M)

    def pv(i):
        acc_scr[...] = al_scr[...] * acc_scr[...] + _dot_tn(values(i), p_scr[i % 2])

    @pl.when(jnp.logical_and(active, pure))
    def _():
        one_pass = jnp.logical_and(safe_ref[0] == 1, kj != first)

        @pl.when(jnp.logical_not(one_pass))
        def _():
            _flash_blocks(nb, qk, pv, softmax)

        @pl.when(one_pass)
        def _():
            _flash_one_pass(nb, scores, values, m_scr, l_scr, acc_scr, p_scr)

    @pl.when(jnp.logical_and(active, jnp.logical_not(pure)))
    def _():
        _flash_blocks(nb, qk, pv, softmax, fix=fix)

    @pl.when(kj == first)
    def _():
        safe_ref[0] = _flash_is_safe(u_scr, m_scr)

    @pl.when(kj == last)
    def _():
        lp = lp_ref[...]
        s01 = jnp.sum(jnp.sum(lp[0:1] * lp[1:2], axis=1, keepdims=True), axis=0, keepdims=True)
        s23 = jnp.sum(jnp.sum(lp[2:3] * lp[3:4], axis=1, keepdims=True), axis=0, keepdims=True)
        lam = jnp.exp(s01) - jnp.exp(s23) + lambda_init
        inv = 1.0 / l_scr[...]
        ot = acc_scr[:, 0:tq] * inv[:, 0:tq] - lam * (acc_scr[:, tq:M] * inv[:, tq:M])
        o = ot.T
        ms = jnp.mean(o * o, axis=1, keepdims=True)
        o_ref[...] = (o * lax.rsqrt(ms + NORM_EPS) * ng_ref[...] * (1.0 - lambda_init)).astype(BF16)


def _diff_prep_kernel(q_ref, k_ref, v_ref, bd_ref, ko_ref, vo_ref, nrm_ref):
    tm = q_ref.shape[0]
    lane = lax.broadcasted_iota(jnp.int32, (tm, LANE), 1)
    nrm = jnp.zeros((tm, LANE), F32)
    for h in range(DIFF_HEADS):
        cols = slice(h * LANE, (h + 1) * LANE)
        ko_ref[h] = k_ref[:, cols]
        vo_ref[h] = v_ref[:, cols]
        for src, off in ((q_ref, 0), (k_ref, DIFF_HEADS)):
            x = src[:, cols].astype(F32)
            ss = _dot_hi(x * x, bd_ref[...])
            n = jnp.sqrt(jnp.maximum(ss, pltpu.roll(ss, DIFF_HD, axis=1)))
            nrm = jnp.where(lane == off + h, n, nrm)
    nrm_ref[...] = nrm


def _diff_prep(p, bd32, B, N, tm=512):
    T = B * N
    W = BRANCH_W
    return pl.pallas_call(
        _diff_prep_kernel,
        grid=(T // tm,),
        in_specs=[
            pl.BlockSpec((tm, W), lambda i: (i, DF_Q // W)),
            pl.BlockSpec((tm, W), lambda i: (i, DF_K // W)),
            pl.BlockSpec((tm, W), lambda i: (i, DF_V // W)),
            pl.BlockSpec((LANE, LANE), lambda i: (0, 0)),
        ],
        out_specs=[
            pl.BlockSpec((DIFF_HEADS, tm, LANE), lambda i: (0, i, 0)),
            pl.BlockSpec((DIFF_HEADS, tm, LANE), lambda i: (0, i, 0)),
            pl.BlockSpec((tm, LANE), lambda i: (i, 0)),
        ],
        out_shape=[
            jax.ShapeDtypeStruct((DIFF_HEADS, T, LANE), BF16),
            jax.ShapeDtypeStruct((DIFF_HEADS, T, LANE), BF16),
            jax.ShapeDtypeStruct((T, LANE), F32),
        ],
        compiler_params=_cparams(("parallel",)),
        name="diff_prep",
    )(p, p, p, bd32)


def _diff_bands(nrm, B, N, tq, tk):
    nq, nkt = N // tq, N // tk
    qn = nrm[:, 0:DIFF_HEADS].reshape(B, N, DIFF_HEADS)
    kn = nrm[:, DIFF_HEADS:2 * DIFF_HEADS].reshape(B, N, DIFF_HEADS)
    qmax = jnp.max(qn.reshape(B, nq, tq, DIFF_HEADS), axis=2)
    kself = jnp.max(kn.reshape(B, nq, tq, DIFF_HEADS), axis=2)
    kmax = jnp.max(kn.reshape(B, nkt, tk, DIFF_HEADS), axis=2)
    bound = qmax[:, :, None, :] * (kmax[:, None, :, :] + kself[:, :, None, :]) * 1.001 + DIFF_SKIP_LOG2
    q0 = np.arange(nq)[:, None] * tq
    k0 = np.arange(nkt)[None, :] * tk
    dist = np.maximum(0, np.maximum(q0 - (k0 + tk - 1), k0 - (q0 + tq - 1))).astype(np.float32)
    slopes = np.array([_alibi_slope_log2(h) for h in range(DIFF_HEADS)], np.float32)
    keep = bound > jnp.asarray(dist)[None, :, :, None] * jnp.asarray(slopes)[None, None, None, :]
    keep = jnp.transpose(keep, (0, 3, 1, 2))
    first = jnp.argmax(keep, axis=-1).astype(jnp.int32)
    last = (nkt - 1 - jnp.argmax(keep[..., ::-1], axis=-1)).astype(jnp.int32)
    return first.reshape(-1), last.reshape(-1)


def _diff_branch(p, ktab, qtab, bd32, lam_params, norm_g, lambda_init, B, N, tq=512, tk=2048):
    T = B * N
    nq, nk = N // tq, N // tk
    M = 2 * tq
    tkb = FLASH_TKB
    kd, vd, nrm = _diff_prep(p, bd32, B, N)
    first, last = _diff_bands(nrm, B, N, tq, tk)
    kmax = jnp.max(nrm[:, DIFF_HEADS:2 * DIFF_HEADS].reshape(B, N, DIFF_HEADS), axis=1)
    kmax = jnp.broadcast_to(kmax.reshape(B * DIFF_HEADS, 1, 1), (B * DIFF_HEADS, 1, LANE))

    def kstep(b, h, i, j, lo, hi):
        idx = (b * DIFF_HEADS + h) * nq + i
        return jnp.minimum(jnp.maximum(j, lo[idx]), hi[idx])

    return pl.pallas_call(
        functools.partial(_diff_flash_kernel, lambda_init=lambda_init),
        grid_spec=pltpu.PrefetchScalarGridSpec(
            num_scalar_prefetch=2,
            grid=(B, DIFF_HEADS, nq, nk),
            in_specs=[
                pl.BlockSpec((tq, LANE), lambda b, h, i, j, lo, hi: (b * nq + i, DF_Q // LANE + h)),
                pl.BlockSpec((tq, LANE), lambda b, h, i, j, lo, hi: (h * nq + i, 0)),
                pl.BlockSpec((None, tk, LANE), lambda b, h, i, j, lo, hi: (h, b * nk + kstep(b, h, i, j, lo, hi), 0)),
                pl.BlockSpec((tk, LANE), lambda b, h, i, j, lo, hi: (h * nk + kstep(b, h, i, j, lo, hi), 0)),
                pl.BlockSpec((None, tk, LANE), lambda b, h, i, j, lo, hi: (h, b * nk + kstep(b, h, i, j, lo, hi), 0)),
                pl.BlockSpec((None, 1, LANE), lambda b, h, i, j, lo, hi: (b * DIFF_HEADS + h, 0, 0)),
                pl.BlockSpec((4, DIFF_HD), lambda b, h, i, j, lo, hi: (0, 0)),
                pl.BlockSpec((1, LANE), lambda b, h, i, j, lo, hi: (0, h)),
            ],
            out_specs=pl.BlockSpec((tq, LANE), lambda b, h, i, j, lo, hi: (b * nq + i, h)),
            scratch_shapes=[
                pltpu.VMEM((2, M, 2 * LANE), BF16),
                pltpu.VMEM((1, M), F32),
                pltpu.VMEM((1, M), F32),
                pltpu.VMEM((1, M), F32),
                pltpu.VMEM((LANE, M), F32),
                pltpu.VMEM((2, tkb, M), F32),
                pltpu.VMEM((2, tkb, M), BF16),
                pltpu.VMEM((1, M), F32),
                pltpu.SMEM((1,), jnp.int32),
            ],
        ),
        out_shape=jax.ShapeDtypeStruct((T, BRANCH_W), BF16),
        compiler_params=_cparams(("parallel", "parallel", "parallel", "arbitrary")),
        name="diff_flash",
    )(first, last, p, qtab, kd, ktab, vd, kmax, lam_params, norm_g)


SSD_HALO = 16


def _ssd_prep_kernel(x_ref, xp_ref, xn_ref, bc_ref, bcp_ref, bcn_ref, sm_ref, cw_ref, cb_ref,
                     dtb_ref, arow_ref, xo_ref, bco_ref, dto_ref):
    tc = x_ref.shape[0]
    c = pl.program_id(1)
    nc = pl.num_programs(1)
    has_prev = jnp.where(c > 0, 1.0, 0.0)
    has_next = jnp.where(c < nc - 1, 1.0, 0.0)
    pad = SSD_CONV // 2

    def conv(main_ref, prev_ref, next_ref, off):
        xf = jnp.concatenate([prev_ref[...].astype(F32) * has_prev, main_ref[...].astype(F32),
                              next_ref[...].astype(F32) * has_next], axis=0)
        n = xf.shape[0]
        acc = jnp.zeros((tc, xf.shape[1]), F32) + cb_ref[:, off:off + xf.shape[1]]
        for j in range(SSD_CONV):
            sh = pltpu.roll(xf, (pad - j) % n, axis=0)[SSD_HALO:SSD_HALO + tc, :]
            acc = acc + sh * cw_ref[j:j + 1, off:off + xf.shape[1]]
        return _silu(acc)

    xo_ref[...] = conv(x_ref, xp_ref, xn_ref, 0).astype(BF16)
    bco_ref[...] = conv(bc_ref, bcp_ref, bcn_ref, BRANCH_W).astype(BF16)
    dt = _softplus(sm_ref[...] + dtb_ref[...])
    a = pltpu.roll(dt * arow_ref[...], LANE - 16, axis=1)
    lane = lax.broadcasted_iota(jnp.int32, (tc, LANE), 1)
    dto_ref[...] = jnp.where(lane < 16, a, jnp.where(lane < 32, dt, 0.0))


def _ssd_prep(p, small, conv_w, conv_b, dtb_row, a_row, B, N, tc=512):
    T = B * N
    nc = N // tc
    hb = tc // SSD_HALO
    W = BRANCH_W

    def main(col):
        return pl.BlockSpec((tc, W), lambda b, c: (b * nc + c, col))

    def prev(col):
        return pl.BlockSpec((SSD_HALO, W), lambda b, c: (jnp.maximum((b * nc + c) * hb - 1, 0), col))

    def nxt(col):
        return pl.BlockSpec((SSD_HALO, W), lambda b, c: (jnp.minimum((b * nc + c + 1) * hb, T // SSD_HALO - 1), col))

    cx, cbc = SS_X // W, SS_BC // W
    return pl.pallas_call(
        _ssd_prep_kernel,
        grid=(B, nc),
        in_specs=[
            main(cx), prev(cx), nxt(cx), main(cbc), prev(cbc), nxt(cbc),
            pl.BlockSpec((tc, SMALL_COLS), lambda b, c: (b * nc + c, 0)),
            pl.BlockSpec((SSD_CONV, 2 * W), lambda b, c: (0, 0)),
            pl.BlockSpec((1, 2 * W), lambda b, c: (0, 0)),
            pl.BlockSpec((1, LANE), lambda b, c: (0, 0)),
            pl.BlockSpec((1, LANE), lambda b, c: (0, 0)),
        ],
        out_specs=[
            pl.BlockSpec((tc, W), lambda b, c: (b * nc + c, 0)),
            pl.BlockSpec((tc, W), lambda b, c: (b * nc + c, 0)),
            pl.BlockSpec((tc, LANE), lambda b, c: (b * nc + c, 0)),
        ],
        out_shape=[
            jax.ShapeDtypeStruct((T, W), BF16),
            jax.ShapeDtypeStruct((T, W), BF16),
            jax.ShapeDtypeStruct((T, LANE), F32),
        ],
        compiler_params=_cparams(("parallel", "parallel")),
        name="ssd_prep",
    )(p, p, p, p, p, p, small, conv_w, conv_b, dtb_row, a_row)


def _ssd_kernel(*refs, reverse, final):
    x_ref = refs[0]
    hs_scr = refs[-1]

    @pl.when(pl.program_id(1) == 0)
    def _():
        hs_scr[...] = jnp.zeros_like(hs_scr)

    for bb in range(x_ref.shape[0]):
        _ssd_chunk(bb, refs, reverse, final)


def _ssd_chunk(bb, refs, reverse, final):
    if final:
        (x_ref, bc_ref, dta_ref, ea_ref, ed_ref, z_ref, yb_ref, dsk_ref, ng_ref,
         out_ref, hs_scr) = refs
    else:
        x_ref, bc_ref, dta_ref, ea_ref, ed_ref, out_ref, hs_scr = refs
    L = CHUNK
    S = SSD_STATE
    W = BRANCH_W
    GW = W // SSD_GROUPS
    mask = _tri_masks(L, reverse)
    tri = mask.astype(F32)
    dta = dta_ref[bb]
    acs = _dot_hi(tri, dta)
    acs_t = acs.T
    acs_e = _dot_hi(acs, ea_ref[...])
    dt_e = _dot_hi(dta, ed_ref[...])
    edge = 0 if reverse else L - 1
    a_off = 8 if reverse else 0
    acs_end = acs_e[edge:edge + 1, :]
    xf = x_ref[bb].astype(F32)
    xdt = xf * dt_e
    xdt_b = xdt.astype(BF16)
    xdend = (xdt * jnp.exp(acs_end - acs_e)).astype(BF16)
    e_acs = jnp.exp(acs_e)
    cdec = jnp.exp(acs_end)
    lane = lax.broadcasted_iota(jnp.int32, (L, LANE), 1)
    lo = lane < SSD_HD
    zero = jnp.zeros((L, LANE), BF16)

    ys = []
    for g in range(SSD_GROUPS):
        bg = bc_ref[bb, :, g * S:(g + 1) * S]
        cg = bc_ref[bb, :, SSD_GROUPS * S + g * S:SSD_GROUPS * S + (g + 1) * S]
        cb = _dot_nt(cg, bg)
        hs = hs_scr[bb * SSD_GROUPS + g]
        y_off = _dot(cg, hs.astype(BF16)) * e_acs[:, g * GW:(g + 1) * GW]
        hs_scr[bb * SSD_GROUPS + g] = (cdec[:, g * GW:(g + 1) * GW] * hs
                                       + _dot_tn(bg, xdend[:, g * GW:(g + 1) * GW]))
        for pr in range(GW // LANE):
            col = g * GW + pr * LANE
            xp = xdt_b[:, col:col + LANE]
            yd = jnp.zeros((L, LANE), F32)
            for hh in range(2):
                hd = (col // SSD_HD) + hh
                a_col = acs[:, a_off + hd:a_off + hd + 1]
                a_row = acs_t[a_off + hd:a_off + hd + 1, :]
                decay = jnp.exp(jnp.where(mask, a_col - a_row, -jnp.inf))
                wm = (cb * decay).astype(BF16)
                xm = jnp.where(lo, xp, zero) if hh == 0 else jnp.where(lo, zero, xp)
                yd = yd + _dot(wm, xm)
            ys.append(yd + y_off[:, pr * LANE:(pr + 1) * LANE])
    y = jnp.concatenate(ys, axis=1)

    if not final:
        out_ref[bb] = y
    else:
        y = y + yb_ref[bb] + dsk_ref[...] * xf
        y = y * _silu(z_ref[bb].astype(F32))
        outs = []
        for g in range(SSD_GROUPS):
            yg = y[:, g * GW:(g + 1) * GW]
            ms = jnp.mean(yg * yg, axis=1, keepdims=True)
            outs.append(yg * lax.rsqrt(ms + NORM_EPS))
        out_ref[bb] = (jnp.concatenate(outs, axis=1) * ng_ref[...]).astype(BF16)


def _ssd_dir(xs, bc, dta, ea, ed, B, N, reverse, p3=None, yb=None, dskip_row=None, norm_g=None):
    L = CHUNK
    nc = N // L
    nbat = SCAN_NBAT
    W = BRANCH_W
    final = yb is not None

    def crow(c):
        return (nc - 1 - c) if reverse else c

    in_specs = [
        pl.BlockSpec((nbat, L, W), lambda b, c: (b, crow(c), 0)),
        pl.BlockSpec((nbat, L, W), lambda b, c: (b, crow(c), 0)),
        pl.BlockSpec((nbat, L, LANE), lambda b, c: (b, crow(c), 0)),
        pl.BlockSpec((LANE, W), lambda b, c: (0, 0)),
        pl.BlockSpec((LANE, W), lambda b, c: (0, 0)),
    ]
    args = [xs, bc, dta, ea, ed]
    if final:
        in_specs += [
            pl.BlockSpec((nbat, L, W), lambda b, c: (b, crow(c), SS_Z // W)),
            pl.BlockSpec((nbat, L, W), lambda b, c: (b, crow(c), 0)),
            pl.BlockSpec((1, W), lambda b, c: (0, 0)),
            pl.BlockSpec((1, W), lambda b, c: (0, 0)),
        ]
        args += [p3, yb, dskip_row, norm_g]
    return pl.pallas_call(
        functools.partial(_ssd_kernel, reverse=reverse, final=final),
        grid=(B // nbat, nc),
        in_specs=in_specs,
        out_specs=pl.BlockSpec((nbat, L, W), lambda b, c: (b, crow(c), 0)),
        out_shape=jax.ShapeDtypeStruct((B, N, W), BF16 if final else F32),
        scratch_shapes=[pltpu.VMEM((nbat * SSD_GROUPS, SSD_STATE, W // SSD_GROUPS), F32)],
        compiler_params=_cparams(("parallel", "arbitrary")),
        name="ssd_bwd" if reverse else "ssd_fwd",
    )(*args)


def _ssd_expand_mats():
    ea_f = np.zeros((LANE, BRANCH_W), np.float32)
    ea_b = np.zeros((LANE, BRANCH_W), np.float32)
    ed_f = np.zeros((LANE, BRANCH_W), np.float32)
    ed_b = np.zeros((LANE, BRANCH_W), np.float32)
    for h in range(SSD_HEADS):
        ea_f[h, h * SSD_HD:(h + 1) * SSD_HD] = 1.0
        ea_b[8 + h, h * SSD_HD:(h + 1) * SSD_HD] = 1.0
        ed_f[16 + h, h * SSD_HD:(h + 1) * SSD_HD] = 1.0
        ed_b[24 + h, h * SSD_HD:(h + 1) * SSD_HD] = 1.0
    return ea_f, ea_b, ed_f, ed_b


_EA_F, _EA_B, _ED_F, _ED_B = _ssd_expand_mats()


def _ssd_branch(p, small, conv_w, conv_b, dtb_row, a_row, dskip_row, norm_g, B, N):
    xs, bc, dta = _ssd_prep(p, small, conv_w, conv_b, dtb_row, a_row, B, N)
    xs, bc, dta = (a.reshape(B, N, a.shape[-1]) for a in (xs, bc, dta))
    yb = _ssd_dir(xs, bc, dta, jnp.asarray(_EA_B), jnp.asarray(_ED_B), B, N, reverse=True)
    y = _ssd_dir(xs, bc, dta, jnp.asarray(_EA_F), jnp.asarray(_ED_F), B, N, reverse=False,
                 p3=p.reshape(B, N, P_PAD), yb=yb, dskip_row=dskip_row, norm_g=norm_g)
    return y.reshape(B * N, BRANCH_W)


def _merge_kernel(h_ref, y0_ref, y1_ref, y2_ref, y3_ref, wg_ref, wb_ref, o_ref):
    h = h_ref[...]
    acc = None
    for i, y_ref in enumerate((y0_ref, y1_ref, y2_ref, y3_ref)):
        gate = _sigmoid(_dot(h, wg_ref[i]))
        term = gate * _dot(y_ref[...], wb_ref[i])
        acc = term if acc is None else acc + term
    o_ref[...] = acc.astype(BF16)


def _merge(h, ys, wg, wb, tm=1024, tn=256):
    T = h.shape[0]
    return pl.pallas_call(
        _merge_kernel,
        grid=(T // tm, D_MODEL // tn),
        in_specs=[pl.BlockSpec((tm, D_MODEL), lambda i, j: (i, 0))]
        + [pl.BlockSpec((tm, BRANCH_W), lambda i, j: (i, 0))] * 4
        + [
            pl.BlockSpec((4, D_MODEL, tn), lambda i, j: (0, 0, j)),
            pl.BlockSpec((4, BRANCH_W, tn), lambda i, j: (0, 0, j)),
        ],
        out_specs=pl.BlockSpec((tm, tn), lambda i, j: (i, j)),
        out_shape=jax.ShapeDtypeStruct((T, D_MODEL), BF16),
        compiler_params=_cparams(("parallel", "arbitrary")),
        name="merge",
    )(h, *ys, wg, wb)


def _outproj_kernel(x_ref, m_ref, wo_ref, g_ref, wr_ref, xo_ref, h2_ref, r_ref):
    x = x_ref[...] + _dot(m_ref[...], wo_ref[...])
    xo_ref[...] = x
    ms = jnp.mean(x * x, axis=-1, keepdims=True)
    h2 = x * lax.rsqrt(ms + NORM_EPS) * g_ref[...]
    h2_ref[...] = h2.astype(BF16)
    logits = _dot_hi(h2, wr_ref[...])
    tm = x.shape[0]
    lane = lax.broadcasted_iota(jnp.int32, (tm, LANE), 1).astype(F32)
    neg = -jnp.inf
    big = float(LANE)
    gl = jnp.where(lane < MOE_GROUPS, logits, neg)
    gmax = jnp.max(gl, axis=1, keepdims=True)
    g_idx = jnp.min(jnp.where(gl == gmax, lane, big), axis=1, keepdims=True)
    g_w = 1.0 / jnp.sum(jnp.exp(gl - gmax), axis=1, keepdims=True)
    e_lo = MOE_GROUPS + g_idx * MOE_EPG
    el = jnp.where(lane >= e_lo, jnp.where(lane < e_lo + MOE_EPG, logits, neg), neg)
    m1 = jnp.max(el, axis=1, keepdims=True)
    i1 = jnp.min(jnp.where(el == m1, lane, big), axis=1, keepdims=True)
    el2 = jnp.where(lane == i1, neg, el)
    m2 = jnp.max(el2, axis=1, keepdims=True)
    i2 = jnp.min(jnp.where(el2 == m2, lane, big), axis=1, keepdims=True)
    den = jnp.sum(jnp.exp(el - m1), axis=1, keepdims=True)
    p1 = 1.0 / den
    p2 = jnp.exp(m2 - m1) / den
    w1 = p1 / (p1 + p2) * g_w
    w2 = p2 / (p1 + p2) * g_w
    e1 = i1 - MOE_GROUPS
    e2 = i2 - MOE_GROUPS
    r_ref[...] = jnp.where(lane == 0, e1, jnp.where(lane == 1, e2, jnp.where(lane == 2, w1, jnp.where(lane == 3, w2, 0.0))))


def _out_proj(x, merged, wo, g, wr, tm=512):
    T = x.shape[0]
    return pl.pallas_call(
        _outproj_kernel,
        grid=(T // tm,),
        in_specs=[
            pl.BlockSpec((tm, D_MODEL), lambda i: (i, 0)),
            pl.BlockSpec((tm, D_MODEL), lambda i: (i, 0)),
            pl.BlockSpec((D_MODEL, D_MODEL), lambda i: (0, 0)),
            pl.BlockSpec((1, D_MODEL), lambda i: (0, 0)),
            pl.BlockSpec((D_MODEL, LANE), lambda i: (0, 0)),
        ],
        out_specs=[
            pl.BlockSpec((tm, D_MODEL), lambda i: (i, 0)),
            pl.BlockSpec((tm, D_MODEL), lambda i: (i, 0)),
            pl.BlockSpec((tm, LANE), lambda i: (i, 0)),
        ],
        out_shape=[
            jax.ShapeDtypeStruct((T, D_MODEL), F32),
            jax.ShapeDtypeStruct((T, D_MODEL), BF16),
            jax.ShapeDtypeStruct((T, LANE), F32),
        ],
        compiler_params=_cparams(("parallel",)),
        name="out_proj_router",
    )(x, merged, wo, g, wr)


def _moe_ffn_kernel(be_ref, nu_ref, xs_ref, w1_ref, w3_ref, w2_ref, ys_ref):
    used = pl.program_id(0) < nu_ref[0]

    @pl.when(used)
    def _():
        xb = xs_ref[...]
        a = _dot(xb, w1_ref[0].astype(BF16))
        u = _dot(xb, w3_ref[0].astype(BF16))
        hmid = (_silu(a) * u).astype(BF16)
        ys_ref[...] = _dot(hmid, w2_ref[0].astype(BF16)).astype(BF16)

    @pl.when(jnp.logical_not(used))
    def _():
        ys_ref[...] = jnp.zeros_like(ys_ref)


def _moe_ffn(xs, blk_e, nused, w1, w3, w2):
    R = xs.shape[0]
    nblk = R // MOE_BLK
    return pl.pallas_call(
        _moe_ffn_kernel,
        grid_spec=pltpu.PrefetchScalarGridSpec(
            num_scalar_prefetch=2,
            grid=(nblk,),
            in_specs=[
                pl.BlockSpec((MOE_BLK, D_MODEL), lambda b, be, nu: (b, 0)),
                pl.BlockSpec((1, D_MODEL, MOE_FF), lambda b, be, nu: (be[b], 0, 0)),
                pl.BlockSpec((1, D_MODEL, MOE_FF), lambda b, be, nu: (be[b], 0, 0)),
                pl.BlockSpec((1, MOE_FF, D_MODEL), lambda b, be, nu: (be[b], 0, 0)),
            ],
            out_specs=pl.BlockSpec((MOE_BLK, D_MODEL), lambda b, be, nu: (b, 0)),
        ),
        out_shape=jax.ShapeDtypeStruct((R, D_MODEL), BF16),
        compiler_params=_cparams(("arbitrary",)),
        name="moe_ffn",
    )(blk_e, nused, xs, w1, w3, w2)


def _combine_kernel(x_ref, a_ref, b_ref, r_ref, g_ref, o_ref, *, final):
    r = r_ref[...]
    x = x_ref[...] + (r[:, 2:3] * a_ref[...].astype(F32) + r[:, 3:4] * b_ref[...].astype(F32))
    if final:
        ms = jnp.mean(x * x, axis=-1, keepdims=True)
        x = x * lax.rsqrt(ms + NORM_EPS) * g_ref[...]
    o_ref[...] = x


def _combine(x, a, b, route, g, final, tm=512):
    T = x.shape[0]
    return pl.pallas_call(
        functools.partial(_combine_kernel, final=final),
        grid=(T // tm,),
        in_specs=[pl.BlockSpec((tm, D_MODEL), lambda i: (i, 0))] * 3
        + [pl.BlockSpec((tm, LANE), lambda i: (i, 0)), pl.BlockSpec((1, D_MODEL), lambda i: (0, 0))],
        out_specs=pl.BlockSpec((tm, D_MODEL), lambda i: (i, 0)),
        out_shape=jax.ShapeDtypeStruct((T, D_MODEL), F32),
        compiler_params=_cparams(("parallel",)),
        name="moe_combine",
    )(x, a, b, route, g)


def _moe(x, h2, route, w1, w3, w2, g_final, final):
    T = x.shape[0]
    K = 2
    e_idx = route[:, 0:K].astype(jnp.int32)
    flat_e = e_idx.reshape(-1)
    onehot = (flat_e[:, None] == jnp.arange(MOE_EXPERTS)[None, :]).astype(jnp.int32)
    csum = jnp.cumsum(onehot, axis=0)
    rank = jnp.take_along_axis(csum, flat_e[:, None], axis=1)[:, 0] - 1
    counts = csum[-1]
    pcounts = (counts + MOE_BLK - 1) // MOE_BLK * MOE_BLK
    pends = jnp.cumsum(pcounts)
    pstarts = pends - pcounts
    dest = pstarts[flat_e] + rank
    R = T * K + MOE_EXPERTS * MOE_BLK
    nblk = R // MOE_BLK
    row_src = jnp.zeros((R,), jnp.int32).at[dest].set(jnp.arange(T * K, dtype=jnp.int32) // K)
    nused = (pends[-1] // MOE_BLK).astype(jnp.int32)
    blk_start = jnp.arange(nblk, dtype=jnp.int32) * MOE_BLK
    blk_e = jnp.minimum(jnp.sum((pends[None, :] <= blk_start[:, None]).astype(jnp.int32), axis=1), MOE_EXPERTS - 1)
    last_e = blk_e[jnp.maximum(nused - 1, 0)]
    blk_e = jnp.where(jnp.arange(nblk) < nused, blk_e, last_e)
    xs = jnp.take(h2, row_src, axis=0)
    ys = _moe_ffn(xs, blk_e, nused.reshape(1), w1, w3, w2)
    pos = dest.reshape(T, K)
    ga = jnp.take(ys, pos[:, 0], axis=0)
    gb = jnp.take(ys, pos[:, 1], axis=0)
    return _combine(x, ga, gb, route, g_final, final)


def _prep_layer(l, w_in, mlstm_gate_bias, ssd_conv_w, ssd_conv_b, ssd_dt_bias, ssd_a_log, ssd_d,
                gqa_q_norm, gqa_k_norm, w_router_group, w_router_expert):
    w = w_in[l]
    wm = jnp.take(w, jnp.asarray(_MAIN_IDX), axis=1)
    dscale = jnp.ones((P_COLS,), F32).at[DF_Q:DF_Q + BRANCH_W].set((DIFF_HD ** -0.5) * LOG2E)
    wm = jnp.pad((wm * dscale[None, :]).astype(BF16), ((0, 0), (0, P_PAD - P_COLS)))
    ws = jnp.zeros((D_MODEL, SMALL_COLS), F32).at[:, :32].set(jnp.take(w, jnp.asarray(_SMALL_IDX), axis=1)).astype(BF16)
    gb_row = jnp.zeros((1, SMALL_COLS), F32).at[0, :16].set(mlstm_gate_bias[l])
    dtb_row = jnp.zeros((1, LANE), F32).at[0, 16:32].set(ssd_dt_bias[l].reshape(-1))
    a_row = jnp.zeros((1, LANE), F32).at[0, 16:32].set((-jnp.exp(ssd_a_log[l])).reshape(-1))
    dskip_row = jnp.repeat(ssd_d[l], SSD_HD)[None, :]
    qg = jnp.tile(gqa_q_norm[l][jnp.asarray(_DEINT64)], LANE // GQA_HD)[None, :]
    kg = jnp.tile(gqa_k_norm[l][jnp.asarray(_DEINT64)], LANE // GQA_HD)[None, :]
    wr = jnp.zeros((D_MODEL, LANE), F32).at[:, :MOE_GROUPS].set(w_router_group[l])
    wr = wr.at[:, MOE_GROUPS:MOE_GROUPS + MOE_EXPERTS].set(w_router_expert[l])
    return dict(wm=wm, ws=ws, gb_row=gb_row, dtb_row=dtb_row, a_row=a_row, dskip_row=dskip_row,
                qg=qg, kg=kg, wr=wr, conv_w=ssd_conv_w[l], conv_b=ssd_conv_b[l][None, :])


def _bd_ones():
    i = np.arange(LANE)
    return (i[:, None] // GQA_HD == i[None, :] // GQA_HD).astype(np.float32)


def _mixer_layer(x, B, N, l, lp, prm, tabs):
    cos_t, sin_t, ktab, qtab = tabs
    bd32 = jnp.asarray(_bd_ones())
    bd = bd32.astype(BF16)
    lambda_init = 0.8 - 0.6 * math.exp(-0.3 * l)
    p, h, small = _in_proj(x, prm["norm_mix"][l][None, :], lp["wm"], lp["ws"])
    y0 = _mlstm_branch(p, small, lp["gb_row"], prm["mlstm_norm"][l][None, :], B, N)
    y1 = _gqa_branch(p, cos_t, sin_t, lp["qg"], lp["kg"], bd, B, N)
    y2 = _ssd_branch(p, small, lp["conv_w"], lp["conv_b"], lp["dtb_row"], lp["a_row"], lp["dskip_row"],
                     prm["ssd_norm"][l][None, :], B, N)
    y3 = _diff_branch(p, ktab, qtab, bd32, prm["diff_lambda"][l], prm["diff_norm"][l][None, :], lambda_init, B, N)
    merged = _merge(h, (y0, y1, y2, y3), lp["wg"], lp["wb"])
    return _out_proj(x, merged, lp["wo"], prm["norm_ffn"][l][None, :], lp["wr"])


def _model(x_groups, prm, L):
    dims = [(x.shape[0], x.shape[1]) for x in x_groups]
    xs = [x.reshape(B * N, D_MODEL) for x, (B, N) in zip(x_groups, dims)]
    tabs = [_rope_tables(N) + _alibi_tables(N) for _, N in dims]
    for l in range(DEPTH):
        mixed = [_mixer_layer(x, B, N, l, L[l], prm, tb) for x, (B, N), tb in zip(xs, dims, tabs)]
        xcat, h2cat, rcat = (jnp.concatenate([m[i] for m in mixed], axis=0) for i in range(3))
        xcat = _moe(xcat, h2cat, rcat, L[l]["w1"], L[l]["w3"], L[l]["w2"], prm["norm_final"][None, :],
                    final=(l == DEPTH - 1))
        offs = np.cumsum([0] + [B * N for B, N in dims])
        xs = [xcat[offs[g]:offs[g + 1]] for g in range(len(dims))]
    return tuple(x.reshape(B, N, D_MODEL) for x, (B, N) in zip(xs, dims))


def kernel(x_prompt, x_sample, norm_mix, w_in, mlstm_gate_bias, mlstm_norm, gqa_q_norm, gqa_k_norm, ssd_conv_w, ssd_conv_b, ssd_dt_bias, ssd_a_log, ssd_d, ssd_norm, diff_lambda, diff_norm, w_branch, w_gate, w_out, norm_ffn, w_router_group, w_router_expert, moe_w_gate, moe_w_up, moe_w_down, norm_final):
    prm = dict(norm_mix=norm_mix, mlstm_norm=mlstm_norm, ssd_norm=ssd_norm, diff_lambda=diff_lambda,
               diff_norm=diff_norm, norm_ffn=norm_ffn, norm_final=norm_final)
    layers = []
    for l in range(DEPTH):
        lp = _prep_layer(l, w_in, mlstm_gate_bias, ssd_conv_w, ssd_conv_b, ssd_dt_bias, ssd_a_log, ssd_d,
                         gqa_q_norm, gqa_k_norm, w_router_group, w_router_expert)
        lp["wg"] = w_gate[l].astype(BF16)
        lp["wb"] = w_branch[l].astype(BF16)
        lp["wo"] = w_out[l].astype(BF16)
        lp["w1"] = moe_w_gate[l]
        lp["w3"] = moe_w_up[l]
        lp["w2"] = moe_w_down[l]
        layers.append(lp)
    return _model([x_prompt, x_sample], prm, layers)
```

```python
import functools
import math

import jax
import jax.numpy as jnp
import numpy as np
from jax import lax
from jax.experimental import pallas as pl
from jax.experimental.pallas import tpu as pltpu

F32 = jnp.float32
BF16 = jnp.bfloat16

D_MODEL = 1024
DEPTH = 2
GRID_W = 64
BRANCH_W = 512
NORM_EPS = 1e-6
MLSTM_HEADS = 4
MLSTM_HD = 128
GQA_HD = 64
GQA_HEADS = 8
GQA_KV_HEADS = 2
ROPE_THETA = 10000.0
SSD_HD = 64
SSD_HEADS = 8
SSD_GROUPS = 2
SSD_STATE = 128
SSD_CONV = 5
DIFF_HD = 64
DIFF_HEADS = 4
MOE_GROUPS = 4
MOE_EPG = 8
MOE_EXPERTS = 32
MOE_FF = 512
CHUNK = 128
LOG2E = 1.4426950408889634

_MLSTM_COLS = 4 * BRANCH_W + 4 * MLSTM_HEADS
_GQA_COLS = GQA_HEADS * GQA_HD + 2 * GQA_KV_HEADS * GQA_HD
_SSD_XBC = BRANCH_W + 2 * SSD_GROUPS * SSD_STATE
_SSD_COLS = BRANCH_W + _SSD_XBC + 2 * SSD_HEADS
_O0 = _MLSTM_COLS
_O1 = _O0 + _GQA_COLS
_O2 = _O1 + _SSD_COLS

ML_Q, ML_K, ML_V, ML_O = 0, 512, 1024, 1536
GQ_Q = 2048
SS_Z = 2560
DF_Q, DF_K, DF_V = 3072, 3584, 4096
SS_X, SS_BC = 4608, 5120
GQ_KV = 5632
P_COLS = 5888
P_PAD = 6144
SMALL_COLS = 128

LANE = 128
VMEM_LIMIT = 48 * 1024 * 1024

MOE_BLK = 256


def _cparams(sem):
    return pltpu.CompilerParams(dimension_semantics=sem, vmem_limit_bytes=VMEM_LIMIT)


def _dot(a, b):
    return jnp.dot(a, b, preferred_element_type=F32)


def _dot_nt(a, b):
    return lax.dot_general(a, b, (((1,), (1,)), ((), ())), preferred_element_type=F32)


def _dot_tn(a, b):
    return lax.dot_general(a, b, (((0,), (0,)), ((), ())), preferred_element_type=F32)


def _dot_hi(a, b):
    return jnp.dot(a, b, preferred_element_type=F32, precision=lax.Precision.HIGHEST)


def _sigmoid(x):
    return 1.0 / (1.0 + jnp.exp(-x))


def _silu(x):
    return x * _sigmoid(x)


def _log_sigmoid(x):
    return jnp.minimum(x, 0.0) - jnp.log(1.0 + jnp.exp(-jnp.abs(x)))


def _softplus(x):
    return jnp.maximum(x, 0.0) + jnp.log(1.0 + jnp.exp(-jnp.abs(x)))


def _main_col_index():
    gq = np.arange(GQA_HEADS * GQA_HD).reshape(GQA_HEADS, GQA_HD // 2, 2)
    gq = np.concatenate([gq[..., 0], gq[..., 1]], axis=-1).reshape(-1)
    gk = np.arange(GQA_KV_HEADS * GQA_HD).reshape(GQA_KV_HEADS, GQA_HD // 2, 2)
    gk = np.concatenate([gk[..., 0], gk[..., 1]], axis=-1).reshape(-1)
    segs = [
        np.arange(0, 4 * BRANCH_W),
        _O0 + gq,
        _O1 + np.arange(0, BRANCH_W),
        _O2 + np.arange(0, 3 * BRANCH_W),
        _O1 + BRANCH_W + np.arange(0, _SSD_XBC),
        _O0 + GQA_HEADS * GQA_HD + gk,
        _O0 + GQA_HEADS * GQA_HD + GQA_KV_HEADS * GQA_HD + np.arange(GQA_KV_HEADS * GQA_HD),
    ]
    idx = np.concatenate(segs)
    assert idx.shape[0] == P_COLS
    return idx


def _small_col_index():
    return np.concatenate([4 * BRANCH_W + np.arange(4 * MLSTM_HEADS),
                           _O1 + BRANCH_W + _SSD_XBC + np.arange(2 * SSD_HEADS)])


_MAIN_IDX = _main_col_index()
_SMALL_IDX = _small_col_index()
_DEINT64 = np.concatenate([np.arange(0, GQA_HD, 2), np.arange(1, GQA_HD, 2)])


def _inproj_kernel(x_ref, g_ref, w_ref, ws_ref, p_ref, h_ref, s_ref, h_scr):
    @pl.when(pl.program_id(1) == 0)
    def _():
        x = x_ref[...]
        ms = jnp.mean(x * x, axis=-1, keepdims=True)
        h = (x * lax.rsqrt(ms + NORM_EPS) * g_ref[...]).astype(BF16)
        h_scr[...] = h
        h_ref[...] = h
        s_ref[...] = _dot(h, ws_ref[...])

    p_ref[...] = _dot(h_scr[...], w_ref[...]).astype(BF16)


def _in_proj(x, g, w_main, w_small, tm=1024, tn=1024):
    T = x.shape[0]
    return pl.pallas_call(
        _inproj_kernel,
        grid=(T // tm, P_PAD // tn),
        in_specs=[
            pl.BlockSpec((tm, D_MODEL), lambda i, j: (i, 0)),
            pl.BlockSpec((1, D_MODEL), lambda i, j: (0, 0)),
            pl.BlockSpec((D_MODEL, tn), lambda i, j: (0, j)),
            pl.BlockSpec((D_MODEL, SMALL_COLS), lambda i, j: (0, 0)),
        ],
        out_specs=[
            pl.BlockSpec((tm, tn), lambda i, j: (i, j)),
            pl.BlockSpec((tm, D_MODEL), lambda i, j: (i, 0)),
            pl.BlockSpec((tm, SMALL_COLS), lambda i, j: (i, 0)),
        ],
        out_shape=[
            jax.ShapeDtypeStruct((T, P_PAD), BF16),
            jax.ShapeDtypeStruct((T, D_MODEL), BF16),
            jax.ShapeDtypeStruct((T, SMALL_COLS), F32),
        ],
        scratch_shapes=[pltpu.VMEM((tm, D_MODEL), BF16)],
        compiler_params=_cparams(("parallel", "arbitrary")),
        name="in_proj",
    )(x, g, w_main, w_small)


def _tri_masks(L, reverse):
    r = lax.broadcasted_iota(jnp.int32, (L, L), 0)
    c = lax.broadcasted_iota(jnp.int32, (L, L), 1)
    return (c >= r) if reverse else (c <= r)


def _mlstm_kernel(*refs, reverse, final):
    q_ref = refs[0]
    ct_scr, n_scr, m_scr = refs[-3:]

    @pl.when(pl.program_id(1) == 0)
    def _():
        ct_scr[...] = jnp.zeros_like(ct_scr)
        n_scr[...] = jnp.zeros_like(n_scr)
        m_scr[...] = jnp.zeros_like(m_scr)

    _mlstm_step(refs, reverse, final)


def _mlstm_step(refs, reverse, final):
    if final:
        (q_ref, k_ref, v_ref, sm_ref, gb_ref, o_ref, hb_ref, ng_ref,
         out_ref, ct_scr, n_scr, m_scr) = refs
    else:
        q_ref, k_ref, v_ref, sm_ref, gb_ref, out_ref, ct_scr, n_scr, m_scr = refs
    L = CHUNK
    H = MLSTM_HEADS
    d = MLSTM_HD
    nbat = q_ref.shape[0]
    pairs = [(bb, h) for bb in range(nbat) for h in range(H)]
    mask = _tri_masks(L, reverse)
    tri = mask.astype(F32)
    i_off = 8 if reverse else 0
    f_off = i_off + 4
    edge = 0 if reverse else L - 1
    scale = d ** -0.5
    lane = lax.broadcasted_iota(jnp.int32, (1, LANE), 1)

    gates = [sm_ref[bb] + gb_ref[...] for bb in range(nbat)]
    bcum = [_dot_hi(tri, _log_sigmoid(g)) for g in gates]
    gates_t = [g.T for g in gates]
    bcum_t = [b.T for b in bcum]
    m_all = [m_scr[bb] for bb in range(nbat)]

    def cols(h):
        return slice(h * d, (h + 1) * d)

    b_col, i_col, m_prev, m_t, w_intra, w_inter = {}, {}, {}, {}, {}, {}
    for pr in pairs:
        bb, h = pr
        b_col[pr] = bcum[bb][:, f_off + h:f_off + h + 1]
        i_col[pr] = gates[bb][:, i_off + h:i_off + h + 1]
        a_row = gates_t[bb][i_off + h:i_off + h + 1, :] - bcum_t[bb][f_off + h:f_off + h + 1, :]
        m_prev[pr] = m_all[bb][:, h:h + 1]
        dmat = jnp.where(mask, b_col[pr] + a_row, -jnp.inf)
        inter = b_col[pr] + m_prev[pr]
        m_t[pr] = jnp.maximum(inter, jnp.max(dmat, axis=1, keepdims=True))
        w_intra[pr] = jnp.exp(dmat - m_t[pr])
        w_inter[pr] = jnp.exp(inter - m_t[pr])

    ks, s = {}, {}
    for pr in pairs:
        bb, h = pr
        ks[pr] = (k_ref[bb, :, cols(h)].astype(F32) * scale).astype(BF16)
        s[pr] = _dot_nt(q_ref[bb, :, cols(h)], ks[pr]) * w_intra[pr]

    hs = {}
    for pr in pairs:
        bb, h = pr
        qh = q_ref[bb, :, cols(h)]
        ct = ct_scr[bb * H + h]
        n_row = n_scr[bb * H + h]
        num = _dot(s[pr].astype(BF16), v_ref[bb, :, cols(h)]) + w_inter[pr] * _dot(qh, ct.astype(BF16))
        qn = jnp.sum(qh.astype(F32) * n_row, axis=1, keepdims=True)
        den = jnp.sum(s[pr], axis=1, keepdims=True) + w_inter[pr] * qn
        hs[pr] = num / jnp.maximum(jnp.abs(den), jnp.exp(-m_t[pr]))

    for pr in pairs:
        bb, h = pr
        vh = v_ref[bb, :, cols(h)]
        btot = b_col[pr][edge:edge + 1, :]
        dec = btot - b_col[pr] + i_col[pr]
        m_new = jnp.maximum(btot + m_prev[pr], jnp.max(dec, axis=0, keepdims=True))
        ws = jnp.exp(dec - m_new)
        wc = jnp.exp(btot + m_prev[pr] - m_new)
        wsv = (ws * vh.astype(F32)).astype(BF16)
        ct_scr[bb * H + h] = wc * ct_scr[bb * H + h] + _dot_tn(ks[pr], wsv)
        n_scr[bb * H + h] = wc * n_scr[bb * H + h] + jnp.sum(ws * ks[pr].astype(F32), axis=0, keepdims=True)
        m_all[bb] = jnp.where(lane == h, m_new, m_all[bb])
    for bb in range(nbat):
        m_scr[bb] = m_all[bb]

    for pr in pairs:
        bb, h = pr
        if not final:
            out_ref[bb, :, cols(h)] = hs[pr]
        else:
            hsum = hs[pr] + hb_ref[bb, :, cols(h)]
            ms = jnp.mean(hsum * hsum, axis=1, keepdims=True)
            y = hsum * lax.rsqrt(ms + NORM_EPS) * ng_ref[:, cols(h)]
            o = o_ref[bb, :, cols(h)].astype(F32)
            out_ref[bb, :, cols(h)] = (_sigmoid(o) * y).astype(BF16)


SCAN_NBAT = 2


def _mlstm_dir(p3, small3, gate_bias_row, B, N, reverse, hb=None, norm_g=None):
    L = CHUNK
    nc = N // L
    nbat = SCAN_NBAT
    final = hb is not None

    def crow(c):
        return (nc - 1 - c) if reverse else c

    W = BRANCH_W
    in_specs = [
        pl.BlockSpec((nbat, L, W), lambda b, c: (b, crow(c), ML_Q // W)),
        pl.BlockSpec((nbat, L, W), lambda b, c: (b, crow(c), ML_K // W)),
        pl.BlockSpec((nbat, L, W), lambda b, c: (b, crow(c), ML_V // W)),
        pl.BlockSpec((nbat, L, SMALL_COLS), lambda b, c: (b, crow(c), 0)),
        pl.BlockSpec((1, SMALL_COLS), lambda b, c: (0, 0)),
    ]
    args = [p3, p3, p3, small3, gate_bias_row]
    if final:
        in_specs += [
            pl.BlockSpec((nbat, L, W), lambda b, c: (b, crow(c), ML_O // W)),
            pl.BlockSpec((nbat, L, W), lambda b, c: (b, crow(c), 0)),
            pl.BlockSpec((1, W), lambda b, c: (0, 0)),
        ]
        args += [p3, hb, norm_g]
    return pl.pallas_call(
        functools.partial(_mlstm_kernel, reverse=reverse, final=final),
        grid=(B // nbat, nc),
        in_specs=in_specs,
        out_specs=pl.BlockSpec((nbat, L, W), lambda b, c: (b, crow(c), 0)),
        out_shape=jax.ShapeDtypeStruct((B, N, W), BF16 if final else F32),
        scratch_shapes=[
            pltpu.VMEM((nbat * MLSTM_HEADS, MLSTM_HD, MLSTM_HD), F32),
            pltpu.VMEM((nbat * MLSTM_HEADS, 1, MLSTM_HD), F32),
            pltpu.VMEM((nbat, 1, LANE), F32),
        ],
        compiler_params=_cparams(("parallel", "arbitrary")),
        name="mlstm_bwd" if reverse else "mlstm_fwd",
    )(*args)


def _mlstm_branch(p, small, gate_bias_row, norm_g, B, N):
    p3 = p.reshape(B, N, P_PAD)
    small3 = small.reshape(B, N, SMALL_COLS)
    hb = _mlstm_dir(p3, small3, gate_bias_row, B, N, reverse=True)
    y = _mlstm_dir(p3, small3, gate_bias_row, B, N, reverse=False, hb=hb, norm_g=norm_g)
    return y.reshape(B * N, BRANCH_W)


def _gqa_prep_kernel(q_ref, kv_ref, cos_ref, sin_ref, qg_ref, kg_ref, bd_ref, qo_ref, ko_ref, vo_ref):
    tm = q_ref.shape[0]
    bd = bd_ref[...]
    cos = cos_ref[...]
    sin = sin_ref[...]
    lane = lax.broadcasted_iota(jnp.int32, (tm, LANE), 1)
    first = (lane % GQA_HD) < (GQA_HD // 2)
    lo = lane < GQA_HD

    def norm_rope(x, g):
        ss = _dot((x * x).astype(BF16), bd)
        xn = x * lax.rsqrt(ss * (1.0 / GQA_HD) + NORM_EPS) * g
        partner = jnp.where(first, pltpu.roll(xn, LANE - GQA_HD // 2, axis=1),
                            pltpu.roll(xn, GQA_HD // 2, axis=1))
        return xn * cos + partner * sin

    qscale = (GQA_HD ** -0.5) * LOG2E
    for a in range(GQA_HEADS * GQA_HD // LANE):
        x = q_ref[:, a * LANE:(a + 1) * LANE].astype(F32)
        qo_ref[:, a * LANE:(a + 1) * LANE] = (norm_rope(x, qg_ref[...]) * qscale).astype(BF16)
    k = norm_rope(kv_ref[:, 0:LANE].astype(F32), kg_ref[...])
    k_sw = pltpu.roll(k, GQA_HD, axis=1)
    ko_ref[0] = jnp.where(lo, k, k_sw).astype(BF16)
    ko_ref[1] = jnp.where(lo, k_sw, k).astype(BF16)
    v = kv_ref[:, LANE:2 * LANE].astype(F32)
    v_sw = pltpu.roll(v, GQA_HD, axis=1)
    vo_ref[0] = jnp.where(lo, v, 1.0).astype(BF16)
    vo_ref[1] = jnp.where(lo, v_sw, 1.0).astype(BF16)


def _gqa_prep(p, cos_t, sin_t, qg, kg, bd, B, N, tm=512):
    T = B * N
    nb = N // tm
    return pl.pallas_call(
        _gqa_prep_kernel,
        grid=(T // tm,),
        in_specs=[
            pl.BlockSpec((tm, 512), lambda i: (i, GQ_Q // 512)),
            pl.BlockSpec((tm, 256), lambda i: (i, GQ_KV // 256)),
            pl.BlockSpec((tm, LANE), lambda i: (i % nb, 0)),
            pl.BlockSpec((tm, LANE), lambda i: (i % nb, 0)),
            pl.BlockSpec((1, LANE), lambda i: (0, 0)),
            pl.BlockSpec((1, LANE), lambda i: (0, 0)),
            pl.BlockSpec((LANE, LANE), lambda i: (0, 0)),
        ],
        out_specs=[
            pl.BlockSpec((tm, 512), lambda i: (i, 0)),
            pl.BlockSpec((GQA_KV_HEADS, tm, LANE), lambda i: (0, i, 0)),
            pl.BlockSpec((GQA_KV_HEADS, tm, LANE), lambda i: (0, i, 0)),
        ],
        out_shape=[
            jax.ShapeDtypeStruct((T, 512), BF16),
            jax.ShapeDtypeStruct((GQA_KV_HEADS, T, LANE), BF16),
            jax.ShapeDtypeStruct((GQA_KV_HEADS, T, LANE), BF16),
        ],
        compiler_params=_cparams(("parallel",)),
        name="gqa_prep",
    )(p, p, cos_t, sin_t, qg, kg, bd)


FLASH_TKB = 512


def _softmax_keys(s_ref, p_ref, m_scr, al_scr, l_scr, M):
    tkb = s_ref.shape[0]
    for c in range(M // LANE):
        cols = slice(c * LANE, (c + 1) * LANE)
        mx = jnp.max(jnp.max(s_ref[:, cols].reshape(tkb // 8, 8, LANE), axis=0), axis=0, keepdims=True)
        m_prev = m_scr[:, cols]
        m_new = jnp.maximum(m_prev, mx)
        alpha = jnp.exp2(m_prev - m_new)
        p = jnp.exp2(s_ref[:, cols] - m_new)
        p_ref[:, cols] = p.astype(BF16)
        if l_scr is not None:
            lsum = jnp.sum(jnp.sum(p.reshape(tkb // 8, 8, LANE), axis=0), axis=0, keepdims=True)
            l_scr[:, cols] = alpha * l_scr[:, cols] + lsum
        al_scr[:, cols] = alpha
        m_scr[:, cols] = m_new


def _flash_blocks(nb, qk, pv, softmax, fix=None):
    qk(0)
    for i in range(nb):
        if i + 1 < nb:
            qk(i + 1)
        if fix is not None:
            fix(i)
        softmax(i)
        pv(i)


FLASH_SAFE_LOG2 = 120.0


def _flash_one_pass(nb, scores, values, m_scr, l_scr, acc_scr, p_scr):
    for i in range(nb):
        s = scores(i)
        tkb, M = s.shape
        m_used = m_scr[...]
        p = jnp.exp2(s - m_used)
        p_scr[i % 2] = p.astype(BF16)
        mx = jnp.max(jnp.max(s.reshape(tkb // 8, 8, M), axis=0), axis=0, keepdims=True)
        m_new = jnp.maximum(m_used, mx)
        alpha = jnp.exp2(m_used - m_new)
        if l_scr is not None:
            lsum = jnp.sum(jnp.sum(p.reshape(tkb // 8, 8, M), axis=0), axis=0, keepdims=True)
            l_scr[...] = alpha * (l_scr[...] + lsum)
        acc_scr[...] = alpha * (acc_scr[...] + _dot_tn(values(i), p_scr[i % 2]))
        m_scr[...] = m_new


def _flash_score_bound(qs, kmax):
    qf = qs.astype(F32)
    n2 = _dot_nt(jnp.ones((8, LANE), BF16), (qf * qf).astype(BF16))[0:1, :]
    return jnp.sqrt(n2) * kmax * 1.01 + 1.0


def _flash_is_safe(u_scr, m_scr):
    return jnp.where(jnp.max(u_scr[...] - m_scr[...]) <= FLASH_SAFE_LOG2, 1, 0)


def _gqa_flash_kernel(q_ref, k_ref, v_ref, kmax_ref, o_ref, qs_scr, m_scr, al_scr, acc_scr, s_scr, p_scr,
                      u_scr, safe_ref):
    tq = q_ref.shape[0]
    M = 4 * tq
    tkb = s_scr.shape[1]
    nb = k_ref.shape[0] // tkb
    kj = pl.program_id(3)

    @pl.when(kj == 0)
    def _():
        lane = lax.broadcasted_iota(jnp.int32, (tq, LANE), 1)
        lo = lane < GQA_HD
        zero = jnp.zeros((tq, LANE), BF16)
        for a in range(2):
            qa = q_ref[:, a * LANE:(a + 1) * LANE]
            qs_scr[(2 * a) * tq:(2 * a + 1) * tq, :] = jnp.where(lo, qa, zero)
            qs_scr[(2 * a + 1) * tq:(2 * a + 2) * tq, :] = jnp.where(lo, zero, qa)
        m_scr[...] = jnp.full_like(m_scr, -jnp.inf)
        acc_scr[...] = jnp.zeros_like(acc_scr)
        u_scr[...] = _flash_score_bound(qs_scr[...], kmax_ref[:, 0:1])
        safe_ref[0] = 0

    def scores(i):
        return _dot_nt(k_ref[i * tkb:(i + 1) * tkb, :], qs_scr[...])

    def values(i):
        return v_ref[i * tkb:(i + 1) * tkb, :]

    def qk(i):
        s_scr[i % 2] = scores(i)

    def softmax(i):
        _softmax_keys(s_scr.at[i % 2], p_scr.at[i % 2], m_scr, al_scr, None, M)

    def pv(i):
        acc_scr[...] = al_scr[...] * acc_scr[...] + _dot_tn(values(i), p_scr[i % 2])

    one_pass = safe_ref[0] == 1

    @pl.when(jnp.logical_not(one_pass))
    def _():
        _flash_blocks(nb, qk, pv, softmax)

    @pl.when(one_pass)
    def _():
        _flash_one_pass(nb, scores, values, m_scr, None, acc_scr, p_scr)

    @pl.when(kj == 0)
    def _():
        safe_ref[0] = _flash_is_safe(u_scr, m_scr)

    @pl.when(kj == pl.num_programs(3) - 1)
    def _():
        inv = 1.0 / acc_scr[GQA_HD:GQA_HD + 1, :]
        ot = jnp.concatenate([acc_scr[0:GQA_HD, h * tq:(h + 1) * tq] * inv[:, h * tq:(h + 1) * tq]
                              for h in range(4)], axis=0)
        o_ref[...] = ot.T.astype(BF16)


def _gqa_flash(qn, k2, va, B, N, tq=256, tk=2048):
    T = B * N
    tk = min(tk, N)
    nq, nk = N // tq, N // tk
    M = 4 * tq
    tkb = FLASH_TKB
    kf = k2[:, :, 0:GQA_HD].astype(F32).reshape(GQA_KV_HEADS, B, N, GQA_HD)
    kmax = jnp.max(jnp.sqrt(jnp.sum(kf * kf, axis=-1)), axis=-1)
    kmax = jnp.broadcast_to(kmax.T.reshape(B * GQA_KV_HEADS, 1, 1), (B * GQA_KV_HEADS, 1, LANE))
    return pl.pallas_call(
        _gqa_flash_kernel,
        grid=(B, GQA_KV_HEADS, nq, nk),
        in_specs=[
            pl.BlockSpec((tq, 256), lambda b, g, i, j: (b * nq + i, g)),
            pl.BlockSpec((None, tk, LANE), lambda b, g, i, j: (g, b * nk + j, 0)),
            pl.BlockSpec((None, tk, LANE), lambda b, g, i, j: (g, b * nk + j, 0)),
            pl.BlockSpec((None, 1, LANE), lambda b, g, i, j: (b * GQA_KV_HEADS + g, 0, 0)),
        ],
        out_specs=pl.BlockSpec((tq, 256), lambda b, g, i, j: (b * nq + i, g)),
        out_shape=jax.ShapeDtypeStruct((T, 512), BF16),
        scratch_shapes=[
            pltpu.VMEM((M, LANE), BF16),
            pltpu.VMEM((1, M), F32),
            pltpu.VMEM((1, M), F32),
            pltpu.VMEM((LANE, M), F32),
            pltpu.VMEM((2, tkb, M), F32),
            pltpu.VMEM((2, tkb, M), BF16),
            pltpu.VMEM((1, M), F32),
            pltpu.SMEM((1,), jnp.int32),
        ],
        compiler_params=_cparams(("parallel", "parallel", "parallel", "arbitrary")),
        name="gqa_flash",
    )(qn, k2, va, kmax)


def _rope_tables(N):
    rows = N // GRID_W
    row = jnp.repeat(jnp.arange(rows, dtype=F32), GRID_W)
    col = (jnp.arange(N) % GRID_W).astype(F32)
    half = GQA_HD // 2
    inv = 1.0 / (ROPE_THETA ** (jnp.arange(0, half, 2, dtype=F32) / half))
    ang = jnp.concatenate([row[:, None] * inv, col[:, None] * inv], axis=-1)
    cos, sin = jnp.cos(ang), jnp.sin(ang)
    cos_h = jnp.concatenate([cos, cos], axis=-1)
    sin_h = jnp.concatenate([-sin, sin], axis=-1)
    return jnp.tile(cos_h, (1, LANE // GQA_HD)), jnp.tile(sin_h, (1, LANE // GQA_HD))


def _gqa_branch(p, cos_t, sin_t, qg, kg, bd, B, N):
    qn, k2, va = _gqa_prep(p, cos_t, sin_t, qg, kg, bd, B, N)
    return _gqa_flash(qn, k2, va, B, N)


DIFF_SKIP_LOG2 = 40.0


def _alibi_slope_log2(h):
    return (2.0 ** (-8.0 * (h + 1) / DIFF_HEADS)) * LOG2E


def _alibi_tables(N):
    pos = np.arange(N)
    hi = ((pos // 128) * 128).astype(np.float32)
    lo = (pos % 128).astype(np.float32)
    kt = np.zeros((DIFF_HEADS, N, LANE), np.float32)
    qt = np.zeros((DIFF_HEADS, N, LANE), np.float32)
    for h in range(DIFF_HEADS):
        s = np.float32(_alibi_slope_log2(h))
        s1 = np.float32(s).astype(BF16).astype(np.float32)
        s2 = np.float32(s - s1).astype(BF16).astype(np.float32)
        for c, val in enumerate((hi, hi, lo, lo)):
            kt[h, :, c] = val
            qt[h, :, 4 + c] = -val
        for c, val in enumerate((s1, s2, s1, s2)):
            kt[h, :, 4 + c] = val
            qt[h, :, c] = val
    return (jnp.asarray(kt.reshape(DIFF_HEADS * N, LANE), BF16), jnp.asarray(qt.reshape(DIFF_HEADS * N, LANE), BF16))


def _diff_flash_kernel(lo_ref, hi_ref, q_ref, qa_ref, k_ref, ka_ref, v_ref, kmax_ref, lp_ref, ng_ref, o_ref,
                       qs_scr, m_scr, l_scr, al_scr, acc_scr, s_scr, p_scr, u_scr, safe_ref, *, lambda_init):
    tq = q_ref.shape[0]
    tk = k_ref.shape[0]
    M = 2 * tq
    tkb = s_scr.shape[1]
    nb = tk // tkb
    b, h, qi, kj = pl.program_id(0), pl.program_id(1), pl.program_id(2), pl.program_id(3)
    idx = (b * pl.num_programs(1) + h) * pl.num_programs(2) + qi
    first, last = lo_ref[idx], hi_ref[idx]

    @pl.when(kj == first)
    def _():
        lane = lax.broadcasted_iota(jnp.int32, (tq, LANE), 1)
        lo = lane < DIFF_HD
        zero = jnp.zeros((tq, LANE), BF16)
        q = q_ref[...]
        qa = qa_ref[...]
        for var, aug in enumerate((qa, -qa)):
            qs_scr[var, 0:tq, 0:LANE] = jnp.where(lo, q, zero)
            qs_scr[var, tq:M, 0:LANE] = jnp.where(lo, zero, q)
            qs_scr[var, 0:tq, LANE:2 * LANE] = aug
            qs_scr[var, tq:M, LANE:2 * LANE] = aug
        m_scr[...] = jnp.full_like(m_scr, -jnp.inf)
        l_scr[...] = jnp.zeros_like(l_scr)
        acc_scr[...] = jnp.zeros_like(acc_scr)
        u_scr[...] = _flash_score_bound(qs_scr[0, :, 0:LANE], kmax_ref[:, 0:1])
        safe_ref[0] = 0

    is_left = (kj + 1) * tk <= qi * tq
    is_right = kj * tk >= (qi + 1) * tq
    pure = jnp.logical_or(is_left, is_right)
    active = jnp.logical_and(kj >= first, kj <= last)
    sel = jnp.where(is_right, 1, 0)

    def scores(i):
        rows = slice(i * tkb, (i + 1) * tkb)
        kk = jnp.concatenate([k_ref[rows, :], ka_ref[rows, :]], axis=1)
        return _dot_nt(kk, qs_scr[sel])

    def values(i):
        return v_ref[i * tkb:(i + 1) * tkb, :]

    def qk(i):
        s_scr[i % 2] = scores(i)

    def fix(i):
        slope = jnp.exp2(-2.0 * (jnp.full((1, 1), h, jnp.int32).astype(F32) + 1.0)) * LOG2E
        j = kj * tk + i * tkb + lax.broadcasted_iota(jnp.int32, (tkb, 1), 0)
        col = lax.broadcasted_iota(jnp.int32, (1, M), 1)
        iq = qi * tq + col - jnp.where(col >= tq, tq, 0)
        d = jnp.maximum(j - iq, 0).astype(F32)
        s_scr[i % 2] = s_scr[i % 2] - (2.0 * slope) * d

    def softmax(i):
        _softmax_keys(s_scr.at[i % 2], p_scr.at[i % 2], m_scr, al_scr, l_scr, M)

    def pv(i):
        acc_scr[...] = al_scr[...] * acc_scr[...] + _dot_tn(values(i), p_scr[i % 2])

    @pl.when(jnp.logical_and(active, pure))
    def _():
        one_pass = jnp.logical_and(safe_ref[0] == 1, kj != first)

        @pl.when(jnp.logical_not(one_pass))
        def _():
            _flash_blocks(nb, qk, pv, softmax)

        @pl.when(one_pass)
        def _():
            _flash_one_pass(nb, scores, values, m_scr, l_scr, acc_scr, p_scr)

    @pl.when(jnp.logical_and(active, jnp.logical_not(pure)))
    def _():
        _flash_blocks(nb, qk, pv, softmax, fix=fix)

    @pl.when(kj == first)
    def _():
        safe_ref[0] = _flash_is_safe(u_scr, m_scr)

    @pl.when(kj == last)
    def _():
        lp = lp_ref[...]
        s01 = jnp.sum(jnp.sum(lp[0:1] * lp[1:2], axis=1, keepdims=True), axis=0, keepdims=True)
        s23 = jnp.sum(jnp.sum(lp[2:3] * lp[3:4], axis=1, keepdims=True), axis=0, keepdims=True)
        lam = jnp.exp(s01) - jnp.exp(s23) + lambda_init
        inv = 1.0 / l_scr[...]
        ot = acc_scr[:, 0:tq] * inv[:, 0:tq] - lam * (acc_scr[:, tq:M] * inv[:, tq:M])
        o = ot.T
        ms = jnp.mean(o * o, axis=1, keepdims=True)
        o_ref[...] = (o * lax.rsqrt(ms + NORM_EPS) * ng_ref[...] * (1.0 - lambda_init)).astype(BF16)


def _diff_prep_kernel(q_ref, k_ref, v_ref, bd_ref, ko_ref, vo_ref, nrm_ref):
    tm = q_ref.shape[0]
    lane = lax.broadcasted_iota(jnp.int32, (tm, LANE), 1)
    nrm = jnp.zeros((tm, LANE), F32)
    for h in range(DIFF_HEADS):
        cols = slice(h * LANE, (h + 1) * LANE)
        ko_ref[h] = k_ref[:, cols]
        vo_ref[h] = v_ref[:, cols]
        for src, off in ((q_ref, 0), (k_ref, DIFF_HEADS)):
            x = src[:, cols].astype(F32)
            ss = _dot_hi(x * x, bd_ref[...])
            n = jnp.sqrt(jnp.maximum(ss, pltpu.roll(ss, DIFF_HD, axis=1)))
            nrm = jnp.where(lane == off + h, n, nrm)
    nrm_ref[...] = nrm


def _diff_prep(p, bd32, B, N, tm=512):
    T = B * N
    W = BRANCH_W
    return pl.pallas_call(
        _diff_prep_kernel,
        grid=(T // tm,),
        in_specs=[
            pl.BlockSpec((tm, W), lambda i: (i, DF_Q // W)),
            pl.BlockSpec((tm, W), lambda i: (i, DF_K // W)),
            pl.BlockSpec((tm, W), lambda i: (i, DF_V // W)),
            pl.BlockSpec((LANE, LANE), lambda i: (0, 0)),
        ],
        out_specs=[
            pl.BlockSpec((DIFF_HEADS, tm, LANE), lambda i: (0, i, 0)),
            pl.BlockSpec((DIFF_HEADS, tm, LANE), lambda i: (0, i, 0)),
            pl.BlockSpec((tm, LANE), lambda i: (i, 0)),
        ],
        out_shape=[
            jax.ShapeDtypeStruct((DIFF_HEADS, T, LANE), BF16),
            jax.ShapeDtypeStruct((DIFF_HEADS, T, LANE), BF16),
            jax.ShapeDtypeStruct((T, LANE), F32),
        ],
        compiler_params=_cparams(("parallel",)),
        name="diff_prep",
    )(p, p, p, bd32)


def _diff_bands(nrm, B, N, tq, tk):
    nq, nkt = N // tq, N // tk
    qn = nrm[:, 0:DIFF_HEADS].reshape(B, N, DIFF_HEADS)
    kn = nrm[:, DIFF_HEADS:2 * DIFF_HEADS].reshape(B, N, DIFF_HEADS)
    qmax = jnp.max(qn.reshape(B, nq, tq, DIFF_HEADS), axis=2)
    kself = jnp.max(kn.reshape(B, nq, tq, DIFF_HEADS), axis=2)
    kmax = jnp.max(kn.reshape(B, nkt, tk, DIFF_HEADS), axis=2)
    bound = qmax[:, :, None, :] * (kmax[:, None, :, :] + kself[:, :, None, :]) * 1.001 + DIFF_SKIP_LOG2
    q0 = np.arange(nq)[:, None] * tq
    k0 = np.arange(nkt)[None, :] * tk
    dist = np.maximum(0, np.maximum(q0 - (k0 + tk - 1), k0 - (q0 + tq - 1))).astype(np.float32)
    slopes = np.array([_alibi_slope_log2(h) for h in range(DIFF_HEADS)], np.float32)
    keep = bound > jnp.asarray(dist)[None, :, :, None] * jnp.asarray(slopes)[None, None, None, :]
    keep = jnp.transpose(keep, (0, 3, 1, 2))
    first = jnp.argmax(keep, axis=-1).astype(jnp.int32)
    last = (nkt - 1 - jnp.argmax(keep[..., ::-1], axis=-1)).astype(jnp.int32)
    return first.reshape(-1), last.reshape(-1)


def _diff_branch(p, ktab, qtab, bd32, lam_params, norm_g, lambda_init, B, N, tq=512, tk=2048):
    T = B * N
    nq, nk = N // tq, N // tk
    M = 2 * tq
    tkb = FLASH_TKB
    kd, vd, nrm = _diff_prep(p, bd32, B, N)
    first, last = _diff_bands(nrm, B, N, tq, tk)
    kmax = jnp.max(nrm[:, DIFF_HEADS:2 * DIFF_HEADS].reshape(B, N, DIFF_HEADS), axis=1)
    kmax = jnp.broadcast_to(kmax.reshape(B * DIFF_HEADS, 1, 1), (B * DIFF_HEADS, 1, LANE))

    def kstep(b, h, i, j, lo, hi):
        idx = (b * DIFF_HEADS + h) * nq + i
        return jnp.minimum(jnp.maximum(j, lo[idx]), hi[idx])

    return pl.pallas_call(
        functools.partial(_diff_flash_kernel, lambda_init=lambda_init),
        grid_spec=pltpu.PrefetchScalarGridSpec(
            num_scalar_prefetch=2,
            grid=(B, DIFF_HEADS, nq, nk),
            in_specs=[
                pl.BlockSpec((tq, LANE), lambda b, h, i, j, lo, hi: (b * nq + i, DF_Q // LANE + h)),
                pl.BlockSpec((tq, LANE), lambda b, h, i, j, lo, hi: (h * nq + i, 0)),
                pl.BlockSpec((None, tk, LANE), lambda b, h, i, j, lo, hi: (h, b * nk + kstep(b, h, i, j, lo, hi), 0)),
                pl.BlockSpec((tk, LANE), lambda b, h, i, j, lo, hi: (h * nk + kstep(b, h, i, j, lo, hi), 0)),
                pl.BlockSpec((None, tk, LANE), lambda b, h, i, j, lo, hi: (h, b * nk + kstep(b, h, i, j, lo, hi), 0)),
                pl.BlockSpec((None, 1, LANE), lambda b, h, i, j, lo, hi: (b * DIFF_HEADS + h, 0, 0)),
                pl.BlockSpec((4, DIFF_HD), lambda b, h, i, j, lo, hi: (0, 0)),
                pl.BlockSpec((1, LANE), lambda b, h, i, j, lo, hi: (0, h)),
            ],
            out_specs=pl.BlockSpec((tq, LANE), lambda b, h, i, j, lo, hi: (b * nq + i, h)),
            scratch_shapes=[
                pltpu.VMEM((2, M, 2 * LANE), BF16),
                pltpu.VMEM((1, M), F32),
                pltpu.VMEM((1, M), F32),
                pltpu.VMEM((1, M), F32),
                pltpu.VMEM((LANE, M), F32),
                pltpu.VMEM((2, tkb, M), F32),
                pltpu.VMEM((2, tkb, M), BF16),
                pltpu.VMEM((1, M), F32),
                pltpu.SMEM((1,), jnp.int32),
            ],
        ),
        out_shape=jax.ShapeDtypeStruct((T, BRANCH_W), BF16),
        compiler_params=_cparams(("parallel", "parallel", "parallel", "arbitrary")),
        name="diff_flash",
    )(first, last, p, qtab, kd, ktab, vd, kmax, lam_params, norm_g)


SSD_HALO = 16


def _ssd_prep_kernel(x_ref, xp_ref, xn_ref, bc_ref, bcp_ref, bcn_ref, sm_ref, cw_ref, cb_ref,
                     dtb_ref, arow_ref, xo_ref, bco_ref, dto_ref):
    tc = x_ref.shape[0]
    c = pl.program_id(1)
    nc = pl.num_programs(1)
    has_prev = jnp.where(c > 0, 1.0, 0.0)
    has_next = jnp.where(c < nc - 1, 1.0, 0.0)
    pad = SSD_CONV // 2

    def conv(main_ref, prev_ref, next_ref, off):
        xf = jnp.concatenate([prev_ref[...].astype(F32) * has_prev, main_ref[...].astype(F32),
                              next_ref[...].astype(F32) * has_next], axis=0)
        n = xf.shape[0]
        acc = jnp.zeros((tc, xf.shape[1]), F32) + cb_ref[:, off:off + xf.shape[1]]
        for j in range(SSD_CONV):
            sh = pltpu.roll(xf, (pad - j) % n, axis=0)[SSD_HALO:SSD_HALO + tc, :]
            acc = acc + sh * cw_ref[j:j + 1, off:off + xf.shape[1]]
        return _silu(acc)

    xo_ref[...] = conv(x_ref, xp_ref, xn_ref, 0).astype(BF16)
    bco_ref[...] = conv(bc_ref, bcp_ref, bcn_ref, BRANCH_W).astype(BF16)
    dt = _softplus(sm_ref[...] + dtb_ref[...])
    a = pltpu.roll(dt * arow_ref[...], LANE - 16, axis=1)
    lane = lax.broadcasted_iota(jnp.int32, (tc, LANE), 1)
    dto_ref[...] = jnp.where(lane < 16, a, jnp.where(lane < 32, dt, 0.0))


def _ssd_prep(p, small, conv_w, conv_b, dtb_row, a_row, B, N, tc=512):
    T = B * N
    nc = N // tc
    hb = tc // SSD_HALO
    W = BRANCH_W

    def main(col):
        return pl.BlockSpec((tc, W), lambda b, c: (b * nc + c, col))

    def prev(col):
        return pl.BlockSpec((SSD_HALO, W), lambda b, c: (jnp.maximum((b * nc + c) * hb - 1, 0), col))

    def nxt(col):
        return pl.BlockSpec((SSD_HALO, W), lambda b, c: (jnp.minimum((b * nc + c + 1) * hb, T // SSD_HALO - 1), col))

    cx, cbc = SS_X // W, SS_BC // W
    return pl.pallas_call(
        _ssd_prep_kernel,
        grid=(B, nc),
        in_specs=[
            main(cx), prev(cx), nxt(cx), main(cbc), prev(cbc), nxt(cbc),
            pl.BlockSpec((tc, SMALL_COLS), lambda b, c: (b * nc + c, 0)),
            pl.BlockSpec((SSD_CONV, 2 * W), lambda b, c: (0, 0)),
            pl.BlockSpec((1, 2 * W), lambda b, c: (0, 0)),
            pl.BlockSpec((1, LANE), lambda b, c: (0, 0)),
            pl.BlockSpec((1, LANE), lambda b, c: (0, 0)),
        ],
        out_specs=[
            pl.BlockSpec((tc, W), lambda b, c: (b * nc + c, 0)),
            pl.BlockSpec((tc, W), lambda b, c: (b * nc + c, 0)),
            pl.BlockSpec((tc, LANE), lambda b, c: (b * nc + c, 0)),
        ],
        out_shape=[
            jax.ShapeDtypeStruct((T, W), BF16),
            jax.ShapeDtypeStruct((T, W), BF16),
            jax.ShapeDtypeStruct((T, LANE), F32),
        ],
        compiler_params=_cparams(("parallel", "parallel")),
        name="ssd_prep",
    )(p, p, p, p, p, p, small, conv_w, conv_b, dtb_row, a_row)


def _ssd_kernel(*refs, reverse, final):
    x_ref = refs[0]
    hs_scr = refs[-1]

    @pl.when(pl.program_id(1) == 0)
    def _():
        hs_scr[...] = jnp.zeros_like(hs_scr)

    for bb in range(x_ref.shape[0]):
        _ssd_chunk(bb, refs, reverse, final)


def _ssd_chunk(bb, refs, reverse, final):
    if final:
        (x_ref, bc_ref, dta_ref, ea_ref, ed_ref, z_ref, yb_ref, dsk_ref, ng_ref,
         out_ref, hs_scr) = refs
    else:
        x_ref, bc_ref, dta_ref, ea_ref, ed_ref, out_ref, hs_scr = refs
    L = CHUNK
    S = SSD_STATE
    W = BRANCH_W
    GW = W // SSD_GROUPS
    mask = _tri_masks(L, reverse)
    tri = mask.astype(F32)
    dta = dta_ref[bb]
    acs = _dot_hi(tri, dta)
    acs_t = acs.T
    acs_e = _dot_hi(acs, ea_ref[...])
    dt_e = _dot_hi(dta, ed_ref[...])
    edge = 0 if reverse else L - 1
    a_off = 8 if reverse else 0
    acs_end = acs_e[edge:edge + 1, :]
    xf = x_ref[bb].astype(F32)
    xdt = xf * dt_e
    xdt_b = xdt.astype(BF16)
    xdend = (xdt * jnp.exp(acs_end - acs_e)).astype(BF16)
    e_acs = jnp.exp(acs_e)
    cdec = jnp.exp(acs_end)
    lane = lax.broadcasted_iota(jnp.int32, (L, LANE), 1)
    lo = lane < SSD_HD
    zero = jnp.zeros((L, LANE), BF16)

    ys = []
    for g in range(SSD_GROUPS):
        bg = bc_ref[bb, :, g * S:(g + 1) * S]
        cg = bc_ref[bb, :, SSD_GROUPS * S + g * S:SSD_GROUPS * S + (g + 1) * S]
        cb = _dot_nt(cg, bg)
        hs = hs_scr[bb * SSD_GROUPS + g]
        y_off = _dot(cg, hs.astype(BF16)) * e_acs[:, g * GW:(g + 1) * GW]
        hs_scr[bb * SSD_GROUPS + g] = (cdec[:, g * GW:(g + 1) * GW] * hs
                                       + _dot_tn(bg, xdend[:, g * GW:(g + 1) * GW]))
        for pr in range(GW // LANE):
            col = g * GW + pr * LANE
            xp = xdt_b[:, col:col + LANE]
            yd = jnp.zeros((L, LANE), F32)
            for hh in range(2):
                hd = (col // SSD_HD) + hh
                a_col = acs[:, a_off + hd:a_off + hd + 1]
                a_row = acs_t[a_off + hd:a_off + hd + 1, :]
                decay = jnp.exp(jnp.where(mask, a_col - a_row, -jnp.inf))
                wm = (cb * decay).astype(BF16)
                xm = jnp.where(lo, xp, zero) if hh == 0 else jnp.where(lo, zero, xp)
                yd = yd + _dot(wm, xm)
            ys.append(yd + y_off[:, pr * LANE:(pr + 1) * LANE])
    y = jnp.concatenate(ys, axis=1)

    if not final:
        out_ref[bb] = y
    else:
        y = y + yb_ref[bb] + dsk_ref[...] * xf
        y = y * _silu(z_ref[bb].astype(F32))
        outs = []
        for g in range(SSD_GROUPS):
            yg = y[:, g * GW:(g + 1) * GW]
            ms = jnp.mean(yg * yg, axis=1, keepdims=True)
            outs.append(yg * lax.rsqrt(ms + NORM_EPS))
        out_ref[bb] = (jnp.concatenate(outs, axis=1) * ng_ref[...]).astype(BF16)


def _ssd_dir(xs, bc, dta, ea, ed, B, N, reverse, p3=None, yb=None, dskip_row=None, norm_g=None):
    L = CHUNK
    nc = N // L
    nbat = SCAN_NBAT
    W = BRANCH_W
    final = yb is not None

    def crow(c):
        return (nc - 1 - c) if reverse else c

    in_specs = [
        pl.BlockSpec((nbat, L, W), lambda b, c: (b, crow(c), 0)),
        pl.BlockSpec((nbat, L, W), lambda b, c: (b, crow(c), 0)),
        pl.BlockSpec((nbat, L, LANE), lambda b, c: (b, crow(c), 0)),
        pl.BlockSpec((LANE, W), lambda b, c: (0, 0)),
        pl.BlockSpec((LANE, W), lambda b, c: (0, 0)),
    ]
    args = [xs, bc, dta, ea, ed]
    if final:
        in_specs += [
            pl.BlockSpec((nbat, L, W), lambda b, c: (b, crow(c), SS_Z // W)),
            pl.BlockSpec((nbat, L, W), lambda b, c: (b, crow(c), 0)),
            pl.BlockSpec((1, W), lambda b, c: (0, 0)),
            pl.BlockSpec((1, W), lambda b, c: (0, 0)),
        ]
        args += [p3, yb, dskip_row, norm_g]
    return pl.pallas_call(
        functools.partial(_ssd_kernel, reverse=reverse, final=final),
        grid=(B // nbat, nc),
        in_specs=in_specs,
        out_specs=pl.BlockSpec((nbat, L, W), lambda b, c: (b, crow(c), 0)),
        out_shape=jax.ShapeDtypeStruct((B, N, W), BF16 if final else F32),
        scratch_shapes=[pltpu.VMEM((nbat * SSD_GROUPS, SSD_STATE, W // SSD_GROUPS), F32)],
        compiler_params=_cparams(("parallel", "arbitrary")),
        name="ssd_bwd" if reverse else "ssd_fwd",
    )(*args)


def _ssd_expand_mats():
    ea_f = np.zeros((LANE, BRANCH_W), np.float32)
    ea_b = np.zeros((LANE, BRANCH_W), np.float32)
    ed_f = np.zeros((LANE, BRANCH_W), np.float32)
    ed_b = np.zeros((LANE, BRANCH_W), np.float32)
    for h in range(SSD_HEADS):
        ea_f[h, h * SSD_HD:(h + 1) * SSD_HD] = 1.0
        ea_b[8 + h, h * SSD_HD:(h + 1) * SSD_HD] = 1.0
        ed_f[16 + h, h * SSD_HD:(h + 1) * SSD_HD] = 1.0
        ed_b[24 + h, h * SSD_HD:(h + 1) * SSD_HD] = 1.0
    return ea_f, ea_b, ed_f, ed_b


_EA_F, _EA_B, _ED_F, _ED_B = _ssd_expand_mats()


def _ssd_branch(p, small, conv_w, conv_b, dtb_row, a_row, dskip_row, norm_g, B, N):
    xs, bc, dta = _ssd_prep(p, small, conv_w, conv_b, dtb_row, a_row, B, N)
    xs, bc, dta = (a.reshape(B, N, a.shape[-1]) for a in (xs, bc, dta))
    yb = _ssd_dir(xs, bc, dta, jnp.asarray(_EA_B), jnp.asarray(_ED_B), B, N, reverse=True)
    y = _ssd_dir(xs, bc, dta, jnp.asarray(_EA_F), jnp.asarray(_ED_F), B, N, reverse=False,
                 p3=p.reshape(B, N, P_PAD), yb=yb, dskip_row=dskip_row, norm_g=norm_g)
    return y.reshape(B * N, BRANCH_W)


def _merge_kernel(h_ref, y0_ref, y1_ref, y2_ref, y3_ref, wg_ref, wb_ref, o_ref):
    h = h_ref[...]
    acc = None
    for i, y_ref in enumerate((y0_ref, y1_ref, y2_ref, y3_ref)):
        gate = _sigmoid(_dot(h, wg_ref[i]))
        term = gate * _dot(y_ref[...], wb_ref[i])
        acc = term if acc is None else acc + term
    o_ref[...] = acc.astype(BF16)


def _merge(h, ys, wg, wb, tm=1024, tn=256):
    T = h.shape[0]
    return pl.pallas_call(
        _merge_kernel,
        grid=(T // tm, D_MODEL // tn),
        in_specs=[pl.BlockSpec((tm, D_MODEL), lambda i, j: (i, 0))]
        + [pl.BlockSpec((tm, BRANCH_W), lambda i, j: (i, 0))] * 4
        + [
            pl.BlockSpec((4, D_MODEL, tn), lambda i, j: (0, 0, j)),
            pl.BlockSpec((4, BRANCH_W, tn), lambda i, j: (0, 0, j)),
        ],
        out_specs=pl.BlockSpec((tm, tn), lambda i, j: (i, j)),
        out_shape=jax.ShapeDtypeStruct((T, D_MODEL), BF16),
        compiler_params=_cparams(("parallel", "arbitrary")),
        name="merge",
    )(h, *ys, wg, wb)


def _outproj_kernel(x_ref, m_ref, wo_ref, g_ref, wr_ref, xo_ref, h2_ref, r_ref):
    x = x_ref[...] + _dot(m_ref[...], wo_ref[...])
    xo_ref[...] = x
    ms = jnp.mean(x * x, axis=-1, keepdims=True)
    h2 = x * lax.rsqrt(ms + NORM_EPS) * g_ref[...]
    h2_ref[...] = h2.astype(BF16)
    logits = _dot_hi(h2, wr_ref[...])
    tm = x.shape[0]
    lane = lax.broadcasted_iota(jnp.int32, (tm, LANE), 1).astype(F32)
    neg = -jnp.inf
    big = float(LANE)
    gl = jnp.where(lane < MOE_GROUPS, logits, neg)
    gmax = jnp.max(gl, axis=1, keepdims=True)
    g_idx = jnp.min(jnp.where(gl == gmax, lane, big), axis=1, keepdims=True)
    g_w = 1.0 / jnp.sum(jnp.exp(gl - gmax), axis=1, keepdims=True)
    e_lo = MOE_GROUPS + g_idx * MOE_EPG
    el = jnp.where(lane >= e_lo, jnp.where(lane < e_lo + MOE_EPG, logits, neg), neg)
    m1 = jnp.max(el, axis=1, keepdims=True)
    i1 = jnp.min(jnp.where(el == m1, lane, big), axis=1, keepdims=True)
    el2 = jnp.where(lane == i1, neg, el)
    m2 = jnp.max(el2, axis=1, keepdims=True)
    i2 = jnp.min(jnp.where(el2 == m2, lane, big), axis=1, keepdims=True)
    den = jnp.sum(jnp.exp(el - m1), axis=1, keepdims=True)
    p1 = 1.0 / den
    p2 = jnp.exp(m2 - m1) / den
    w1 = p1 / (p1 + p2) * g_w
    w2 = p2 / (p1 + p2) * g_w
    e1 = i1 - MOE_GROUPS
    e2 = i2 - MOE_GROUPS
    r_ref[...] = jnp.where(lane == 0, e1, jnp.where(lane == 1, e2, jnp.where(lane == 2, w1, jnp.where(lane == 3, w2, 0.0))))


def _out_proj(x, merged, wo, g, wr, tm=512):
    T = x.shape[0]
    return pl.pallas_call(
        _outproj_kernel,
        grid=(T // tm,),
        in_specs=[
            pl.BlockSpec((tm, D_MODEL), lambda i: (i, 0)),
            pl.BlockSpec((tm, D_MODEL), lambda i: (i, 0)),
            pl.BlockSpec((D_MODEL, D_MODEL), lambda i: (0, 0)),
            pl.BlockSpec((1, D_MODEL), lambda i: (0, 0)),
            pl.BlockSpec((D_MODEL, LANE), lambda i: (0, 0)),
        ],
        out_specs=[
            pl.BlockSpec((tm, D_MODEL), lambda i: (i, 0)),
            pl.BlockSpec((tm, D_MODEL), lambda i: (i, 0)),
            pl.BlockSpec((tm, LANE), lambda i: (i, 0)),
        ],
        out_shape=[
            jax.ShapeDtypeStruct((T, D_MODEL), F32),
            jax.ShapeDtypeStruct((T, D_MODEL), BF16),
            jax.ShapeDtypeStruct((T, LANE), F32),
        ],
        compiler_params=_cparams(("parallel",)),
        name="out_proj_router",
    )(x, merged, wo, g, wr)


def _moe_ffn_kernel(be_ref, nu_ref, xs_ref, w1_ref, w3_ref, w2_ref, ys_ref):
    used = pl.program_id(0) < nu_ref[0]

    @pl.when(used)
    def _():
        xb = xs_ref[...]
        a = _dot(xb, w1_ref[0].astype(BF16))
        u = _dot(xb, w3_ref[0].astype(BF16))
        hmid = (_silu(a) * u).astype(BF16)
        ys_ref[...] = _dot(hmid, w2_ref[0].astype(BF16)).astype(BF16)

    @pl.when(jnp.logical_not(used))
    def _():
        ys_ref[...] = jnp.zeros_like(ys_ref)


def _moe_ffn(xs, blk_e, nused, w1, w3, w2):
    R = xs.shape[0]
    nblk = R // MOE_BLK
    return pl.pallas_call(
        _moe_ffn_kernel,
        grid_spec=pltpu.PrefetchScalarGridSpec(
            num_scalar_prefetch=2,
            grid=(nblk,),
            in_specs=[
                pl.BlockSpec((MOE_BLK, D_MODEL), lambda b, be, nu: (b, 0)),
                pl.BlockSpec((1, D_MODEL, MOE_FF), lambda b, be, nu: (be[b], 0, 0)),
                pl.BlockSpec((1, D_MODEL, MOE_FF), lambda b, be, nu: (be[b], 0, 0)),
                pl.BlockSpec((1, MOE_FF, D_MODEL), lambda b, be, nu: (be[b], 0, 0)),
            ],
            out_specs=pl.BlockSpec((MOE_BLK, D_MODEL), lambda b, be, nu: (b, 0)),
        ),
        out_shape=jax.ShapeDtypeStruct((R, D_MODEL), BF16),
        compiler_params=_cparams(("arbitrary",)),
        name="moe_ffn",
    )(blk_e, nused, xs, w1, w3, w2)


def _combine_kernel(x_ref, a_ref, b_ref, r_ref, g_ref, o_ref, *, final):
    r = r_ref[...]
    x = x_ref[...] + (r[:, 2:3] * a_ref[...].astype(F32) + r[:, 3:4] * b_ref[...].astype(F32))
    if final:
        ms = jnp.mean(x * x, axis=-1, keepdims=True)
        x = x * lax.rsqrt(ms + NORM_EPS) * g_ref[...]
    o_ref[...] = x


def _combine(x, a, b, route, g, final, row0, tm=512):
    T = x.shape[0]
    off = row0 // tm
    return pl.pallas_call(
        functools.partial(_combine_kernel, final=final),
        grid=(T // tm,),
        in_specs=[pl.BlockSpec((tm, D_MODEL), lambda i: (i, 0))]
        + [pl.BlockSpec((tm, D_MODEL), lambda i: (i + off, 0))] * 2
        + [pl.BlockSpec((tm, LANE), lambda i: (i + off, 0)), pl.BlockSpec((1, D_MODEL), lambda i: (0, 0))],
        out_specs=pl.BlockSpec((tm, D_MODEL), lambda i: (i, 0)),
        out_shape=jax.ShapeDtypeStruct((T, D_MODEL), F32),
        compiler_params=_cparams(("parallel",)),
        name="moe_combine",
    )(x, a, b, route, g)


def _moe(xs_groups, h2, route, w1, w3, w2, g_final, final):
    T = h2.shape[0]
    K = 2
    e_idx = route[:, 0:K].astype(jnp.int32)
    flat_e = e_idx.reshape(-1)
    onehot = (flat_e[:, None] == jnp.arange(MOE_EXPERTS)[None, :]).astype(jnp.int32)
    csum = jnp.cumsum(onehot, axis=0)
    rank = jnp.take_along_axis(csum, flat_e[:, None], axis=1)[:, 0] - 1
    counts = csum[-1]
    pcounts = (counts + MOE_BLK - 1) // MOE_BLK * MOE_BLK
    pends = jnp.cumsum(pcounts)
    pstarts = pends - pcounts
    dest = pstarts[flat_e] + rank
    R = T * K + MOE_EXPERTS * MOE_BLK
    nblk = R // MOE_BLK
    row_src = jnp.zeros((R,), jnp.int32).at[dest].set(jnp.arange(T * K, dtype=jnp.int32) // K)
    nused = (pends[-1] // MOE_BLK).astype(jnp.int32)
    blk_start = jnp.arange(nblk, dtype=jnp.int32) * MOE_BLK
    blk_e = jnp.minimum(jnp.sum((pends[None, :] <= blk_start[:, None]).astype(jnp.int32), axis=1), MOE_EXPERTS - 1)
    last_e = blk_e[jnp.maximum(nused - 1, 0)]
    blk_e = jnp.where(jnp.arange(nblk) < nused, blk_e, last_e)
    xs = jnp.take(h2, row_src, axis=0, mode="clip")
    ys = _moe_ffn(xs, blk_e, nused.reshape(1), w1, w3, w2)
    pos = dest.reshape(T, K)
    ga = jnp.take(ys, pos[:, 0], axis=0, mode="clip")
    gb = jnp.take(ys, pos[:, 1], axis=0, mode="clip")
    outs, row0 = [], 0
    for x in xs_groups:
        outs.append(_combine(x, ga, gb, route, g_final, final, row0))
        row0 += x.shape[0]
    return outs


def _prep_layer(l, w_in, mlstm_gate_bias, ssd_conv_w, ssd_conv_b, ssd_dt_bias, ssd_a_log, ssd_d,
                gqa_q_norm, gqa_k_norm, w_router_group, w_router_expert):
    w = w_in[l]
    wm = jnp.take(w, jnp.asarray(_MAIN_IDX), axis=1)
    dscale = jnp.ones((P_COLS,), F32).at[DF_Q:DF_Q + BRANCH_W].set((DIFF_HD ** -0.5) * LOG2E)
    wm = jnp.pad((wm * dscale[None, :]).astype(BF16), ((0, 0), (0, P_PAD - P_COLS)))
    ws = jnp.zeros((D_MODEL, SMALL_COLS), F32).at[:, :32].set(jnp.take(w, jnp.asarray(_SMALL_IDX), axis=1)).astype(BF16)
    gb_row = jnp.zeros((1, SMALL_COLS), F32).at[0, :16].set(mlstm_gate_bias[l])
    dtb_row = jnp.zeros((1, LANE), F32).at[0, 16:32].set(ssd_dt_bias[l].reshape(-1))
    a_row = jnp.zeros((1, LANE), F32).at[0, 16:32].set((-jnp.exp(ssd_a_log[l])).reshape(-1))
    dskip_row = jnp.repeat(ssd_d[l], SSD_HD)[None, :]
    qg = jnp.tile(gqa_q_norm[l][jnp.asarray(_DEINT64)], LANE // GQA_HD)[None, :]
    kg = jnp.tile(gqa_k_norm[l][jnp.asarray(_DEINT64)], LANE // GQA_HD)[None, :]
    wr = jnp.zeros((D_MODEL, LANE), F32).at[:, :MOE_GROUPS].set(w_router_group[l])
    wr = wr.at[:, MOE_GROUPS:MOE_GROUPS + MOE_EXPERTS].set(w_router_expert[l])
    return dict(wm=wm, ws=ws, gb_row=gb_row, dtb_row=dtb_row, a_row=a_row, dskip_row=dskip_row,
                qg=qg, kg=kg, wr=wr, conv_w=ssd_conv_w[l], conv_b=ssd_conv_b[l][None, :])


def _bd_ones():
    i = np.arange(LANE)
    return (i[:, None] // GQA_HD == i[None, :] // GQA_HD).astype(np.float32)


def _mixer_layer(x, B, N, l, lp, prm, tabs):
    cos_t, sin_t, ktab, qtab = tabs
    bd32 = jnp.asarray(_bd_ones())
    bd = bd32.astype(BF16)
    lambda_init = 0.8 - 0.6 * math.exp(-0.3 * l)
    p, h, small = _in_proj(x, prm["norm_mix"][l][None, :], lp["wm"], lp["ws"])
    y0 = _mlstm_branch(p, small, lp["gb_row"], prm["mlstm_norm"][l][None, :], B, N)
    y1 = _gqa_branch(p, cos_t, sin_t, lp["qg"], lp["kg"], bd, B, N)
    y2 = _ssd_branch(p, small, lp["conv_w"], lp["conv_b"], lp["dtb_row"], lp["a_row"], lp["dskip_row"],
                     prm["ssd_norm"][l][None, :], B, N)
    y3 = _diff_branch(p, ktab, qtab, bd32, prm["diff_lambda"][l], prm["diff_norm"][l][None, :], lambda_init, B, N)
    merged = _merge(h, (y0, y1, y2, y3), lp["wg"], lp["wb"])
    return _out_proj(x, merged, lp["wo"], prm["norm_ffn"][l][None, :], lp["wr"])


def _model(x_groups, prm, L):
    dims = [(x.shape[0], x.shape[1]) for x in x_groups]
    xs = [x.reshape(B * N, D_MODEL) for x, (B, N) in zip(x_groups, dims)]
    tabs = [_rope_tables(N) + _alibi_tables(N) for _, N in dims]
    for l in range(DEPTH):
        mixed = [_mixer_layer(x, B, N, l, L[l], prm, tb) for x, (B, N), tb in zip(xs, dims, tabs)]
        h2cat, rcat = (jnp.concatenate([m[i] for m in mixed], axis=0) for i in (1, 2))
        xs = _moe([m[0] for m in mixed], h2cat, rcat, L[l]["w1"], L[l]["w3"], L[l]["w2"],
                  prm["norm_final"][None, :], final=(l == DEPTH - 1))
    return tuple(x.reshape(B, N, D_MODEL) for x, (B, N) in zip(xs, dims))


def kernel(x_prompt, x_sample, norm_mix, w_in, mlstm_gate_bias, mlstm_norm, gqa_q_norm, gqa_k_norm, ssd_conv_w, ssd_conv_b, ssd_dt_bias, ssd_a_log, ssd_d, ssd_norm, diff_lambda, diff_norm, w_branch, w_gate, w_out, norm_ffn, w_router_group, w_router_expert, moe_w_gate, moe_w_up, moe_w_down, norm_final):
    prm = dict(norm_mix=norm_mix, mlstm_norm=mlstm_norm, ssd_norm=ssd_norm, diff_lambda=diff_lambda,
               diff_norm=diff_norm, norm_ffn=norm_ffn, norm_final=norm_final)
    layers = []
    for l in range(DEPTH):
        lp = _prep_layer(l, w_in, mlstm_gate_bias, ssd_conv_w, ssd_conv_b, ssd_dt_bias, ssd_a_log, ssd_d,
                         gqa_q_norm, gqa_k_norm, w_router_group, w_router_expert)
        lp["wg"] = w_gate[l].astype(BF16)
        lp["wb"] = w_branch[l].astype(BF16)
        lp["wo"] = w_out[l].astype(BF16)
        lp["w1"] = moe_w_gate[l]
        lp["w3"] = moe_w_up[l]
        lp["w2"] = moe_w_down[l]
        layers.append(lp)
    return _model([x_prompt, x_sample], prm, layers)
```

```python
import functools
import math

import jax
import jax.numpy as jnp
import numpy as np
from jax import lax
from jax.experimental import pallas as pl
from jax.experimental.pallas import tpu as pltpu

F32 = jnp.float32
BF16 = jnp.bfloat16

D_MODEL = 1024
DEPTH = 2
GRID_W = 64
BRANCH_W = 512
NORM_EPS = 1e-6
MLSTM_HEADS = 4
MLSTM_HD = 128
GQA_HD = 64
GQA_HEADS = 8
GQA_KV_HEADS = 2
ROPE_THETA = 10000.0
SSD_HD = 64
SSD_HEADS = 8
SSD_GROUPS = 2
SSD_STATE = 128
SSD_CONV = 5
DIFF_HD = 64
DIFF_HEADS = 4
MOE_GROUPS = 4
MOE_EPG = 8
MOE_EXPERTS = 32
MOE_FF = 512
CHUNK = 128
LOG2E = 1.4426950408889634

_MLSTM_COLS = 4 * BRANCH_W + 4 * MLSTM_HEADS
_GQA_COLS = GQA_HEADS * GQA_HD + 2 * GQA_KV_HEADS * GQA_HD
_SSD_XBC = BRANCH_W + 2 * SSD_GROUPS * SSD_STATE
_SSD_COLS = BRANCH_W + _SSD_XBC + 2 * SSD_HEADS
_O0 = _MLSTM_COLS
_O1 = _O0 + _GQA_COLS
_O2 = _O1 + _SSD_COLS

ML_Q, ML_K, ML_V, ML_O = 0, 512, 1024, 1536
GQ_Q = 2048
SS_Z = 2560
DF_Q, DF_K, DF_V = 3072, 3584, 4096
SS_X, SS_BC = 4608, 5120
GQ_KV = 5632
P_COLS = 5888
P_PAD = 6144
SMALL_COLS = 128

LANE = 128
VMEM_LIMIT = 48 * 1024 * 1024

MOE_BLK = 256


def _cparams(sem):
    return pltpu.CompilerParams(dimension_semantics=sem, vmem_limit_bytes=VMEM_LIMIT)


def _dot(a, b):
    return jnp.dot(a, b, preferred_element_type=F32)


def _dot_nt(a, b):
    return lax.dot_general(a, b, (((1,), (1,)), ((), ())), preferred_element_type=F32)


def _dot_tn(a, b):
    return lax.dot_general(a, b, (((0,), (0,)), ((), ())), preferred_element_type=F32)


def _dot_hi(a, b):
    return jnp.dot(a, b, preferred_element_type=F32, precision=lax.Precision.HIGHEST)


def _sigmoid(x):
    return 1.0 / (1.0 + jnp.exp(-x))


def _silu(x):
    return x * _sigmoid(x)


def _log_sigmoid(x):
    return jnp.minimum(x, 0.0) - jnp.log(1.0 + jnp.exp(-jnp.abs(x)))


def _softplus(x):
    return jnp.maximum(x, 0.0) + jnp.log(1.0 + jnp.exp(-jnp.abs(x)))


def _main_col_index():
    gq = np.arange(GQA_HEADS * GQA_HD).reshape(GQA_HEADS, GQA_HD // 2, 2)
    gq = np.concatenate([gq[..., 0], gq[..., 1]], axis=-1).reshape(-1)
    gk = np.arange(GQA_KV_HEADS * GQA_HD).reshape(GQA_KV_HEADS, GQA_HD // 2, 2)
    gk = np.concatenate([gk[..., 0], gk[..., 1]], axis=-1).reshape(-1)
    segs = [
        np.arange(0, 4 * BRANCH_W),
        _O0 + gq,
        _O1 + np.arange(0, BRANCH_W),
        _O2 + np.arange(0, 3 * BRANCH_W),
        _O1 + BRANCH_W + np.arange(0, _SSD_XBC),
        _O0 + GQA_HEADS * GQA_HD + gk,
        _O0 + GQA_HEADS * GQA_HD + GQA_KV_HEADS * GQA_HD + np.arange(GQA_KV_HEADS * GQA_HD),
    ]
    idx = np.concatenate(segs)
    assert idx.shape[0] == P_COLS
    return idx


def _small_col_index():
    return np.concatenate([4 * BRANCH_W + np.arange(4 * MLSTM_HEADS),
                           _O1 + BRANCH_W + _SSD_XBC + np.arange(2 * SSD_HEADS)])


_MAIN_IDX = _main_col_index()
_SMALL_IDX = _small_col_index()
_DEINT64 = np.concatenate([np.arange(0, GQA_HD, 2), np.arange(1, GQA_HD, 2)])


def _inproj_kernel(x_ref, g_ref, w_ref, ws_ref, p_ref, h_ref, s_ref, h_scr):
    @pl.when(pl.program_id(1) == 0)
    def _():
        x = x_ref[...]
        ms = jnp.mean(x * x, axis=-1, keepdims=True)
        h = (x * lax.rsqrt(ms + NORM_EPS) * g_ref[...]).astype(BF16)
        h_scr[...] = h
        h_ref[...] = h
        s_ref[...] = _dot(h, ws_ref[...])

    p_ref[...] = _dot(h_scr[...], w_ref[...]).astype(BF16)


def _in_proj(x, g, w_main, w_small, tm=1024, tn=1024):
    T = x.shape[0]
    return pl.pallas_call(
        _inproj_kernel,
        grid=(T // tm, P_PAD // tn),
        in_specs=[
            pl.BlockSpec((tm, D_MODEL), lambda i, j: (i, 0)),
            pl.BlockSpec((1, D_MODEL), lambda i, j: (0, 0)),
            pl.BlockSpec((D_MODEL, tn), lambda i, j: (0, j)),
            pl.BlockSpec((D_MODEL, SMALL_COLS), lambda i, j: (0, 0)),
        ],
        out_specs=[
            pl.BlockSpec((tm, tn), lambda i, j: (i, j)),
            pl.BlockSpec((tm, D_MODEL), lambda i, j: (i, 0)),
            pl.BlockSpec((tm, SMALL_COLS), lambda i, j: (i, 0)),
        ],
        out_shape=[
            jax.ShapeDtypeStruct((T, P_PAD), BF16),
            jax.ShapeDtypeStruct((T, D_MODEL), BF16),
            jax.ShapeDtypeStruct((T, SMALL_COLS), F32),
        ],
        scratch_shapes=[pltpu.VMEM((tm, D_MODEL), BF16)],
        compiler_params=_cparams(("parallel", "arbitrary")),
        name="in_proj",
    )(x, g, w_main, w_small)


def _tri_masks(L, reverse):
    r = lax.broadcasted_iota(jnp.int32, (L, L), 0)
    c = lax.broadcasted_iota(jnp.int32, (L, L), 1)
    return (c >= r) if reverse else (c <= r)


def _mlstm_kernel(*refs, reverse, final):
    q_ref = refs[0]
    ct_scr, n_scr, m_scr = refs[-3:]

    @pl.when(pl.program_id(1) == 0)
    def _():
        ct_scr[...] = jnp.zeros_like(ct_scr)
        n_scr[...] = jnp.zeros_like(n_scr)
        m_scr[...] = jnp.zeros_like(m_scr)

    _mlstm_step(refs, reverse, final)


def _mlstm_step(refs, reverse, final):
    if final:
        (q_ref, k_ref, v_ref, sm_ref, gb_ref, o_ref, hb_ref, ng_ref,
         out_ref, ct_scr, n_scr, m_scr) = refs
    else:
        q_ref, k_ref, v_ref, sm_ref, gb_ref, out_ref, ct_scr, n_scr, m_scr = refs
    L = CHUNK
    H = MLSTM_HEADS
    d = MLSTM_HD
    nbat = q_ref.shape[0]
    pairs = [(bb, h) for bb in range(nbat) for h in range(H)]
    mask = _tri_masks(L, reverse)
    tri = mask.astype(F32)
    i_off = 8 if reverse else 0
    f_off = i_off + 4
    edge = 0 if reverse else L - 1
    scale = d ** -0.5
    lane = lax.broadcasted_iota(jnp.int32, (1, LANE), 1)

    gates = [sm_ref[bb] + gb_ref[...] for bb in range(nbat)]
    bcum = [_dot_hi(tri, _log_sigmoid(g)) for g in gates]
    gates_t = [g.T for g in gates]
    bcum_t = [b.T for b in bcum]
    m_all = [m_scr[bb] for bb in range(nbat)]

    def cols(h):
        return slice(h * d, (h + 1) * d)

    b_col, i_col, m_prev, m_t, w_intra, w_inter = {}, {}, {}, {}, {}, {}
    for pr in pairs:
        bb, h = pr
        b_col[pr] = bcum[bb][:, f_off + h:f_off + h + 1]
        i_col[pr] = gates[bb][:, i_off + h:i_off + h + 1]
        a_row = gates_t[bb][i_off + h:i_off + h + 1, :] - bcum_t[bb][f_off + h:f_off + h + 1, :]
        m_prev[pr] = m_all[bb][:, h:h + 1]
        dmat = jnp.where(mask, b_col[pr] + a_row, -jnp.inf)
        inter = b_col[pr] + m_prev[pr]
        m_t[pr] = jnp.maximum(inter, jnp.max(dmat, axis=1, keepdims=True))
        w_intra[pr] = jnp.exp(dmat - m_t[pr])
        w_inter[pr] = jnp.exp(inter - m_t[pr])

    ks, s = {}, {}
    for pr in pairs:
        bb, h = pr
        ks[pr] = (k_ref[bb, :, cols(h)].astype(F32) * scale).astype(BF16)
        s[pr] = _dot_nt(q_ref[bb, :, cols(h)], ks[pr]) * w_intra[pr]

    hs = {}
    for pr in pairs:
        bb, h = pr
        qh = q_ref[bb, :, cols(h)]
        ct = ct_scr[bb * H + h]
        n_row = n_scr[bb * H + h]
        num = _dot(s[pr].astype(BF16), v_ref[bb, :, cols(h)]) + w_inter[pr] * _dot(qh, ct.astype(BF16))
        qn = jnp.sum(qh.astype(F32) * n_row, axis=1, keepdims=True)
        den = jnp.sum(s[pr], axis=1, keepdims=True) + w_inter[pr] * qn
        hs[pr] = num / jnp.maximum(jnp.abs(den), jnp.exp(-m_t[pr]))

    for pr in pairs:
        bb, h = pr
        vh = v_ref[bb, :, cols(h)]
        btot = b_col[pr][edge:edge + 1, :]
        dec = btot - b_col[pr] + i_col[pr]
        m_new = jnp.maximum(btot + m_prev[pr], jnp.max(dec, axis=0, keepdims=True))
        ws = jnp.exp(dec - m_new)
        wc = jnp.exp(btot + m_prev[pr] - m_new)
        wsv = (ws * vh.astype(F32)).astype(BF16)
        ct_scr[bb * H + h] = wc * ct_scr[bb * H + h] + _dot_tn(ks[pr], wsv)
        n_scr[bb * H + h] = wc * n_scr[bb * H + h] + jnp.sum(ws * ks[pr].astype(F32), axis=0, keepdims=True)
        m_all[bb] = jnp.where(lane == h, m_new, m_all[bb])
    for bb in range(nbat):
        m_scr[bb] = m_all[bb]

    for pr in pairs:
        bb, h = pr
        if not final:
            out_ref[bb, :, cols(h)] = hs[pr]
        else:
            hsum = hs[pr] + hb_ref[bb, :, cols(h)]
            ms = jnp.mean(hsum * hsum, axis=1, keepdims=True)
            y = hsum * lax.rsqrt(ms + NORM_EPS) * ng_ref[:, cols(h)]
            o = o_ref[bb, :, cols(h)].astype(F32)
            out_ref[bb, :, cols(h)] = (_sigmoid(o) * y).astype(BF16)


SCAN_NBAT = 2


def _mlstm_dir(p3, small3, gate_bias_row, B, N, reverse, hb=None, norm_g=None):
    L = CHUNK
    nc = N // L
    nbat = SCAN_NBAT
    final = hb is not None

    def crow(c):
        return (nc - 1 - c) if reverse else c

    W = BRANCH_W
    in_specs = [
        pl.BlockSpec((nbat, L, W), lambda b, c: (b, crow(c), ML_Q // W)),
        pl.BlockSpec((nbat, L, W), lambda b, c: (b, crow(c), ML_K // W)),
        pl.BlockSpec((nbat, L, W), lambda b, c: (b, crow(c), ML_V // W)),
        pl.BlockSpec((nbat, L, SMALL_COLS), lambda b, c: (b, crow(c), 0)),
        pl.BlockSpec((1, SMALL_COLS), lambda b, c: (0, 0)),
    ]
    args = [p3, p3, p3, small3, gate_bias_row]
    if final:
        in_specs += [
            pl.BlockSpec((nbat, L, W), lambda b, c: (b, crow(c), ML_O // W)),
            pl.BlockSpec((nbat, L, W), lambda b, c: (b, crow(c), 0)),
            pl.BlockSpec((1, W), lambda b, c: (0, 0)),
        ]
        args += [p3, hb, norm_g]
    return pl.pallas_call(
        functools.partial(_mlstm_kernel, reverse=reverse, final=final),
        grid=(B // nbat, nc),
        in_specs=in_specs,
        out_specs=pl.BlockSpec((nbat, L, W), lambda b, c: (b, crow(c), 0)),
        out_shape=jax.ShapeDtypeStruct((B, N, W), BF16 if final else F32),
        scratch_shapes=[
            pltpu.VMEM((nbat * MLSTM_HEADS, MLSTM_HD, MLSTM_HD), F32),
            pltpu.VMEM((nbat * MLSTM_HEADS, 1, MLSTM_HD), F32),
            pltpu.VMEM((nbat, 1, LANE), F32),
        ],
        compiler_params=_cparams(("parallel", "arbitrary")),
        name="mlstm_bwd" if reverse else "mlstm_fwd",
    )(*args)


def _mlstm_branch(p, small, gate_bias_row, norm_g, B, N):
    p3 = p.reshape(B, N, P_PAD)
    small3 = small.reshape(B, N, SMALL_COLS)
    hb = _mlstm_dir(p3, small3, gate_bias_row, B, N, reverse=True)
    y = _mlstm_dir(p3, small3, gate_bias_row, B, N, reverse=False, hb=hb, norm_g=norm_g)
    return y.reshape(B * N, BRANCH_W)


def _gqa_prep_kernel(q_ref, kv_ref, cos_ref, sin_ref, qg_ref, kg_ref, bd_ref, qo_ref, ko_ref, vo_ref):
    tm = q_ref.shape[0]
    bd = bd_ref[...]
    cos = cos_ref[...]
    sin = sin_ref[...]
    lane = lax.broadcasted_iota(jnp.int32, (tm, LANE), 1)
    first = (lane % GQA_HD) < (GQA_HD // 2)
    lo = lane < GQA_HD

    def norm_rope(x, g):
        ss = _dot((x * x).astype(BF16), bd)
        xn = x * lax.rsqrt(ss * (1.0 / GQA_HD) + NORM_EPS) * g
        partner = jnp.where(first, pltpu.roll(xn, LANE - GQA_HD // 2, axis=1),
                            pltpu.roll(xn, GQA_HD // 2, axis=1))
        return xn * cos + partner * sin

    qscale = (GQA_HD ** -0.5) * LOG2E
    for a in range(GQA_HEADS * GQA_HD // LANE):
        x = q_ref[:, a * LANE:(a + 1) * LANE].astype(F32)
        qo_ref[:, a * LANE:(a + 1) * LANE] = (norm_rope(x, qg_ref[...]) * qscale).astype(BF16)
    k = norm_rope(kv_ref[:, 0:LANE].astype(F32), kg_ref[...])
    k_sw = pltpu.roll(k, GQA_HD, axis=1)
    ko_ref[0] = jnp.where(lo, k, k_sw).astype(BF16)
    ko_ref[1] = jnp.where(lo, k_sw, k).astype(BF16)
    v = kv_ref[:, LANE:2 * LANE].astype(F32)
    v_sw = pltpu.roll(v, GQA_HD, axis=1)
    vo_ref[0] = jnp.where(lo, v, 1.0).astype(BF16)
    vo_ref[1] = jnp.where(lo, v_sw, 1.0).astype(BF16)


def _gqa_prep(p, cos_t, sin_t, qg, kg, bd, B, N, tm=512):
    T = B * N
    nb = N // tm
    return pl.pallas_call(
        _gqa_prep_kernel,
        grid=(T // tm,),
        in_specs=[
            pl.BlockSpec((tm, 512), lambda i: (i, GQ_Q // 512)),
            pl.BlockSpec((tm, 256), lambda i: (i, GQ_KV // 256)),
            pl.BlockSpec((tm, LANE), lambda i: (i % nb, 0)),
            pl.BlockSpec((tm, LANE), lambda i: (i % nb, 0)),
            pl.BlockSpec((1, LANE), lambda i: (0, 0)),
            pl.BlockSpec((1, LANE), lambda i: (0, 0)),
            pl.BlockSpec((LANE, LANE), lambda i: (0, 0)),
        ],
        out_specs=[
            pl.BlockSpec((tm, 512), lambda i: (i, 0)),
            pl.BlockSpec((GQA_KV_HEADS, tm, LANE), lambda i: (0, i, 0)),
            pl.BlockSpec((GQA_KV_HEADS, tm, LANE), lambda i: (0, i, 0)),
        ],
        out_shape=[
            jax.ShapeDtypeStruct((T, 512), BF16),
            jax.ShapeDtypeStruct((GQA_KV_HEADS, T, LANE), BF16),
            jax.ShapeDtypeStruct((GQA_KV_HEADS, T, LANE), BF16),
        ],
        compiler_params=_cparams(("parallel",)),
        name="gqa_prep",
    )(p, p, cos_t, sin_t, qg, kg, bd)


FLASH_TKB = 512


def _softmax_keys(s_ref, p_ref, m_scr, al_scr, l_scr, M):
    tkb = s_ref.shape[0]
    for c in range(M // LANE):
        cols = slice(c * LANE, (c + 1) * LANE)
        mx = jnp.max(jnp.max(s_ref[:, cols].reshape(tkb // 8, 8, LANE), axis=0), axis=0, keepdims=True)
        m_prev = m_scr[:, cols]
        m_new = jnp.maximum(m_prev, mx)
        alpha = jnp.exp2(m_prev - m_new)
        p = jnp.exp2(s_ref[:, cols] - m_new)
        p_ref[:, cols] = p.astype(BF16)
        if l_scr is not None:
            lsum = jnp.sum(jnp.sum(p.reshape(tkb // 8, 8, LANE), axis=0), axis=0, keepdims=True)
            l_scr[:, cols] = alpha * l_scr[:, cols] + lsum
        al_scr[:, cols] = alpha
        m_scr[:, cols] = m_new


def _flash_blocks(nb, qk, pv, softmax, fix=None):
    qk(0)
    for i in range(nb):
        if i + 1 < nb:
            qk(i + 1)
        if fix is not None:
            fix(i)
        softmax(i)
        pv(i)


FLASH_SAFE_LOG2 = 100.0


def _flash_one_pass(nb, scores, values, m_scr, l_scr, acc_scr, p_scr):
    for i in range(nb):
        s = scores(i)
        tkb, M = s.shape
        m_used = m_scr[...]
        p = jnp.exp2(s - m_used)
        p_scr[i % 2] = p.astype(BF16)
        mx = jnp.max(jnp.max(s.reshape(tkb // 8, 8, M), axis=0), axis=0, keepdims=True)
        m_new = jnp.maximum(m_used, mx)
        alpha = jnp.exp2(m_used - m_new)
        if l_scr is not None:
            lsum = jnp.sum(jnp.sum(p.reshape(tkb // 8, 8, M), axis=0), axis=0, keepdims=True)
            l_scr[...] = alpha * (l_scr[...] + lsum)
        acc_scr[...] = alpha * (acc_scr[...] + _dot_tn(values(i), p_scr[i % 2]))
        m_scr[...] = m_new


def _flash_score_bound(qs, kmax):
    qf = qs.astype(F32)
    n2 = _dot_nt(jnp.ones((8, LANE), BF16), (qf * qf).astype(BF16))[0:1, :]
    return jnp.sqrt(n2) * kmax * 1.01 + 1.0


def _flash_is_safe(u_scr, m_scr):
    return jnp.where(jnp.max(u_scr[...] - m_scr[...]) <= FLASH_SAFE_LOG2, 1, 0)


def _gqa_flash_kernel(q_ref, k_ref, v_ref, kmax_ref, o_ref, qs_scr, m_scr, al_scr, acc_scr, s_scr, p_scr,
                      u_scr, safe_ref):
    tq = q_ref.shape[0]
    M = 4 * tq
    tkb = s_scr.shape[1]
    nb = k_ref.shape[0] // tkb
    kj = pl.program_id(3)

    @pl.when(kj == 0)
    def _():
        lane = lax.broadcasted_iota(jnp.int32, (tq, LANE), 1)
        lo = lane < GQA_HD
        zero = jnp.zeros((tq, LANE), BF16)
        for a in range(2):
            qa = q_ref[:, a * LANE:(a + 1) * LANE]
            qs_scr[(2 * a) * tq:(2 * a + 1) * tq, :] = jnp.where(lo, qa, zero)
            qs_scr[(2 * a + 1) * tq:(2 * a + 2) * tq, :] = jnp.where(lo, zero, qa)
        acc_scr[...] = jnp.zeros_like(acc_scr)
        u = _flash_score_bound(qs_scr[...], kmax_ref[:, 0:1])
        u_scr[...] = u
        early = jnp.max(u) <= 0.5 * FLASH_SAFE_LOG2
        safe_ref[0] = jnp.where(early, 2, 0)
        m_scr[...] = jnp.where(early, -u, jnp.full_like(u, -jnp.inf))

    def scores(i):
        return _dot_nt(k_ref[i * tkb:(i + 1) * tkb, :], qs_scr[...])

    def values(i):
        return v_ref[i * tkb:(i + 1) * tkb, :]

    def qk(i):
        s_scr[i % 2] = scores(i)

    def softmax(i):
        _softmax_keys(s_scr.at[i % 2], p_scr.at[i % 2], m_scr, al_scr, None, M)

    def pv(i):
        acc_scr[...] = al_scr[...] * acc_scr[...] + _dot_tn(values(i), p_scr[i % 2])

    one_pass = safe_ref[0] >= 1

    @pl.when(jnp.logical_not(one_pass))
    def _():
        _flash_blocks(nb, qk, pv, softmax)

    @pl.when(one_pass)
    def _():
        _flash_one_pass(nb, scores, values, m_scr, None, acc_scr, p_scr)

    @pl.when(kj == 0)
    def _():
        safe_ref[0] = jnp.maximum(safe_ref[0], _flash_is_safe(u_scr, m_scr))

    @pl.when(kj == pl.num_programs(3) - 1)
    def _():
        inv = 1.0 / acc_scr[GQA_HD:GQA_HD + 1, :]
        ot = jnp.concatenate([acc_scr[0:GQA_HD, h * tq:(h + 1) * tq] * inv[:, h * tq:(h + 1) * tq]
                              for h in range(4)], axis=0)
        o_ref[...] = ot.T.astype(BF16)


def _gqa_flash(qn, k2, va, B, N, tq=256, tk=4096):
    T = B * N
    tk = min(tk, N)
    nq, nk = N // tq, N // tk
    M = 4 * tq
    tkb = FLASH_TKB
    kf = k2[:, :, 0:GQA_HD].astype(F32).reshape(GQA_KV_HEADS, B, N, GQA_HD)
    kmax = jnp.max(jnp.sqrt(jnp.sum(kf * kf, axis=-1)), axis=-1)
    kmax = jnp.broadcast_to(kmax.T.reshape(B * GQA_KV_HEADS, 1, 1), (B * GQA_KV_HEADS, 1, LANE))
    return pl.pallas_call(
        _gqa_flash_kernel,
        grid=(B, GQA_KV_HEADS, nq, nk),
        in_specs=[
            pl.BlockSpec((tq, 256), lambda b, g, i, j: (b * nq + i, g)),
            pl.BlockSpec((None, tk, LANE), lambda b, g, i, j: (g, b * nk + j, 0)),
            pl.BlockSpec((None, tk, LANE), lambda b, g, i, j: (g, b * nk + j, 0)),
            pl.BlockSpec((None, 1, LANE), lambda b, g, i, j: (b * GQA_KV_HEADS + g, 0, 0)),
        ],
        out_specs=pl.BlockSpec((tq, 256), lambda b, g, i, j: (b * nq + i, g)),
        out_shape=jax.ShapeDtypeStruct((T, 512), BF16),
        scratch_shapes=[
            pltpu.VMEM((M, LANE), BF16),
            pltpu.VMEM((1, M), F32),
            pltpu.VMEM((1, M), F32),
            pltpu.VMEM((LANE, M), F32),
            pltpu.VMEM((2, tkb, M), F32),
            pltpu.VMEM((2, tkb, M), BF16),
            pltpu.VMEM((1, M), F32),
            pltpu.SMEM((1,), jnp.int32),
        ],
        compiler_params=_cparams(("parallel", "parallel", "parallel", "arbitrary")),
        name="gqa_flash",
    )(qn, k2, va, kmax)


def _rope_tables(N):
    rows = N // GRID_W
    row = jnp.repeat(jnp.arange(rows, dtype=F32), GRID_W)
    col = (jnp.arange(N) % GRID_W).astype(F32)
    half = GQA_HD // 2
    inv = 1.0 / (ROPE_THETA ** (jnp.arange(0, half, 2, dtype=F32) / half))
    ang = jnp.concatenate([row[:, None] * inv, col[:, None] * inv], axis=-1)
    cos, sin = jnp.cos(ang), jnp.sin(ang)
    cos_h = jnp.concatenate([cos, cos], axis=-1)
    sin_h = jnp.concatenate([-sin, sin], axis=-1)
    return jnp.tile(cos_h, (1, LANE // GQA_HD)), jnp.tile(sin_h, (1, LANE // GQA_HD))


def _gqa_branch(p, cos_t, sin_t, qg, kg, bd, B, N):
    qn, k2, va = _gqa_prep(p, cos_t, sin_t, qg, kg, bd, B, N)
    return _gqa_flash(qn, k2, va, B, N)


DIFF_SKIP_LOG2 = 40.0


def _alibi_slope_log2(h):
    return (2.0 ** (-8.0 * (h + 1) / DIFF_HEADS)) * LOG2E


def _alibi_tables(N):
    pos = np.arange(N)
    hi = ((pos // 128) * 128).astype(np.float32)
    lo = (pos % 128).astype(np.float32)
    kt = np.zeros((DIFF_HEADS, N, LANE), np.float32)
    qt = np.zeros((DIFF_HEADS, N, LANE), np.float32)
    for h in range(DIFF_HEADS):
        s = np.float32(_alibi_slope_log2(h))
        s1 = np.float32(s).astype(BF16).astype(np.float32)
        s2 = np.float32(s - s1).astype(BF16).astype(np.float32)
        for c, val in enumerate((hi, hi, lo, lo)):
            kt[h, :, c] = val
            qt[h, :, 4 + c] = -val
        for c, val in enumerate((s1, s2, s1, s2)):
            kt[h, :, 4 + c] = val
            qt[h, :, c] = val
    return (jnp.asarray(kt.reshape(DIFF_HEADS * N, LANE), BF16), jnp.asarray(qt.reshape(DIFF_HEADS * N, LANE), BF16))


def _diff_flash_kernel(lo_ref, hi_ref, q_ref, qa_ref, k_ref, ka_ref, v_ref, kmax_ref, lp_ref, ng_ref, o_ref,
                       qs_scr, m_scr, l_scr, al_scr, acc_scr, s_scr, p_scr, u_scr, safe_ref, *, lambda_init):
    tq = q_ref.shape[0]
    tk = k_ref.shape[0]
    M = 2 * tq
    tkb = s_scr.shape[1]
    nb = tk // tkb
    b, h, qi, kj = pl.program_id(0), pl.program_id(1), pl.program_id(2), pl.program_id(3)
    idx = (b * pl.num_programs(1) + h) * pl.num_programs(2) + qi
    first, last = lo_ref[idx], hi_ref[idx]

    @pl.when(kj == first)
    def _():
        lane = lax.broadcasted_iota(jnp.int32, (tq, LANE), 1)
        lo = lane < DIFF_HD
        zero = jnp.zeros((tq, LANE), BF16)
        q = q_ref[...]
        qa = qa_ref[...]
        for var, aug in enumerate((qa, -qa)):
            qs_scr[var, 0:tq, 0:LANE] = jnp.where(lo, q, zero)
            qs_scr[var, tq:M, 0:LANE] = jnp.where(lo, zero, q)
            qs_scr[var, 0:tq, LANE:2 * LANE] = aug
            qs_scr[var, tq:M, LANE:2 * LANE] = aug
        m_scr[...] = jnp.full_like(m_scr, -jnp.inf)
        l_scr[...] = jnp.zeros_like(l_scr)
        acc_scr[...] = jnp.zeros_like(acc_scr)
        u_scr[...] = _flash_score_bound(qs_scr[0, :, 0:LANE], kmax_ref[:, 0:1])
        safe_ref[0] = 0

    is_left = (kj + 1) * tk <= qi * tq
    is_right = kj * tk >= (qi + 1) * tq
    pure = jnp.logical_or(is_left, is_right)
    active = jnp.logical_and(kj >= first, kj <= last)
    sel = jnp.where(is_right, 1, 0)

    def scores(i):
        rows = slice(i * tkb, (i + 1) * tkb)
        kk = jnp.concatenate([k_ref[rows, :], ka_ref[rows, :]], axis=1)
        return _dot_nt(kk, qs_scr[sel])

    def values(i):
        return v_ref[i * tkb:(i + 1) * tkb, :]

    def qk(i):
        s_scr[i % 2] = scores(i)

    def fix(i):
        slope = jnp.exp2(-2.0 * (jnp.full((1, 1), h, jnp.int32).astype(F32) + 1.0)) * LOG2E
        j = kj * tk + i * tkb + lax.broadcasted_iota(jnp.int32, (tkb, 1), 0)
        col = lax.broadcasted_iota(jnp.int32, (1, M), 1)
        iq = qi * tq + col - jnp.where(col >= tq, tq, 0)
        d = jnp.maximum(j - iq, 0).astype(F32)
        s_scr[i % 2] = s_scr[i % 2] - (2.0 * slope) * d

    def softmax(i):
        _softmax_keys(s_scr.at[i % 2], p_scr.at[i % 2], m_scr, al_scr, l_scr, M)

    def pv(i):
        acc_scr[...] = al_scr[...] * acc_scr[...] + _dot_tn(values(i), p_scr[i % 2])

    @pl.when(jnp.logical_and(active, pure))
    def _():
        one_pass = jnp.logical_and(safe_ref[0] == 1, kj != first)

        @pl.when(jnp.logical_not(one_pass))
        def _():
            _flash_blocks(nb, qk, pv, softmax)

        @pl.when(one_pass)
        def _():
            _flash_one_pass(nb, scores, values, m_scr, l_scr, acc_scr, p_scr)

    @pl.when(jnp.logical_and(active, jnp.logical_not(pure)))
    def _():
        _flash_blocks(nb, qk, pv, softmax, fix=fix)

    @pl.when(kj == first)
    def _():
        safe_ref[0] = _flash_is_safe(u_scr, m_scr)

    @pl.when(kj == last)
    def _():
        lp = lp_ref[...]
        s01 = jnp.sum(jnp.sum(lp[0:1] * lp[1:2], axis=1, keepdims=True), axis=0, keepdims=True)
        s23 = jnp.sum(jnp.sum(lp[2:3] * lp[3:4], axis=1, keepdims=True), axis=0, keepdims=True)
        lam = jnp.exp(s01) - jnp.exp(s23) + lambda_init
        inv = 1.0 / l_scr[...]
        ot = acc_scr[:, 0:tq] * inv[:, 0:tq] - lam * (acc_scr[:, tq:M] * inv[:, tq:M])
        o = ot.T
        ms = jnp.mean(o * o, axis=1, keepdims=True)
        o_ref[...] = (o * lax.rsqrt(ms + NORM_EPS) * ng_ref[...] * (1.0 - lambda_init)).astype(BF16)


def _diff_prep_kernel(q_ref, k_ref, v_ref, bd_ref, ko_ref, vo_ref, nrm_ref):
    tm = q_ref.shape[0]
    lane = lax.broadcasted_iota(jnp.int32, (tm, LANE), 1)
    nrm = jnp.zeros((tm, LANE), F32)
    for h in range(DIFF_HEADS):
        cols = slice(h * LANE, (h + 1) * LANE)
        ko_ref[h] = k_ref[:, cols]
        vo_ref[h] = v_ref[:, cols]
        for src, off in ((q_ref, 0), (k_ref, DIFF_HEADS)):
            x = src[:, cols].astype(F32)
            ss = _dot_hi(x * x, bd_ref[...])
            n = jnp.sqrt(jnp.maximum(ss, pltpu.roll(ss, DIFF_HD, axis=1)))
            nrm = jnp.where(lane == off + h, n, nrm)
    nrm_ref[...] = nrm


def _diff_prep(p, bd32, B, N, tm=512):
    T = B * N
    W = BRANCH_W
    return pl.pallas_call(
        _diff_prep_kernel,
        grid=(T // tm,),
        in_specs=[
            pl.BlockSpec((tm, W), lambda i: (i, DF_Q // W)),
            pl.BlockSpec((tm, W), lambda i: (i, DF_K // W)),
            pl.BlockSpec((tm, W), lambda i: (i, DF_V // W)),
            pl.BlockSpec((LANE, LANE), lambda i: (0, 0)),
        ],
        out_specs=[
            pl.BlockSpec((DIFF_HEADS, tm, LANE), lambda i: (0, i, 0)),
            pl.BlockSpec((DIFF_HEADS, tm, LANE), lambda i: (0, i, 0)),
            pl.BlockSpec((tm, LANE), lambda i: (i, 0)),
        ],
        out_shape=[
            jax.ShapeDtypeStruct((DIFF_HEADS, T, LANE), BF16),
            jax.ShapeDtypeStruct((DIFF_HEADS, T, LANE), BF16),
            jax.ShapeDtypeStruct((T, LANE), F32),
        ],
        compiler_params=_cparams(("parallel",)),
        name="diff_prep",
    )(p, p, p, bd32)


def _diff_bands(nrm, B, N, tq, tk):
    nq, nkt = N // tq, N // tk
    qn = nrm[:, 0:DIFF_HEADS].reshape(B, N, DIFF_HEADS)
    kn = nrm[:, DIFF_HEADS:2 * DIFF_HEADS].reshape(B, N, DIFF_HEADS)
    qmax = jnp.max(qn.reshape(B, nq, tq, DIFF_HEADS), axis=2)
    kself = jnp.max(kn.reshape(B, nq, tq, DIFF_HEADS), axis=2)
    kmax = jnp.max(kn.reshape(B, nkt, tk, DIFF_HEADS), axis=2)
    bound = qmax[:, :, None, :] * (kmax[:, None, :, :] + kself[:, :, None, :]) * 1.001 + DIFF_SKIP_LOG2
    q0 = np.arange(nq)[:, None] * tq
    k0 = np.arange(nkt)[None, :] * tk
    dist = np.maximum(0, np.maximum(q0 - (k0 + tk - 1), k0 - (q0 + tq - 1))).astype(np.float32)
    slopes = np.array([_alibi_slope_log2(h) for h in range(DIFF_HEADS)], np.float32)
    keep = bound > jnp.asarray(dist)[None, :, :, None] * jnp.asarray(slopes)[None, None, None, :]
    keep = jnp.transpose(keep, (0, 3, 1, 2))
    first = jnp.argmax(keep, axis=-1).astype(jnp.int32)
    last = (nkt - 1 - jnp.argmax(keep[..., ::-1], axis=-1)).astype(jnp.int32)
    return first.reshape(-1), last.reshape(-1)


def _diff_branch(p, ktab, qtab, bd32, lam_params, norm_g, lambda_init, B, N, tq=512, tk=2048):
    T = B * N
    nq, nk = N // tq, N // tk
    M = 2 * tq
    tkb = FLASH_TKB
    kd, vd, nrm = _diff_prep(p, bd32, B, N)
    first, last = _diff_bands(nrm, B, N, tq, tk)
    kmax = jnp.max(nrm[:, DIFF_HEADS:2 * DIFF_HEADS].reshape(B, N, DIFF_HEADS), axis=1)
    kmax = jnp.broadcast_to(kmax.reshape(B * DIFF_HEADS, 1, 1), (B * DIFF_HEADS, 1, LANE))

    def kstep(b, h, i, j, lo, hi):
        idx = (b * DIFF_HEADS + h) * nq + i
        return jnp.minimum(jnp.maximum(j, lo[idx]), hi[idx])

    return pl.pallas_call(
        functools.partial(_diff_flash_kernel, lambda_init=lambda_init),
        grid_spec=pltpu.PrefetchScalarGridSpec(
            num_scalar_prefetch=2,
            grid=(B, DIFF_HEADS, nq, nk),
            in_specs=[
                pl.BlockSpec((tq, LANE), lambda b, h, i, j, lo, hi: (b * nq + i, DF_Q // LANE + h)),
                pl.BlockSpec((tq, LANE), lambda b, h, i, j, lo, hi: (h * nq + i, 0)),
                pl.BlockSpec((None, tk, LANE), lambda b, h, i, j, lo, hi: (h, b * nk + kstep(b, h, i, j, lo, hi), 0)),
                pl.BlockSpec((tk, LANE), lambda b, h, i, j, lo, hi: (h * nk + kstep(b, h, i, j, lo, hi), 0)),
                pl.BlockSpec((None, tk, LANE), lambda b, h, i, j, lo, hi: (h, b * nk + kstep(b, h, i, j, lo, hi), 0)),
                pl.BlockSpec((None, 1, LANE), lambda b, h, i, j, lo, hi: (b * DIFF_HEADS + h, 0, 0)),
                pl.BlockSpec((4, DIFF_HD), lambda b, h, i, j, lo, hi: (0, 0)),
                pl.BlockSpec((1, LANE), lambda b, h, i, j, lo, hi: (0, h)),
            ],
            out_specs=pl.BlockSpec((tq, LANE), lambda b, h, i, j, lo, hi: (b * nq + i, h)),
            scratch_shapes=[
                pltpu.VMEM((2, M, 2 * LANE), BF16),
                pltpu.VMEM((1, M), F32),
                pltpu.VMEM((1, M), F32),
                pltpu.VMEM((1, M), F32),
                pltpu.VMEM((LANE, M), F32),
                pltpu.VMEM((2, tkb, M), F32),
                pltpu.VMEM((2, tkb, M), BF16),
                pltpu.VMEM((1, M), F32),
                pltpu.SMEM((1,), jnp.int32),
            ],
        ),
        out_shape=jax.ShapeDtypeStruct((T, BRANCH_W), BF16),
        compiler_params=_cparams(("parallel", "parallel", "parallel", "arbitrary")),
        name="diff_flash",
    )(first, last, p, qtab, kd, ktab, vd, kmax, lam_params, norm_g)


SSD_HALO = 16


def _ssd_prep_kernel(x_ref, xp_ref, xn_ref, bc_ref, bcp_ref, bcn_ref, sm_ref, cw_ref, cb_ref,
                     dtb_ref, arow_ref, xo_ref, bco_ref, dto_ref):
    tc = x_ref.shape[0]
    c = pl.program_id(1)
    nc = pl.num_programs(1)
    has_prev = jnp.where(c > 0, 1.0, 0.0)
    has_next = jnp.where(c < nc - 1, 1.0, 0.0)
    pad = SSD_CONV // 2

    def conv(main_ref, prev_ref, next_ref, off):
        xf = jnp.concatenate([prev_ref[...].astype(F32) * has_prev, main_ref[...].astype(F32),
                              next_ref[...].astype(F32) * has_next], axis=0)
        n = xf.shape[0]
        acc = jnp.zeros((tc, xf.shape[1]), F32) + cb_ref[:, off:off + xf.shape[1]]
        for j in range(SSD_CONV):
            sh = pltpu.roll(xf, (pad - j) % n, axis=0)[SSD_HALO:SSD_HALO + tc, :]
            acc = acc + sh * cw_ref[j:j + 1, off:off + xf.shape[1]]
        return _silu(acc)

    xo_ref[...] = conv(x_ref, xp_ref, xn_ref, 0).astype(BF16)
    bco_ref[...] = conv(bc_ref, bcp_ref, bcn_ref, BRANCH_W).astype(BF16)
    dt = _softplus(sm_ref[...] + dtb_ref[...])
    a = pltpu.roll(dt * arow_ref[...], LANE - 16, axis=1)
    lane = lax.broadcasted_iota(jnp.int32, (tc, LANE), 1)
    dto_ref[...] = jnp.where(lane < 16, a, jnp.where(lane < 32, dt, 0.0))


def _ssd_prep(p, small, conv_w, conv_b, dtb_row, a_row, B, N, tc=512):
    T = B * N
    nc = N // tc
    hb = tc // SSD_HALO
    W = BRANCH_W

    def main(col):
        return pl.BlockSpec((tc, W), lambda b, c: (b * nc + c, col))

    def prev(col):
        return pl.BlockSpec((SSD_HALO, W), lambda b, c: (jnp.maximum((b * nc + c) * hb - 1, 0), col))

    def nxt(col):
        return pl.BlockSpec((SSD_HALO, W), lambda b, c: (jnp.minimum((b * nc + c + 1) * hb, T // SSD_HALO - 1), col))

    cx, cbc = SS_X // W, SS_BC // W
    return pl.pallas_call(
        _ssd_prep_kernel,
        grid=(B, nc),
        in_specs=[
            main(cx), prev(cx), nxt(cx), main(cbc), prev(cbc), nxt(cbc),
            pl.BlockSpec((tc, SMALL_COLS), lambda b, c: (b * nc + c, 0)),
            pl.BlockSpec((SSD_CONV, 2 * W), lambda b, c: (0, 0)),
            pl.BlockSpec((1, 2 * W), lambda b, c: (0, 0)),
            pl.BlockSpec((1, LANE), lambda b, c: (0, 0)),
            pl.BlockSpec((1, LANE), lambda b, c: (0, 0)),
        ],
        out_specs=[
            pl.BlockSpec((tc, W), lambda b, c: (b * nc + c, 0)),
            pl.BlockSpec((tc, W), lambda b, c: (b * nc + c, 0)),
            pl.BlockSpec((tc, LANE), lambda b, c: (b * nc + c, 0)),
        ],
        out_shape=[
            jax.ShapeDtypeStruct((T, W), BF16),
            jax.ShapeDtypeStruct((T, W), BF16),
            jax.ShapeDtypeStruct((T, LANE), F32),
        ],
        compiler_params=_cparams(("parallel", "parallel")),
        name="ssd_prep",
    )(p, p, p, p, p, p, small, conv_w, conv_b, dtb_row, a_row)


def _ssd_kernel(*refs, reverse, final):
    x_ref = refs[0]
    hs_scr = refs[-1]

    @pl.when(pl.program_id(1) == 0)
    def _():
        hs_scr[...] = jnp.zeros_like(hs_scr)

    for bb in range(x_ref.shape[0]):
        _ssd_chunk(bb, refs, reverse, final)


def _ssd_chunk(bb, refs, reverse, final):
    if final:
        (x_ref, bc_ref, dta_ref, ea_ref, ed_ref, z_ref, yb_ref, dsk_ref, ng_ref,
         out_ref, hs_scr) = refs
    else:
        x_ref, bc_ref, dta_ref, ea_ref, ed_ref, out_ref, hs_scr = refs
    L = CHUNK
    S = SSD_STATE
    W = BRANCH_W
    GW = W // SSD_GROUPS
    mask = _tri_masks(L, reverse)
    tri = mask.astype(F32)
    dta = dta_ref[bb]
    acs = _dot_hi(tri, dta)
    acs_t = acs.T
    acs_e = _dot_hi(acs, ea_ref[...])
    dt_e = _dot_hi(dta, ed_ref[...])
    edge = 0 if reverse else L - 1
    a_off = 8 if reverse else 0
    acs_end = acs_e[edge:edge + 1, :]
    xf = x_ref[bb].astype(F32)
    xdt = xf * dt_e
    xdt_b = xdt.astype(BF16)
    xdend = (xdt * jnp.exp(acs_end - acs_e)).astype(BF16)
    e_acs = jnp.exp(acs_e)
    cdec = jnp.exp(acs_end)
    lane = lax.broadcasted_iota(jnp.int32, (L, LANE), 1)
    lo = lane < SSD_HD
    zero = jnp.zeros((L, LANE), BF16)

    ys = []
    for g in range(SSD_GROUPS):
        bg = bc_ref[bb, :, g * S:(g + 1) * S]
        cg = bc_ref[bb, :, SSD_GROUPS * S + g * S:SSD_GROUPS * S + (g + 1) * S]
        cb = _dot_nt(cg, bg)
        hs = hs_scr[bb * SSD_GROUPS + g]
        y_off = _dot(cg, hs.astype(BF16)) * e_acs[:, g * GW:(g + 1) * GW]
        hs_scr[bb * SSD_GROUPS + g] = (cdec[:, g * GW:(g + 1) * GW] * hs
                                       + _dot_tn(bg, xdend[:, g * GW:(g + 1) * GW]))
        for pr in range(GW // LANE):
            col = g * GW + pr * LANE
            xp = xdt_b[:, col:col + LANE]
            yd = jnp.zeros((L, LANE), F32)
            for hh in range(2):
                hd = (col // SSD_HD) + hh
                a_col = acs[:, a_off + hd:a_off + hd + 1]
                a_row = acs_t[a_off + hd:a_off + hd + 1, :]
                decay = jnp.exp(jnp.where(mask, a_col - a_row, -jnp.inf))
                wm = (cb * decay).astype(BF16)
                xm = jnp.where(lo, xp, zero) if hh == 0 else jnp.where(lo, zero, xp)
                yd = yd + _dot(wm, xm)
            ys.append(yd + y_off[:, pr * LANE:(pr + 1) * LANE])
    y = jnp.concatenate(ys, axis=1)

    if not final:
        out_ref[bb] = y
    else:
        y = y + yb_ref[bb] + dsk_ref[...] * xf
        y = y * _silu(z_ref[bb].astype(F32))
        outs = []
        for g in range(SSD_GROUPS):
            yg = y[:, g * GW:(g + 1) * GW]
            ms = jnp.mean(yg * yg, axis=1, keepdims=True)
            outs.append(yg * lax.rsqrt(ms + NORM_EPS))
        out_ref[bb] = (jnp.concatenate(outs, axis=1) * ng_ref[...]).astype(BF16)


def _ssd_dir(xs, bc, dta, ea, ed, B, N, reverse, p3=None, yb=None, dskip_row=None, norm_g=None):
    L = CHUNK
    nc = N // L
    nbat = SCAN_NBAT
    W = BRANCH_W
    final = yb is not None

    def crow(c):
        return (nc - 1 - c) if reverse else c

    in_specs = [
        pl.BlockSpec((nbat, L, W), lambda b, c: (b, crow(c), 0)),
        pl.BlockSpec((nbat, L, W), lambda b, c: (b, crow(c), 0)),
        pl.BlockSpec((nbat, L, LANE), lambda b, c: (b, crow(c), 0)),
        pl.BlockSpec((LANE, W), lambda b, c: (0, 0)),
        pl.BlockSpec((LANE, W), lambda b, c: (0, 0)),
    ]
    args = [xs, bc, dta, ea, ed]
    if final:
        in_specs += [
            pl.BlockSpec((nbat, L, W), lambda b, c: (b, crow(c), SS_Z // W)),
            pl.BlockSpec((nbat, L, W), lambda b, c: (b, crow(c), 0)),
            pl.BlockSpec((1, W), lambda b, c: (0, 0)),
            pl.BlockSpec((1, W), lambda b, c: (0, 0)),
        ]
        args += [p3, yb, dskip_row, norm_g]
    return pl.pallas_call(
        functools.partial(_ssd_kernel, reverse=reverse, final=final),
        grid=(B // nbat, nc),
        in_specs=in_specs,
        out_specs=pl.BlockSpec((nbat, L, W), lambda b, c: (b, crow(c), 0)),
        out_shape=jax.ShapeDtypeStruct((B, N, W), BF16 if final else F32),
        scratch_shapes=[pltpu.VMEM((nbat * SSD_GROUPS, SSD_STATE, W // SSD_GROUPS), F32)],
        compiler_params=_cparams(("parallel", "arbitrary")),
        name="ssd_bwd" if reverse else "ssd_fwd",
    )(*args)


def _ssd_expand_mats():
    ea_f = np.zeros((LANE, BRANCH_W), np.float32)
    ea_b = np.zeros((LANE, BRANCH_W), np.float32)
    ed_f = np.zeros((LANE, BRANCH_W), np.float32)
    ed_b = np.zeros((LANE, BRANCH_W), np.float32)
    for h in range(SSD_HEADS):
        ea_f[h, h * SSD_HD:(h + 1) * SSD_HD] = 1.0
        ea_b[8 + h, h * SSD_HD:(h + 1) * SSD_HD] = 1.0
        ed_f[16 + h, h * SSD_HD:(h + 1) * SSD_HD] = 1.0
        ed_b[24 + h, h * SSD_HD:(h + 1) * SSD_HD] = 1.0
    return ea_f, ea_b, ed_f, ed_b


_EA_F, _EA_B, _ED_F, _ED_B = _ssd_expand_mats()


def _ssd_branch(p, small, conv_w, conv_b, dtb_row, a_row, dskip_row, norm_g, B, N):
    xs, bc, dta = _ssd_prep(p, small, conv_w, conv_b, dtb_row, a_row, B, N)
    xs, bc, dta = (a.reshape(B, N, a.shape[-1]) for a in (xs, bc, dta))
    yb = _ssd_dir(xs, bc, dta, jnp.asarray(_EA_B), jnp.asarray(_ED_B), B, N, reverse=True)
    y = _ssd_dir(xs, bc, dta, jnp.asarray(_EA_F), jnp.asarray(_ED_F), B, N, reverse=False,
                 p3=p.reshape(B, N, P_PAD), yb=yb, dskip_row=dskip_row, norm_g=norm_g)
    return y.reshape(B * N, BRANCH_W)


def _merge_kernel(h_ref, y0_ref, y1_ref, y2_ref, y3_ref, wg_ref, wb_ref, o_ref):
    h = h_ref[...]
    acc = None
    for i, y_ref in enumerate((y0_ref, y1_ref, y2_ref, y3_ref)):
        gate = _sigmoid(_dot(h, wg_ref[i]))
        term = gate * _dot(y_ref[...], wb_ref[i])
        acc = term if acc is None else acc + term
    o_ref[...] = acc.astype(BF16)


def _merge(h, ys, wg, wb, tm=1024, tn=256):
    T = h.shape[0]
    return pl.pallas_call(
        _merge_kernel,
        grid=(T // tm, D_MODEL // tn),
        in_specs=[pl.BlockSpec((tm, D_MODEL), lambda i, j: (i, 0))]
        + [pl.BlockSpec((tm, BRANCH_W), lambda i, j: (i, 0))] * 4
        + [
            pl.BlockSpec((4, D_MODEL, tn), lambda i, j: (0, 0, j)),
            pl.BlockSpec((4, BRANCH_W, tn), lambda i, j: (0, 0, j)),
        ],
        out_specs=pl.BlockSpec((tm, tn), lambda i, j: (i, j)),
        out_shape=jax.ShapeDtypeStruct((T, D_MODEL), BF16),
        compiler_params=_cparams(("parallel", "arbitrary")),
        name="merge",
    )(h, *ys, wg, wb)


def _outproj_kernel(x_ref, m_ref, wo_ref, g_ref, wr_ref, xo_ref, h2_ref, r_ref):
    x = x_ref[...] + _dot(m_ref[...], wo_ref[...])
    xo_ref[...] = x
    ms = jnp.mean(x * x, axis=-1, keepdims=True)
    h2 = x * lax.rsqrt(ms + NORM_EPS) * g_ref[...]
    h2_ref[...] = h2.astype(BF16)
    logits = _dot_hi(h2, wr_ref[...])
    tm = x.shape[0]
    lane = lax.broadcasted_iota(jnp.int32, (tm, LANE), 1).astype(F32)
    neg = -jnp.inf
    big = float(LANE)
    gl = jnp.where(lane < MOE_GROUPS, logits, neg)
    gmax = jnp.max(gl, axis=1, keepdims=True)
    g_idx = jnp.min(jnp.where(gl == gmax, lane, big), axis=1, keepdims=True)
    g_w = 1.0 / jnp.sum(jnp.exp(gl - gmax), axis=1, keepdims=True)
    e_lo = MOE_GROUPS + g_idx * MOE_EPG
    el = jnp.where(lane >= e_lo, jnp.where(lane < e_lo + MOE_EPG, logits, neg), neg)
    m1 = jnp.max(el, axis=1, keepdims=True)
    i1 = jnp.min(jnp.where(el == m1, lane, big), axis=1, keepdims=True)
    el2 = jnp.where(lane == i1, neg, el)
    m2 = jnp.max(el2, axis=1, keepdims=True)
    i2 = jnp.min(jnp.where(el2 == m2, lane, big), axis=1, keepdims=True)
    den = jnp.sum(jnp.exp(el - m1), axis=1, keepdims=True)
    p1 = 1.0 / den
    p2 = jnp.exp(m2 - m1) / den
    w1 = p1 / (p1 + p2) * g_w
    w2 = p2 / (p1 + p2) * g_w
    e1 = i1 - MOE_GROUPS
    e2 = i2 - MOE_GROUPS
    r_ref[...] = jnp.where(lane == 0, e1, jnp.where(lane == 1, e2, jnp.where(lane == 2, w1, jnp.where(lane == 3, w2, 0.0))))


def _out_proj(x, merged, wo, g, wr, tm=512):
    T = x.shape[0]
    return pl.pallas_call(
        _outproj_kernel,
        grid=(T // tm,),
        in_specs=[
            pl.BlockSpec((tm, D_MODEL), lambda i: (i, 0)),
            pl.BlockSpec((tm, D_MODEL), lambda i: (i, 0)),
            pl.BlockSpec((D_MODEL, D_MODEL), lambda i: (0, 0)),
            pl.BlockSpec((1, D_MODEL), lambda i: (0, 0)),
            pl.BlockSpec((D_MODEL, LANE), lambda i: (0, 0)),
        ],
        out_specs=[
            pl.BlockSpec((tm, D_MODEL), lambda i: (i, 0)),
            pl.BlockSpec((tm, D_MODEL), lambda i: (i, 0)),
            pl.BlockSpec((tm, LANE), lambda i: (i, 0)),
        ],
        out_shape=[
            jax.ShapeDtypeStruct((T, D_MODEL), F32),
            jax.ShapeDtypeStruct((T, D_MODEL), BF16),
            jax.ShapeDtypeStruct((T, LANE), F32),
        ],
        compiler_params=_cparams(("parallel",)),
        name="out_proj_router",
    )(x, merged, wo, g, wr)


def _moe_ffn_kernel(be_ref, nu_ref, xs_ref, w1_ref, w3_ref, w2_ref, ys_ref):
    used = pl.program_id(0) < nu_ref[0]

    @pl.when(used)
    def _():
        xb = xs_ref[...]
        a = _dot(xb, w1_ref[0].astype(BF16))
        u = _dot(xb, w3_ref[0].astype(BF16))
        hmid = (_silu(a) * u).astype(BF16)
        ys_ref[...] = _dot(hmid, w2_ref[0].astype(BF16)).astype(BF16)

    @pl.when(jnp.logical_not(used))
    def _():
        ys_ref[...] = jnp.zeros_like(ys_ref)


def _moe_ffn(xs, blk_e, nused, w1, w3, w2):
    R = xs.shape[0]
    nblk = R // MOE_BLK
    return pl.pallas_call(
        _moe_ffn_kernel,
        grid_spec=pltpu.PrefetchScalarGridSpec(
            num_scalar_prefetch=2,
            grid=(nblk,),
            in_specs=[
                pl.BlockSpec((MOE_BLK, D_MODEL), lambda b, be, nu: (b, 0)),
                pl.BlockSpec((1, D_MODEL, MOE_FF), lambda b, be, nu: (be[b], 0, 0)),
                pl.BlockSpec((1, D_MODEL, MOE_FF), lambda b, be, nu: (be[b], 0, 0)),
                pl.BlockSpec((1, MOE_FF, D_MODEL), lambda b, be, nu: (be[b], 0, 0)),
            ],
            out_specs=pl.BlockSpec((MOE_BLK, D_MODEL), lambda b, be, nu: (b, 0)),
        ),
        out_shape=jax.ShapeDtypeStruct((R, D_MODEL), BF16),
        compiler_params=_cparams(("arbitrary",)),
        name="moe_ffn",
    )(blk_e, nused, xs, w1, w3, w2)


def _combine_kernel(x_ref, a_ref, b_ref, r_ref, g_ref, o_ref, *, final):
    r = r_ref[...]
    x = x_ref[...] + (r[:, 2:3] * a_ref[...].astype(F32) + r[:, 3:4] * b_ref[...].astype(F32))
    if final:
        ms = jnp.mean(x * x, axis=-1, keepdims=True)
        x = x * lax.rsqrt(ms + NORM_EPS) * g_ref[...]
    o_ref[...] = x


def _combine(x, a, b, route, g, final, row0, tm=512):
    T = x.shape[0]
    off = row0 // tm
    return pl.pallas_call(
        functools.partial(_combine_kernel, final=final),
        grid=(T // tm,),
        in_specs=[pl.BlockSpec((tm, D_MODEL), lambda i: (i, 0))]
        + [pl.BlockSpec((tm, D_MODEL), lambda i: (i + off, 0))] * 2
        + [pl.BlockSpec((tm, LANE), lambda i: (i + off, 0)), pl.BlockSpec((1, D_MODEL), lambda i: (0, 0))],
        out_specs=pl.BlockSpec((tm, D_MODEL), lambda i: (i, 0)),
        out_shape=jax.ShapeDtypeStruct((T, D_MODEL), F32),
        compiler_params=_cparams(("parallel",)),
        name="moe_combine",
    )(x, a, b, route, g)


def _moe(xs_groups, h2, route, w1, w3, w2, layer, g_final, final):
    T = h2.shape[0]
    K = 2
    e_idx = route[:, 0:K].astype(jnp.int32)
    flat_e = e_idx.reshape(-1)
    onehot = (flat_e[:, None] == jnp.arange(MOE_EXPERTS)[None, :]).astype(jnp.int32)
    csum = jnp.cumsum(onehot, axis=0)
    rank = jnp.take_along_axis(csum, flat_e[:, None], axis=1)[:, 0] - 1
    counts = csum[-1]
    pcounts = (counts + MOE_BLK - 1) // MOE_BLK * MOE_BLK
    pends = jnp.cumsum(pcounts)
    pstarts = pends - pcounts
    dest = pstarts[flat_e] + rank
    R = T * K + MOE_EXPERTS * MOE_BLK
    nblk = R // MOE_BLK
    row_src = jnp.zeros((R,), jnp.int32).at[dest].set(jnp.arange(T * K, dtype=jnp.int32) // K)
    nused = (pends[-1] // MOE_BLK).astype(jnp.int32)
    blk_start = jnp.arange(nblk, dtype=jnp.int32) * MOE_BLK
    blk_e = jnp.minimum(jnp.sum((pends[None, :] <= blk_start[:, None]).astype(jnp.int32), axis=1), MOE_EXPERTS - 1)
    last_e = blk_e[jnp.maximum(nused - 1, 0)]
    blk_e = jnp.where(jnp.arange(nblk) < nused, blk_e, last_e)
    xs = jnp.take(h2, row_src, axis=0, mode="clip")
    ys = _moe_ffn(xs, blk_e + layer * MOE_EXPERTS, nused.reshape(1), w1, w3, w2)
    pos = dest.reshape(T, K)
    ga = jnp.take(ys, pos[:, 0], axis=0, mode="clip")
    gb = jnp.take(ys, pos[:, 1], axis=0, mode="clip")
    outs, row0 = [], 0
    for x in xs_groups:
        outs.append(_combine(x, ga, gb, route, g_final, final, row0))
        row0 += x.shape[0]
    return outs


def _prep_layer(l, w_in, mlstm_gate_bias, ssd_conv_w, ssd_conv_b, ssd_dt_bias, ssd_a_log, ssd_d,
                gqa_q_norm, gqa_k_norm, w_router_group, w_router_expert):
    w = w_in[l]
    wm = jnp.take(w, jnp.asarray(_MAIN_IDX), axis=1)
    dscale = jnp.ones((P_COLS,), F32).at[DF_Q:DF_Q + BRANCH_W].set((DIFF_HD ** -0.5) * LOG2E)
    wm = jnp.pad((wm * dscale[None, :]).astype(BF16), ((0, 0), (0, P_PAD - P_COLS)))
    ws = jnp.zeros((D_MODEL, SMALL_COLS), F32).at[:, :32].set(jnp.take(w, jnp.asarray(_SMALL_IDX), axis=1)).astype(BF16)
    gb_row = jnp.zeros((1, SMALL_COLS), F32).at[0, :16].set(mlstm_gate_bias[l])
    dtb_row = jnp.zeros((1, LANE), F32).at[0, 16:32].set(ssd_dt_bias[l].reshape(-1))
    a_row = jnp.zeros((1, LANE), F32).at[0, 16:32].set((-jnp.exp(ssd_a_log[l])).reshape(-1))
    dskip_row = jnp.repeat(ssd_d[l], SSD_HD)[None, :]
    qg = jnp.tile(gqa_q_norm[l][jnp.asarray(_DEINT64)], LANE // GQA_HD)[None, :]
    kg = jnp.tile(gqa_k_norm[l][jnp.asarray(_DEINT64)], LANE // GQA_HD)[None, :]
    wr = jnp.zeros((D_MODEL, LANE), F32).at[:, :MOE_GROUPS].set(w_router_group[l])
    wr = wr.at[:, MOE_GROUPS:MOE_GROUPS + MOE_EXPERTS].set(w_router_expert[l])
    return dict(wm=wm, ws=ws, gb_row=gb_row, dtb_row=dtb_row, a_row=a_row, dskip_row=dskip_row,
                qg=qg, kg=kg, wr=wr, conv_w=ssd_conv_w[l], conv_b=ssd_conv_b[l][None, :])


def _bd_ones():
    i = np.arange(LANE)
    return (i[:, None] // GQA_HD == i[None, :] // GQA_HD).astype(np.float32)


def _mixer_layer(x, B, N, l, lp, prm, tabs):
    cos_t, sin_t, ktab, qtab = tabs
    bd32 = jnp.asarray(_bd_ones())
    bd = bd32.astype(BF16)
    lambda_init = 0.8 - 0.6 * math.exp(-0.3 * l)
    p, h, small = _in_proj(x, prm["norm_mix"][l][None, :], lp["wm"], lp["ws"])
    y0 = _mlstm_branch(p, small, lp["gb_row"], prm["mlstm_norm"][l][None, :], B, N)
    y1 = _gqa_branch(p, cos_t, sin_t, lp["qg"], lp["kg"], bd, B, N)
    y2 = _ssd_branch(p, small, lp["conv_w"], lp["conv_b"], lp["dtb_row"], lp["a_row"], lp["dskip_row"],
                     prm["ssd_norm"][l][None, :], B, N)
    y3 = _diff_branch(p, ktab, qtab, bd32, prm["diff_lambda"][l], prm["diff_norm"][l][None, :], lambda_init, B, N)
    merged = _merge(h, (y0, y1, y2, y3), lp["wg"], lp["wb"])
    return _out_proj(x, merged, lp["wo"], prm["norm_ffn"][l][None, :], lp["wr"])


def _model(x_groups, prm, L):
    dims = [(x.shape[0], x.shape[1]) for x in x_groups]
    xs = [x.reshape(B * N, D_MODEL) for x, (B, N) in zip(x_groups, dims)]
    tabs = [_rope_tables(N) + _alibi_tables(N) for _, N in dims]
    for l in range(DEPTH):
        mixed = [_mixer_layer(x, B, N, l, L[l], prm, tb) for x, (B, N), tb in zip(xs, dims, tabs)]
        h2cat, rcat = (jnp.concatenate([m[i] for m in mixed], axis=0) for i in (1, 2))
        xs = _moe([m[0] for m in mixed], h2cat, rcat, prm["w1"], prm["w3"], prm["w2"], l,
                  prm["norm_final"][None, :], final=(l == DEPTH - 1))
    return tuple(x.reshape(B, N, D_MODEL) for x, (B, N) in zip(xs, dims))


def kernel(x_prompt, x_sample, norm_mix, w_in, mlstm_gate_bias, mlstm_norm, gqa_q_norm, gqa_k_norm, ssd_conv_w, ssd_conv_b, ssd_dt_bias, ssd_a_log, ssd_d, ssd_norm, diff_lambda, diff_norm, w_branch, w_gate, w_out, norm_ffn, w_router_group, w_router_expert, moe_w_gate, moe_w_up, moe_w_down, norm_final):
    prm = dict(norm_mix=norm_mix, mlstm_norm=mlstm_norm, ssd_norm=ssd_norm, diff_lambda=diff_lambda,
               diff_norm=diff_norm, norm_ffn=norm_ffn, norm_final=norm_final)
    layers = []
    for l in range(DEPTH):
        lp = _prep_layer(l, w_in, mlstm_gate_bias, ssd_conv_w, ssd_conv_b, ssd_dt_bias, ssd_a_log, ssd_d,
                         gqa_q_norm, gqa_k_norm, w_router_group, w_router_expert)
        lp["wg"] = w_gate[l].astype(BF16)
        lp["wb"] = w_branch[l].astype(BF16)
        lp["wo"] = w_out[l].astype(BF16)
        layers.append(lp)
    prm["w1"] = moe_w_gate.reshape(DEPTH * MOE_EXPERTS, D_MODEL, MOE_FF)
    prm["w3"] = moe_w_up.reshape(DEPTH * MOE_EXPERTS, D_MODEL, MOE_FF)
    prm["w2"] = moe_w_down.reshape(DEPTH * MOE_EXPERTS, MOE_FF, D_MODEL)
    return _model([x_prompt, x_sample], prm, layers)
```

```python
import functools
import math

import jax
import jax.numpy as jnp
import numpy as np
from jax import lax
from jax.experimental import pallas as pl
from jax.experimental.pallas import tpu as pltpu

F32 = jnp.float32
BF16 = jnp.bfloat16

D_MODEL = 1024
DEPTH = 2
GRID_W = 64
BRANCH_W = 512
NORM_EPS = 1e-6
MLSTM_HEADS = 4
MLSTM_HD = 128
GQA_HD = 64
GQA_HEADS = 8
GQA_KV_HEADS = 2
ROPE_THETA = 10000.0
SSD_HD = 64
SSD_HEADS = 8
SSD_GROUPS = 2
SSD_STATE = 128
SSD_CONV = 5
DIFF_HD = 64
DIFF_HEADS = 4
MOE_GROUPS = 4
MOE_EPG = 8
MOE_EXPERTS = 32
MOE_FF = 512
CHUNK = 128
LOG2E = 1.4426950408889634

_MLSTM_COLS = 4 * BRANCH_W + 4 * MLSTM_HEADS
_GQA_COLS = GQA_HEADS * GQA_HD + 2 * GQA_KV_HEADS * GQA_HD
_SSD_XBC = BRANCH_W + 2 * SSD_GROUPS * SSD_STATE
_SSD_COLS = BRANCH_W + _SSD_XBC + 2 * SSD_HEADS
_O0 = _MLSTM_COLS
_O1 = _O0 + _GQA_COLS
_O2 = _O1 + _SSD_COLS

ML_Q, ML_K, ML_V, ML_O = 0, 512, 1024, 1536
GQ_Q = 2048
SS_Z = 2560
DF_Q, DF_K, DF_V = 3072, 3584, 4096
SS_X, SS_BC = 4608, 5120
GQ_KV = 5632
P_COLS = 5888
P_PAD = 6144
SMALL_COLS = 128

LANE = 128
VMEM_LIMIT = 48 * 1024 * 1024

MOE_BLK = 256


def _cparams(sem):
    return pltpu.CompilerParams(dimension_semantics=sem, vmem_limit_bytes=VMEM_LIMIT)


def _dot(a, b):
    return jnp.dot(a, b, preferred_element_type=F32)


def _dot_nt(a, b):
    return lax.dot_general(a, b, (((1,), (1,)), ((), ())), preferred_element_type=F32)


def _dot_tn(a, b):
    return lax.dot_general(a, b, (((0,), (0,)), ((), ())), preferred_element_type=F32)


def _dot_hi(a, b):
    return jnp.dot(a, b, preferred_element_type=F32, precision=lax.Precision.HIGHEST)


def _sigmoid(x):
    return 1.0 / (1.0 + jnp.exp(-x))


def _silu(x):
    return x * _sigmoid(x)


def _log_sigmoid(x):
    return jnp.minimum(x, 0.0) - jnp.log(1.0 + jnp.exp(-jnp.abs(x)))


def _softplus(x):
    return jnp.maximum(x, 0.0) + jnp.log(1.0 + jnp.exp(-jnp.abs(x)))


def _main_col_index():
    gq = np.arange(GQA_HEADS * GQA_HD).reshape(GQA_HEADS, GQA_HD // 2, 2)
    gq = np.concatenate([gq[..., 0], gq[..., 1]], axis=-1).reshape(-1)
    gk = np.arange(GQA_KV_HEADS * GQA_HD).reshape(GQA_KV_HEADS, GQA_HD // 2, 2)
    gk = np.concatenate([gk[..., 0], gk[..., 1]], axis=-1).reshape(-1)
    segs = [
        np.arange(0, 4 * BRANCH_W),
        _O0 + gq,
        _O1 + np.arange(0, BRANCH_W),
        _O2 + np.arange(0, 3 * BRANCH_W),
        _O1 + BRANCH_W + np.arange(0, _SSD_XBC),
        _O0 + GQA_HEADS * GQA_HD + gk,
        _O0 + GQA_HEADS * GQA_HD + GQA_KV_HEADS * GQA_HD + np.arange(GQA_KV_HEADS * GQA_HD),
    ]
    idx = np.concatenate(segs)
    assert idx.shape[0] == P_COLS
    return idx


def _small_col_index():
    return np.concatenate([4 * BRANCH_W + np.arange(4 * MLSTM_HEADS),
                           _O1 + BRANCH_W + _SSD_XBC + np.arange(2 * SSD_HEADS)])


_MAIN_IDX = _main_col_index()
_SMALL_IDX = _small_col_index()
_DEINT64 = np.concatenate([np.arange(0, GQA_HD, 2), np.arange(1, GQA_HD, 2)])


def _inproj_kernel(x_ref, g_ref, w_ref, ws_ref, p_ref, h_ref, s_ref, h_scr):
    @pl.when(pl.program_id(1) == 0)
    def _():
        x = x_ref[...]
        ms = jnp.mean(x * x, axis=-1, keepdims=True)
        h = (x * lax.rsqrt(ms + NORM_EPS) * g_ref[...]).astype(BF16)
        h_scr[...] = h
        h_ref[...] = h
        s_ref[...] = _dot(h, ws_ref[...])

    p_ref[...] = _dot(h_scr[...], w_ref[...]).astype(BF16)


def _in_proj(x, g, w_main, w_small, tm=1024, tn=1024):
    T = x.shape[0]
    return pl.pallas_call(
        _inproj_kernel,
        grid=(T // tm, P_PAD // tn),
        in_specs=[
            pl.BlockSpec((tm, D_MODEL), lambda i, j: (i, 0)),
            pl.BlockSpec((1, D_MODEL), lambda i, j: (0, 0)),
            pl.BlockSpec((D_MODEL, tn), lambda i, j: (0, j)),
            pl.BlockSpec((D_MODEL, SMALL_COLS), lambda i, j: (0, 0)),
        ],
        out_specs=[
            pl.BlockSpec((tm, tn), lambda i, j: (i, j)),
            pl.BlockSpec((tm, D_MODEL), lambda i, j: (i, 0)),
            pl.BlockSpec((tm, SMALL_COLS), lambda i, j: (i, 0)),
        ],
        out_shape=[
            jax.ShapeDtypeStruct((T, P_PAD), BF16),
            jax.ShapeDtypeStruct((T, D_MODEL), BF16),
            jax.ShapeDtypeStruct((T, SMALL_COLS), F32),
        ],
        scratch_shapes=[pltpu.VMEM((tm, D_MODEL), BF16)],
        compiler_params=_cparams(("parallel", "arbitrary")),
        name="in_proj",
    )(x, g, w_main, w_small)


def _tri_masks(L, reverse):
    r = lax.broadcasted_iota(jnp.int32, (L, L), 0)
    c = lax.broadcasted_iota(jnp.int32, (L, L), 1)
    return (c >= r) if reverse else (c <= r)


def _mlstm_kernel(*refs, reverse, final):
    q_ref = refs[0]
    ct_scr, n_scr, m_scr = refs[-3:]

    @pl.when(pl.program_id(1) == 0)
    def _():
        ct_scr[...] = jnp.zeros_like(ct_scr)
        n_scr[...] = jnp.zeros_like(n_scr)
        m_scr[...] = jnp.zeros_like(m_scr)

    _mlstm_step(refs, reverse, final)


def _mlstm_step(refs, reverse, final):
    if final:
        (q_ref, k_ref, v_ref, sm_ref, gb_ref, o_ref, hb_ref, ng_ref,
         out_ref, ct_scr, n_scr, m_scr) = refs
    else:
        q_ref, k_ref, v_ref, sm_ref, gb_ref, out_ref, ct_scr, n_scr, m_scr = refs
    L = CHUNK
    H = MLSTM_HEADS
    d = MLSTM_HD
    nbat = q_ref.shape[0]
    pairs = [(bb, h) for bb in range(nbat) for h in range(H)]
    mask = _tri_masks(L, reverse)
    tri = mask.astype(F32)
    i_off = 8 if reverse else 0
    f_off = i_off + 4
    edge = 0 if reverse else L - 1
    scale = d ** -0.5
    lane = lax.broadcasted_iota(jnp.int32, (1, LANE), 1)

    gates = [sm_ref[bb] + gb_ref[...] for bb in range(nbat)]
    bcum = [_dot_hi(tri, _log_sigmoid(g)) for g in gates]
    gates_t = [g.T for g in gates]
    bcum_t = [b.T for b in bcum]
    m_all = [m_scr[bb] for bb in range(nbat)]

    def cols(h):
        return slice(h * d, (h + 1) * d)

    b_col, i_col, m_prev, m_t, w_intra, w_inter = {}, {}, {}, {}, {}, {}
    for pr in pairs:
        bb, h = pr
        b_col[pr] = bcum[bb][:, f_off + h:f_off + h + 1]
        i_col[pr] = gates[bb][:, i_off + h:i_off + h + 1]
        a_row = gates_t[bb][i_off + h:i_off + h + 1, :] - bcum_t[bb][f_off + h:f_off + h + 1, :]
        m_prev[pr] = m_all[bb][:, h:h + 1]
        dmat = jnp.where(mask, b_col[pr] + a_row, -jnp.inf)
        inter = b_col[pr] + m_prev[pr]
        m_t[pr] = jnp.maximum(inter, jnp.max(dmat, axis=1, keepdims=True))
        w_intra[pr] = jnp.exp(dmat - m_t[pr])
        w_inter[pr] = jnp.exp(inter - m_t[pr])

    ks, s = {}, {}
    for pr in pairs:
        bb, h = pr
        ks[pr] = (k_ref[bb, :, cols(h)].astype(F32) * scale).astype(BF16)
        s[pr] = _dot_nt(q_ref[bb, :, cols(h)], ks[pr]) * w_intra[pr]

    hs = {}
    for pr in pairs:
        bb, h = pr
        qh = q_ref[bb, :, cols(h)]
        ct = ct_scr[bb * H + h]
        n_row = n_scr[bb * H + h]
        num = _dot(s[pr].astype(BF16), v_ref[bb, :, cols(h)]) + w_inter[pr] * _dot(qh, ct.astype(BF16))
        qn = jnp.sum(qh.astype(F32) * n_row, axis=1, keepdims=True)
        den = jnp.sum(s[pr], axis=1, keepdims=True) + w_inter[pr] * qn
        hs[pr] = num / jnp.maximum(jnp.abs(den), jnp.exp(-m_t[pr]))

    for pr in pairs:
        bb, h = pr
        vh = v_ref[bb, :, cols(h)]
        btot = b_col[pr][edge:edge + 1, :]
        dec = btot - b_col[pr] + i_col[pr]
        m_new = jnp.maximum(btot + m_prev[pr], jnp.max(dec, axis=0, keepdims=True))
        ws = jnp.exp(dec - m_new)
        wc = jnp.exp(btot + m_prev[pr] - m_new)
        wsv = (ws * vh.astype(F32)).astype(BF16)
        ct_scr[bb * H + h] = wc * ct_scr[bb * H + h] + _dot_tn(ks[pr], wsv)
        n_scr[bb * H + h] = wc * n_scr[bb * H + h] + jnp.sum(ws * ks[pr].astype(F32), axis=0, keepdims=True)
        m_all[bb] = jnp.where(lane == h, m_new, m_all[bb])
    for bb in range(nbat):
        m_scr[bb] = m_all[bb]

    for pr in pairs:
        bb, h = pr
        if not final:
            out_ref[bb, :, cols(h)] = hs[pr]
        else:
            hsum = hs[pr] + hb_ref[bb, :, cols(h)]
            ms = jnp.mean(hsum * hsum, axis=1, keepdims=True)
            y = hsum * lax.rsqrt(ms + NORM_EPS) * ng_ref[:, cols(h)]
            o = o_ref[bb, :, cols(h)].astype(F32)
            out_ref[bb, :, cols(h)] = (_sigmoid(o) * y).astype(BF16)


SCAN_NBAT = 2


def _mlstm_dir(p3, small3, gate_bias_row, B, N, reverse, hb=None, norm_g=None):
    L = CHUNK
    nc = N // L
    nbat = SCAN_NBAT
    final = hb is not None

    def crow(c):
        return (nc - 1 - c) if reverse else c

    W = BRANCH_W
    in_specs = [
        pl.BlockSpec((nbat, L, W), lambda b, c: (b, crow(c), ML_Q // W)),
        pl.BlockSpec((nbat, L, W), lambda b, c: (b, crow(c), ML_K // W)),
        pl.BlockSpec((nbat, L, W), lambda b, c: (b, crow(c), ML_V // W)),
        pl.BlockSpec((nbat, L, SMALL_COLS), lambda b, c: (b, crow(c), 0)),
        pl.BlockSpec((1, SMALL_COLS), lambda b, c: (0, 0)),
    ]
    args = [p3, p3, p3, small3, gate_bias_row]
    if final:
        in_specs += [
            pl.BlockSpec((nbat, L, W), lambda b, c: (b, crow(c), ML_O // W)),
            pl.BlockSpec((nbat, L, W), lambda b, c: (b, crow(c), 0)),
            pl.BlockSpec((1, W), lambda b, c: (0, 0)),
        ]
        args += [p3, hb, norm_g]
    return pl.pallas_call(
        functools.partial(_mlstm_kernel, reverse=reverse, final=final),
        grid=(B // nbat, nc),
        in_specs=in_specs,
        out_specs=pl.BlockSpec((nbat, L, W), lambda b, c: (b, crow(c), 0)),
        out_shape=jax.ShapeDtypeStruct((B, N, W), BF16 if final else F32),
        scratch_shapes=[
            pltpu.VMEM((nbat * MLSTM_HEADS, MLSTM_HD, MLSTM_HD), F32),
            pltpu.VMEM((nbat * MLSTM_HEADS, 1, MLSTM_HD), F32),
            pltpu.VMEM((nbat, 1, LANE), F32),
        ],
        compiler_params=_cparams(("parallel", "arbitrary")),
        name="mlstm_bwd" if reverse else "mlstm_fwd",
    )(*args)


def _mlstm_branch(p, small, gate_bias_row, norm_g, B, N):
    p3 = p.reshape(B, N, P_PAD)
    small3 = small.reshape(B, N, SMALL_COLS)
    hb = _mlstm_dir(p3, small3, gate_bias_row, B, N, reverse=True)
    y = _mlstm_dir(p3, small3, gate_bias_row, B, N, reverse=False, hb=hb, norm_g=norm_g)
    return y.reshape(B * N, BRANCH_W)


def _gqa_prep_kernel(q_ref, kv_ref, cos_ref, sin_ref, qg_ref, kg_ref, bd_ref, qo_ref, ko_ref, vo_ref):
    tm = q_ref.shape[0]
    bd = bd_ref[...]
    cos = cos_ref[...]
    sin = sin_ref[...]
    lane = lax.broadcasted_iota(jnp.int32, (tm, LANE), 1)
    first = (lane % GQA_HD) < (GQA_HD // 2)
    lo = lane < GQA_HD

    def norm_rope(x, g):
        ss = _dot((x * x).astype(BF16), bd)
        xn = x * lax.rsqrt(ss * (1.0 / GQA_HD) + NORM_EPS) * g
        partner = jnp.where(first, pltpu.roll(xn, LANE - GQA_HD // 2, axis=1),
                            pltpu.roll(xn, GQA_HD // 2, axis=1))
        return xn * cos + partner * sin

    qscale = (GQA_HD ** -0.5) * LOG2E
    for a in range(GQA_HEADS * GQA_HD // LANE):
        x = q_ref[:, a * LANE:(a + 1) * LANE].astype(F32)
        qo_ref[:, a * LANE:(a + 1) * LANE] = (norm_rope(x, qg_ref[...]) * qscale).astype(BF16)
    k = norm_rope(kv_ref[:, 0:LANE].astype(F32), kg_ref[...])
    k_sw = pltpu.roll(k, GQA_HD, axis=1)
    ko_ref[0] = jnp.where(lo, k, k_sw).astype(BF16)
    ko_ref[1] = jnp.where(lo, k_sw, k).astype(BF16)
    v = kv_ref[:, LANE:2 * LANE].astype(F32)
    v_sw = pltpu.roll(v, GQA_HD, axis=1)
    vo_ref[0] = jnp.where(lo, v, 1.0).astype(BF16)
    vo_ref[1] = jnp.where(lo, v_sw, 1.0).astype(BF16)


def _gqa_prep(p, cos_t, sin_t, qg, kg, bd, B, N, tm=512):
    T = B * N
    nb = N // tm
    return pl.pallas_call(
        _gqa_prep_kernel,
        grid=(T // tm,),
        in_specs=[
            pl.BlockSpec((tm, 512), lambda i: (i, GQ_Q // 512)),
            pl.BlockSpec((tm, 256), lambda i: (i, GQ_KV // 256)),
            pl.BlockSpec((tm, LANE), lambda i: (i % nb, 0)),
            pl.BlockSpec((tm, LANE), lambda i: (i % nb, 0)),
            pl.BlockSpec((1, LANE), lambda i: (0, 0)),
            pl.BlockSpec((1, LANE), lambda i: (0, 0)),
            pl.BlockSpec((LANE, LANE), lambda i: (0, 0)),
        ],
        out_specs=[
            pl.BlockSpec((tm, 512), lambda i: (i, 0)),
            pl.BlockSpec((GQA_KV_HEADS, tm, LANE), lambda i: (0, i, 0)),
            pl.BlockSpec((GQA_KV_HEADS, tm, LANE), lambda i: (0, i, 0)),
        ],
        out_shape=[
            jax.ShapeDtypeStruct((T, 512), BF16),
            jax.ShapeDtypeStruct((GQA_KV_HEADS, T, LANE), BF16),
            jax.ShapeDtypeStruct((GQA_KV_HEADS, T, LANE), BF16),
        ],
        compiler_params=_cparams(("parallel",)),
        name="gqa_prep",
    )(p, p, cos_t, sin_t, qg, kg, bd)


FLASH_TKB = 512


def _softmax_keys(s_ref, p_ref, m_scr, al_scr, l_scr, M):
    tkb = s_ref.shape[0]
    for c in range(M // LANE):
        cols = slice(c * LANE, (c + 1) * LANE)
        mx = jnp.max(jnp.max(s_ref[:, cols].reshape(tkb // 8, 8, LANE), axis=0), axis=0, keepdims=True)
        m_prev = m_scr[:, cols]
        m_new = jnp.maximum(m_prev, mx)
        alpha = jnp.exp2(m_prev - m_new)
        p = jnp.exp2(s_ref[:, cols] - m_new)
        p_ref[:, cols] = p.astype(BF16)
        if l_scr is not None:
            lsum = jnp.sum(jnp.sum(p.reshape(tkb // 8, 8, LANE), axis=0), axis=0, keepdims=True)
            l_scr[:, cols] = alpha * l_scr[:, cols] + lsum
        al_scr[:, cols] = alpha
        m_scr[:, cols] = m_new


def _flash_blocks(nb, qk, pv, softmax, fix=None):
    qk(0)
    for i in range(nb):
        if i + 1 < nb:
            qk(i + 1)
        if fix is not None:
            fix(i)
        softmax(i)
        pv(i)


FLASH_SAFE_LOG2 = 100.0


def _flash_one_pass(nb, scores, values, m_scr, l_scr, acc_scr, p_scr):
    for i in range(nb):
        s = scores(i)
        tkb, M = s.shape
        m_used = m_scr[...]
        p = jnp.exp2(s - m_used)
        p_scr[i % 2] = p.astype(BF16)
        mx = jnp.max(jnp.max(s.reshape(tkb // 8, 8, M), axis=0), axis=0, keepdims=True)
        m_new = jnp.maximum(m_used, mx)
        alpha = jnp.exp2(m_used - m_new)
        if l_scr is not None:
            lsum = jnp.sum(jnp.sum(p.reshape(tkb // 8, 8, M), axis=0), axis=0, keepdims=True)
            l_scr[...] = alpha * (l_scr[...] + lsum)
        acc_scr[...] = alpha * (acc_scr[...] + _dot_tn(values(i), p_scr[i % 2]))
        m_scr[...] = m_new


def _flash_score_bound(qs, kmax):
    qf = qs.astype(F32)
    n2 = _dot_nt(jnp.ones((8, LANE), BF16), (qf * qf).astype(BF16))[0:1, :]
    return jnp.sqrt(n2) * kmax * 1.01 + 1.0


def _flash_is_safe(u_scr, m_scr):
    return jnp.where(jnp.max(u_scr[...] - m_scr[...]) <= FLASH_SAFE_LOG2, 1, 0)


def _gqa_flash_kernel(q_ref, k_ref, v_ref, kmax_ref, o_ref, qs_scr, m_scr, al_scr, acc_scr, s_scr, p_scr,
                      u_scr, safe_ref):
    tq = q_ref.shape[0]
    M = 4 * tq
    tkb = s_scr.shape[1]
    nb = k_ref.shape[0] // tkb
    kj = pl.program_id(3)

    @pl.when(kj == 0)
    def _():
        lane = lax.broadcasted_iota(jnp.int32, (tq, LANE), 1)
        lo = lane < GQA_HD
        zero = jnp.zeros((tq, LANE), BF16)
        for a in range(2):
            qa = q_ref[:, a * LANE:(a + 1) * LANE]
            qs_scr[(2 * a) * tq:(2 * a + 1) * tq, :] = jnp.where(lo, qa, zero)
            qs_scr[(2 * a + 1) * tq:(2 * a + 2) * tq, :] = jnp.where(lo, zero, qa)
        acc_scr[...] = jnp.zeros_like(acc_scr)
        u = _flash_score_bound(qs_scr[...], kmax_ref[:, 0:1])
        u_scr[...] = u
        early = jnp.max(u) <= 0.5 * FLASH_SAFE_LOG2
        safe_ref[0] = jnp.where(early, 2, 0)
        m_scr[...] = jnp.where(early, -u, jnp.full_like(u, -jnp.inf))

    def scores(i):
        return _dot_nt(k_ref[i * tkb:(i + 1) * tkb, :], qs_scr[...])

    def values(i):
        return v_ref[i * tkb:(i + 1) * tkb, :]

    def qk(i):
        s_scr[i % 2] = scores(i)

    def softmax(i):
        _softmax_keys(s_scr.at[i % 2], p_scr.at[i % 2], m_scr, al_scr, None, M)

    def pv(i):
        acc_scr[...] = al_scr[...] * acc_scr[...] + _dot_tn(values(i), p_scr[i % 2])

    one_pass = safe_ref[0] >= 1

    @pl.when(jnp.logical_not(one_pass))
    def _():
        _flash_blocks(nb, qk, pv, softmax)

    @pl.when(one_pass)
    def _():
        _flash_one_pass(nb, scores, values, m_scr, None, acc_scr, p_scr)

    @pl.when(kj == 0)
    def _():
        safe_ref[0] = jnp.maximum(safe_ref[0], _flash_is_safe(u_scr, m_scr))

    @pl.when(kj == pl.num_programs(3) - 1)
    def _():
        inv = 1.0 / acc_scr[GQA_HD:GQA_HD + 1, :]
        ot = jnp.concatenate([acc_scr[0:GQA_HD, h * tq:(h + 1) * tq] * inv[:, h * tq:(h + 1) * tq]
                              for h in range(4)], axis=0)
        o_ref[...] = ot.T.astype(BF16)


def _gqa_flash(qn, k2, va, B, N, tq=256, tk=4096):
    T = B * N
    tk = min(tk, N)
    nq, nk = N // tq, N // tk
    M = 4 * tq
    tkb = FLASH_TKB
    kf = k2[:, :, 0:GQA_HD].astype(F32).reshape(GQA_KV_HEADS, B, N, GQA_HD)
    kmax = jnp.max(jnp.sqrt(jnp.sum(kf * kf, axis=-1)), axis=-1)
    kmax = jnp.broadcast_to(kmax.T.reshape(B * GQA_KV_HEADS, 1, 1), (B * GQA_KV_HEADS, 1, LANE))
    return pl.pallas_call(
        _gqa_flash_kernel,
        grid=(B, GQA_KV_HEADS, nq, nk),
        in_specs=[
            pl.BlockSpec((tq, 256), lambda b, g, i, j: (b * nq + i, g)),
            pl.BlockSpec((None, tk, LANE), lambda b, g, i, j: (g, b * nk + j, 0)),
            pl.BlockSpec((None, tk, LANE), lambda b, g, i, j: (g, b * nk + j, 0)),
            pl.BlockSpec((None, 1, LANE), lambda b, g, i, j: (b * GQA_KV_HEADS + g, 0, 0)),
        ],
        out_specs=pl.BlockSpec((tq, 256), lambda b, g, i, j: (b * nq + i, g)),
        out_shape=jax.ShapeDtypeStruct((T, 512), BF16),
        scratch_shapes=[
            pltpu.VMEM((M, LANE), BF16),
            pltpu.VMEM((1, M), F32),
            pltpu.VMEM((1, M), F32),
            pltpu.VMEM((LANE, M), F32),
            pltpu.VMEM((2, tkb, M), F32),
            pltpu.VMEM((2, tkb, M), BF16),
            pltpu.VMEM((1, M), F32),
            pltpu.SMEM((1,), jnp.int32),
        ],
        compiler_params=_cparams(("parallel", "parallel", "parallel", "arbitrary")),
        name="gqa_flash",
    )(qn, k2, va, kmax)


def _rope_tables(N):
    rows = N // GRID_W
    row = jnp.repeat(jnp.arange(rows, dtype=F32), GRID_W)
    col = (jnp.arange(N) % GRID_W).astype(F32)
    half = GQA_HD // 2
    inv = 1.0 / (ROPE_THETA ** (jnp.arange(0, half, 2, dtype=F32) / half))
    ang = jnp.concatenate([row[:, None] * inv, col[:, None] * inv], axis=-1)
    cos, sin = jnp.cos(ang), jnp.sin(ang)
    cos_h = jnp.concatenate([cos, cos], axis=-1)
    sin_h = jnp.concatenate([-sin, sin], axis=-1)
    return jnp.tile(cos_h, (1, LANE // GQA_HD)), jnp.tile(sin_h, (1, LANE // GQA_HD))


def _gqa_branch(p, cos_t, sin_t, qg, kg, bd, B, N):
    qn, k2, va = _gqa_prep(p, cos_t, sin_t, qg, kg, bd, B, N)
    return _gqa_flash(qn, k2, va, B, N)


DIFF_SKIP_LOG2 = 40.0


def _alibi_slope_log2(h):
    return (2.0 ** (-8.0 * (h + 1) / DIFF_HEADS)) * LOG2E


def _alibi_tables(N):
    pos = np.arange(N)
    hi = ((pos // 128) * 128).astype(np.float32)
    lo = (pos % 128).astype(np.float32)
    kt = np.zeros((DIFF_HEADS, N, LANE), np.float32)
    qt = np.zeros((DIFF_HEADS, N, LANE), np.float32)
    for h in range(DIFF_HEADS):
        s = np.float32(_alibi_slope_log2(h))
        s1 = np.float32(s).astype(BF16).astype(np.float32)
        s2 = np.float32(s - s1).astype(BF16).astype(np.float32)
        for c, val in enumerate((hi, hi, lo, lo)):
            kt[h, :, c] = val
            qt[h, :, 4 + c] = -val
        for c, val in enumerate((s1, s2, s1, s2)):
            kt[h, :, 4 + c] = val
            qt[h, :, c] = val
    return (jnp.asarray(kt.reshape(DIFF_HEADS * N, LANE), BF16), jnp.asarray(qt.reshape(DIFF_HEADS * N, LANE), BF16))


def _diff_flash_kernel(lo_ref, hi_ref, q_ref, qa_ref, k_ref, ka_ref, v_ref, kmax_ref, lp_ref, ng_ref, o_ref,
                       qs_scr, m_scr, l_scr, al_scr, acc_scr, s_scr, p_scr, u_scr, safe_ref, *, lambda_init):
    tq = q_ref.shape[0]
    tk = k_ref.shape[0]
    M = 2 * tq
    tkb = s_scr.shape[1]
    nb = tk // tkb
    b, h, qi, kj = pl.program_id(0), pl.program_id(1), pl.program_id(2), pl.program_id(3)
    idx = (b * pl.num_programs(1) + h) * pl.num_programs(2) + qi
    first, last = lo_ref[idx], hi_ref[idx]

    @pl.when(kj == first)
    def _():
        lane = lax.broadcasted_iota(jnp.int32, (tq, LANE), 1)
        lo = lane < DIFF_HD
        zero = jnp.zeros((tq, LANE), BF16)
        q = q_ref[...]
        qa = qa_ref[...]
        for var, aug in enumerate((qa, -qa)):
            qs_scr[var, 0:tq, 0:LANE] = jnp.where(lo, q, zero)
            qs_scr[var, tq:M, 0:LANE] = jnp.where(lo, zero, q)
            qs_scr[var, 0:tq, LANE:2 * LANE] = aug
            qs_scr[var, tq:M, LANE:2 * LANE] = aug
        l_scr[...] = jnp.zeros_like(l_scr)
        acc_scr[...] = jnp.zeros_like(acc_scr)
        u = _flash_score_bound(qs_scr[0, :, 0:LANE], kmax_ref[:, 0:1])
        u_scr[...] = u
        early = jnp.max(u) <= 0.5 * FLASH_SAFE_LOG2
        safe_ref[0] = jnp.where(early, 2, 0)
        m_scr[...] = jnp.where(early, -u, jnp.full_like(u, -jnp.inf))

    is_left = (kj + 1) * tk <= qi * tq
    is_right = kj * tk >= (qi + 1) * tq
    pure = jnp.logical_or(is_left, is_right)
    active = jnp.logical_and(kj >= first, kj <= last)
    sel = jnp.where(is_right, 1, 0)

    def scores(i):
        rows = slice(i * tkb, (i + 1) * tkb)
        kk = jnp.concatenate([k_ref[rows, :], ka_ref[rows, :]], axis=1)
        return _dot_nt(kk, qs_scr[sel])

    def values(i):
        return v_ref[i * tkb:(i + 1) * tkb, :]

    def qk(i):
        s_scr[i % 2] = scores(i)

    def fix(i):
        slope = jnp.exp2(-2.0 * (jnp.full((1, 1), h, jnp.int32).astype(F32) + 1.0)) * LOG2E
        j = kj * tk + i * tkb + lax.broadcasted_iota(jnp.int32, (tkb, 1), 0)
        col = lax.broadcasted_iota(jnp.int32, (1, M), 1)
        iq = qi * tq + col - jnp.where(col >= tq, tq, 0)
        d = jnp.maximum(j - iq, 0).astype(F32)
        s_scr[i % 2] = s_scr[i % 2] - (2.0 * slope) * d

    def softmax(i):
        _softmax_keys(s_scr.at[i % 2], p_scr.at[i % 2], m_scr, al_scr, l_scr, M)

    def pv(i):
        acc_scr[...] = al_scr[...] * acc_scr[...] + _dot_tn(values(i), p_scr[i % 2])

    @pl.when(jnp.logical_and(active, pure))
    def _():
        safe = safe_ref[0]
        one_pass = jnp.logical_or(safe == 2, jnp.logical_and(safe == 1, kj != first))

        @pl.when(jnp.logical_not(one_pass))
        def _():
            _flash_blocks(nb, qk, pv, softmax)

        @pl.when(one_pass)
        def _():
            _flash_one_pass(nb, scores, values, m_scr, l_scr, acc_scr, p_scr)

    @pl.when(jnp.logical_and(active, jnp.logical_not(pure)))
    def _():
        _flash_blocks(nb, qk, pv, softmax, fix=fix)

    @pl.when(kj == first)
    def _():
        safe_ref[0] = jnp.maximum(safe_ref[0], _flash_is_safe(u_scr, m_scr))

    @pl.when(kj == last)
    def _():
        lp = lp_ref[...]
        s01 = jnp.sum(jnp.sum(lp[0:1] * lp[1:2], axis=1, keepdims=True), axis=0, keepdims=True)
        s23 = jnp.sum(jnp.sum(lp[2:3] * lp[3:4], axis=1, keepdims=True), axis=0, keepdims=True)
        lam = jnp.exp(s01) - jnp.exp(s23) + lambda_init
        inv = 1.0 / l_scr[...]
        ot = acc_scr[:, 0:tq] * inv[:, 0:tq] - lam * (acc_scr[:, tq:M] * inv[:, tq:M])
        o = ot.T
        ms = jnp.mean(o * o, axis=1, keepdims=True)
        o_ref[...] = (o * lax.rsqrt(ms + NORM_EPS) * ng_ref[...] * (1.0 - lambda_init)).astype(BF16)


def _diff_prep_kernel(q_ref, k_ref, v_ref, bd_ref, ko_ref, vo_ref, nrm_ref):
    tm = q_ref.shape[0]
    lane = lax.broadcasted_iota(jnp.int32, (tm, LANE), 1)
    nrm = jnp.zeros((tm, LANE), F32)
    for h in range(DIFF_HEADS):
        cols = slice(h * LANE, (h + 1) * LANE)
        ko_ref[h] = k_ref[:, cols]
        vo_ref[h] = v_ref[:, cols]
        for src, off in ((q_ref, 0), (k_ref, DIFF_HEADS)):
            x = src[:, cols].astype(F32)
            ss = _dot_hi(x * x, bd_ref[...])
            n = jnp.sqrt(jnp.maximum(ss, pltpu.roll(ss, DIFF_HD, axis=1)))
            nrm = jnp.where(lane == off + h, n, nrm)
    nrm_ref[...] = nrm


def _diff_prep(p, bd32, B, N, tm=512):
    T = B * N
    W = BRANCH_W
    return pl.pallas_call(
        _diff_prep_kernel,
        grid=(T // tm,),
        in_specs=[
            pl.BlockSpec((tm, W), lambda i: (i, DF_Q // W)),
            pl.BlockSpec((tm, W), lambda i: (i, DF_K // W)),
            pl.BlockSpec((tm, W), lambda i: (i, DF_V // W)),
            pl.BlockSpec((LANE, LANE), lambda i: (0, 0)),
        ],
        out_specs=[
            pl.BlockSpec((DIFF_HEADS, tm, LANE), lambda i: (0, i, 0)),
            pl.BlockSpec((DIFF_HEADS, tm, LANE), lambda i: (0, i, 0)),
            pl.BlockSpec((tm, LANE), lambda i: (i, 0)),
        ],
        out_shape=[
            jax.ShapeDtypeStruct((DIFF_HEADS, T, LANE), BF16),
            jax.ShapeDtypeStruct((DIFF_HEADS, T, LANE), BF16),
            jax.ShapeDtypeStruct((T, LANE), F32),
        ],
        compiler_params=_cparams(("parallel",)),
        name="diff_prep",
    )(p, p, p, bd32)


def _diff_bands(nrm, B, N, tq, tk):
    nq, nkt = N // tq, N // tk
    qn = nrm[:, 0:DIFF_HEADS].reshape(B, N, DIFF_HEADS)
    kn = nrm[:, DIFF_HEADS:2 * DIFF_HEADS].reshape(B, N, DIFF_HEADS)
    qmax = jnp.max(qn.reshape(B, nq, tq, DIFF_HEADS), axis=2)
    kself = jnp.max(kn.reshape(B, nq, tq, DIFF_HEADS), axis=2)
    kmax = jnp.max(kn.reshape(B, nkt, tk, DIFF_HEADS), axis=2)
    bound = qmax[:, :, None, :] * (kmax[:, None, :, :] + kself[:, :, None, :]) * 1.001 + DIFF_SKIP_LOG2
    q0 = np.arange(nq)[:, None] * tq
    k0 = np.arange(nkt)[None, :] * tk
    dist = np.maximum(0, np.maximum(q0 - (k0 + tk - 1), k0 - (q0 + tq - 1))).astype(np.float32)
    slopes = np.array([_alibi_slope_log2(h) for h in range(DIFF_HEADS)], np.float32)
    keep = bound > jnp.asarray(dist)[None, :, :, None] * jnp.asarray(slopes)[None, None, None, :]
    keep = jnp.transpose(keep, (0, 3, 1, 2))
    first = jnp.argmax(keep, axis=-1).astype(jnp.int32)
    last = (nkt - 1 - jnp.argmax(keep[..., ::-1], axis=-1)).astype(jnp.int32)
    return first.reshape(-1), last.reshape(-1)


def _diff_branch(p, ktab, qtab, bd32, lam_params, norm_g, lambda_init, B, N, tq=512, tk=2048):
    T = B * N
    nq, nk = N // tq, N // tk
    M = 2 * tq
    tkb = FLASH_TKB
    kd, vd, nrm = _diff_prep(p, bd32, B, N)
    first, last = _diff_bands(nrm, B, N, tq, tk)
    kmax = jnp.max(nrm[:, DIFF_HEADS:2 * DIFF_HEADS].reshape(B, N, DIFF_HEADS), axis=1)
    kmax = jnp.broadcast_to(kmax.reshape(B * DIFF_HEADS, 1, 1), (B * DIFF_HEADS, 1, LANE))

    def kstep(b, h, i, j, lo, hi):
        idx = (b * DIFF_HEADS + h) * nq + i
        return jnp.minimum(jnp.maximum(j, lo[idx]), hi[idx])

    return pl.pallas_call(
        functools.partial(_diff_flash_kernel, lambda_init=lambda_init),
        grid_spec=pltpu.PrefetchScalarGridSpec(
            num_scalar_prefetch=2,
            grid=(B, DIFF_HEADS, nq, nk),
            in_specs=[
                pl.BlockSpec((tq, LANE), lambda b, h, i, j, lo, hi: (b * nq + i, DF_Q // LANE + h)),
                pl.BlockSpec((tq, LANE), lambda b, h, i, j, lo, hi: (h * nq + i, 0)),
                pl.BlockSpec((None, tk, LANE), lambda b, h, i, j, lo, hi: (h, b * nk + kstep(b, h, i, j, lo, hi), 0)),
                pl.BlockSpec((tk, LANE), lambda b, h, i, j, lo, hi: (h * nk + kstep(b, h, i, j, lo, hi), 0)),
                pl.BlockSpec((None, tk, LANE), lambda b, h, i, j, lo, hi: (h, b * nk + kstep(b, h, i, j, lo, hi), 0)),
                pl.BlockSpec((None, 1, LANE), lambda b, h, i, j, lo, hi: (b * DIFF_HEADS + h, 0, 0)),
                pl.BlockSpec((4, DIFF_HD), lambda b, h, i, j, lo, hi: (0, 0)),
                pl.BlockSpec((1, LANE), lambda b, h, i, j, lo, hi: (0, h)),
            ],
            out_specs=pl.BlockSpec((tq, LANE), lambda b, h, i, j, lo, hi: (b * nq + i, h)),
            scratch_shapes=[
                pltpu.VMEM((2, M, 2 * LANE), BF16),
                pltpu.VMEM((1, M), F32),
                pltpu.VMEM((1, M), F32),
                pltpu.VMEM((1, M), F32),
                pltpu.VMEM((LANE, M), F32),
                pltpu.VMEM((2, tkb, M), F32),
                pltpu.VMEM((2, tkb, M), BF16),
                pltpu.VMEM((1, M), F32),
                pltpu.SMEM((1,), jnp.int32),
            ],
        ),
        out_shape=jax.ShapeDtypeStruct((T, BRANCH_W), BF16),
        compiler_params=_cparams(("parallel", "parallel", "parallel", "arbitrary")),
        name="diff_flash",
    )(first, last, p, qtab, kd, ktab, vd, kmax, lam_params, norm_g)


SSD_HALO = 16


def _ssd_prep_kernel(x_ref, xp_ref, xn_ref, bc_ref, bcp_ref, bcn_ref, sm_ref, cw_ref, cb_ref,
                     dtb_ref, arow_ref, xo_ref, bco_ref, dto_ref):
    tc = x_ref.shape[0]
    c = pl.program_id(1)
    nc = pl.num_programs(1)
    has_prev = jnp.where(c > 0, 1.0, 0.0)
    has_next = jnp.where(c < nc - 1, 1.0, 0.0)
    pad = SSD_CONV // 2

    def conv(main_ref, prev_ref, next_ref, off):
        xf = jnp.concatenate([prev_ref[...].astype(F32) * has_prev, main_ref[...].astype(F32),
                              next_ref[...].astype(F32) * has_next], axis=0)
        n = xf.shape[0]
        acc = jnp.zeros((tc, xf.shape[1]), F32) + cb_ref[:, off:off + xf.shape[1]]
        for j in range(SSD_CONV):
            sh = pltpu.roll(xf, (pad - j) % n, axis=0)[SSD_HALO:SSD_HALO + tc, :]
            acc = acc + sh * cw_ref[j:j + 1, off:off + xf.shape[1]]
        return _silu(acc)

    xo_ref[...] = conv(x_ref, xp_ref, xn_ref, 0).astype(BF16)
    bco_ref[...] = conv(bc_ref, bcp_ref, bcn_ref, BRANCH_W).astype(BF16)
    dt = _softplus(sm_ref[...] + dtb_ref[...])
    a = pltpu.roll(dt * arow_ref[...], LANE - 16, axis=1)
    lane = lax.broadcasted_iota(jnp.int32, (tc, LANE), 1)
    dto_ref[...] = jnp.where(lane < 16, a, jnp.where(lane < 32, dt, 0.0))


def _ssd_prep(p, small, conv_w, conv_b, dtb_row, a_row, B, N, tc=512):
    T = B * N
    nc = N // tc
    hb = tc // SSD_HALO
    W = BRANCH_W

    def main(col):
        return pl.BlockSpec((tc, W), lambda b, c: (b * nc + c, col))

    def prev(col):
        return pl.BlockSpec((SSD_HALO, W), lambda b, c: (jnp.maximum((b * nc + c) * hb - 1, 0), col))

    def nxt(col):
        return pl.BlockSpec((SSD_HALO, W), lambda b, c: (jnp.minimum((b * nc + c + 1) * hb, T // SSD_HALO - 1), col))

    cx, cbc = SS_X // W, SS_BC // W
    return pl.pallas_call(
        _ssd_prep_kernel,
        grid=(B, nc),
        in_specs=[
            main(cx), prev(cx), nxt(cx), main(cbc), prev(cbc), nxt(cbc),
            pl.BlockSpec((tc, SMALL_COLS), lambda b, c: (b * nc + c, 0)),
            pl.BlockSpec((SSD_CONV, 2 * W), lambda b, c: (0, 0)),
            pl.BlockSpec((1, 2 * W), lambda b, c: (0, 0)),
            pl.BlockSpec((1, LANE), lambda b, c: (0, 0)),
            pl.BlockSpec((1, LANE), lambda b, c: (0, 0)),
        ],
        out_specs=[
            pl.BlockSpec((tc, W), lambda b, c: (b * nc + c, 0)),
            pl.BlockSpec((tc, W), lambda b, c: (b * nc + c, 0)),
            pl.BlockSpec((tc, LANE), lambda b, c: (b * nc + c, 0)),
        ],
        out_shape=[
            jax.ShapeDtypeStruct((T, W), BF16),
            jax.ShapeDtypeStruct((T, W), BF16),
            jax.ShapeDtypeStruct((T, LANE), F32),
        ],
        compiler_params=_cparams(("parallel", "parallel")),
        name="ssd_prep",
    )(p, p, p, p, p, p, small, conv_w, conv_b, dtb_row, a_row)


def _ssd_kernel(*refs, reverse, final):
    x_ref = refs[0]
    hs_scr = refs[-1]

    @pl.when(pl.program_id(1) == 0)
    def _():
        hs_scr[...] = jnp.zeros_like(hs_scr)

    for bb in range(x_ref.shape[0]):
        _ssd_chunk(bb, refs, reverse, final)


def _ssd_chunk(bb, refs, reverse, final):
    if final:
        (x_ref, bc_ref, dta_ref, ea_ref, ed_ref, z_ref, yb_ref, dsk_ref, ng_ref,
         out_ref, hs_scr) = refs
    else:
        x_ref, bc_ref, dta_ref, ea_ref, ed_ref, out_ref, hs_scr = refs
    L = CHUNK
    S = SSD_STATE
    W = BRANCH_W
    GW = W // SSD_GROUPS
    mask = _tri_masks(L, reverse)
    tri = mask.astype(F32)
    dta = dta_ref[bb]
    acs = _dot_hi(tri, dta)
    acs_t = acs.T
    acs_e = _dot_hi(acs, ea_ref[...])
    dt_e = _dot_hi(dta, ed_ref[...])
    edge = 0 if reverse else L - 1
    a_off = 8 if reverse else 0
    acs_end = acs_e[edge:edge + 1, :]
    xf = x_ref[bb].astype(F32)
    xdt = xf * dt_e
    xdt_b = xdt.astype(BF16)
    xdend = (xdt * jnp.exp(acs_end - acs_e)).astype(BF16)
    e_acs = jnp.exp(acs_e)
    cdec = jnp.exp(acs_end)
    lane = lax.broadcasted_iota(jnp.int32, (L, LANE), 1)
    lo = lane < SSD_HD
    zero = jnp.zeros((L, LANE), BF16)

    ys = []
    for g in range(SSD_GROUPS):
        bg = bc_ref[bb, :, g * S:(g + 1) * S]
        cg = bc_ref[bb, :, SSD_GROUPS * S + g * S:SSD_GROUPS * S + (g + 1) * S]
        cb = _dot_nt(cg, bg)
        hs = hs_scr[bb * SSD_GROUPS + g]
        y_off = _dot(cg, hs.astype(BF16)) * e_acs[:, g * GW:(g + 1) * GW]
        hs_scr[bb * SSD_GROUPS + g] = (cdec[:, g * GW:(g + 1) * GW] * hs
                                       + _dot_tn(bg, xdend[:, g * GW:(g + 1) * GW]))
        for pr in range(GW // LANE):
            col = g * GW + pr * LANE
            xp = xdt_b[:, col:col + LANE]
            yd = jnp.zeros((L, LANE), F32)
            for hh in range(2):
                hd = (col // SSD_HD) + hh
                a_col = acs[:, a_off + hd:a_off + hd + 1]
                a_row = acs_t[a_off + hd:a_off + hd + 1, :]
                decay = jnp.exp(jnp.where(mask, a_col - a_row, -jnp.inf))
                wm = (cb * decay).astype(BF16)
                xm = jnp.where(lo, xp, zero) if hh == 0 else jnp.where(lo, zero, xp)
                yd = yd + _dot(wm, xm)
            ys.append(yd + y_off[:, pr * LANE:(pr + 1) * LANE])
    y = jnp.concatenate(ys, axis=1)

    if not final:
        out_ref[bb] = y
    else:
        y = y + yb_ref[bb] + dsk_ref[...] * xf
        y = y * _silu(z_ref[bb].astype(F32))
        outs = []
        for g in range(SSD_GROUPS):
            yg = y[:, g * GW:(g + 1) * GW]
            ms = jnp.mean(yg * yg, axis=1, keepdims=True)
            outs.append(yg * lax.rsqrt(ms + NORM_EPS))
        out_ref[bb] = (jnp.concatenate(outs, axis=1) * ng_ref[...]).astype(BF16)


def _ssd_dir(xs, bc, dta, ea, ed, B, N, reverse, p3=None, yb=None, dskip_row=None, norm_g=None):
    L = CHUNK
    nc = N // L
    nbat = SCAN_NBAT
    W = BRANCH_W
    final = yb is not None

    def crow(c):
        return (nc - 1 - c) if reverse else c

    in_specs = [
        pl.BlockSpec((nbat, L, W), lambda b, c: (b, crow(c), 0)),
        pl.BlockSpec((nbat, L, W), lambda b, c: (b, crow(c), 0)),
        pl.BlockSpec((nbat, L, LANE), lambda b, c: (b, crow(c), 0)),
        pl.BlockSpec((LANE, W), lambda b, c: (0, 0)),
        pl.BlockSpec((LANE, W), lambda b, c: (0, 0)),
    ]
    args = [xs, bc, dta, ea, ed]
    if final:
        in_specs += [
            pl.BlockSpec((nbat, L, W), lambda b, c: (b, crow(c), SS_Z // W)),
            pl.BlockSpec((nbat, L, W), lambda b, c: (b, crow(c), 0)),
            pl.BlockSpec((1, W), lambda b, c: (0, 0)),
            pl.BlockSpec((1, W), lambda b, c: (0, 0)),
        ]
        args += [p3, yb, dskip_row, norm_g]
    return pl.pallas_call(
        functools.partial(_ssd_kernel, reverse=reverse, final=final),
        grid=(B // nbat, nc),
        in_specs=in_specs,
        out_specs=pl.BlockSpec((nbat, L, W), lambda b, c: (b, crow(c), 0)),
        out_shape=jax.ShapeDtypeStruct((B, N, W), BF16 if final else F32),
        scratch_shapes=[pltpu.VMEM((nbat * SSD_GROUPS, SSD_STATE, W // SSD_GROUPS), F32)],
        compiler_params=_cparams(("parallel", "arbitrary")),
        name="ssd_bwd" if reverse else "ssd_fwd",
    )(*args)


def _ssd_expand_mats():
    ea_f = np.zeros((LANE, BRANCH_W), np.float32)
    ea_b = np.zeros((LANE, BRANCH_W), np.float32)
    ed_f = np.zeros((LANE, BRANCH_W), np.float32)
    ed_b = np.zeros((LANE, BRANCH_W), np.float32)
    for h in range(SSD_HEADS):
        ea_f[h, h * SSD_HD:(h + 1) * SSD_HD] = 1.0
        ea_b[8 + h, h * SSD_HD:(h + 1) * SSD_HD] = 1.0
        ed_f[16 + h, h * SSD_HD:(h + 1) * SSD_HD] = 1.0
        ed_b[24 + h, h * SSD_HD:(h + 1) * SSD_HD] = 1.0
    return ea_f, ea_b, ed_f, ed_b


_EA_F, _EA_B, _ED_F, _ED_B = _ssd_expand_mats()


def _ssd_branch(p, small, conv_w, conv_b, dtb_row, a_row, dskip_row, norm_g, B, N):
    xs, bc, dta = _ssd_prep(p, small, conv_w, conv_b, dtb_row, a_row, B, N)
    xs, bc, dta = (a.reshape(B, N, a.shape[-1]) for a in (xs, bc, dta))
    yb = _ssd_dir(xs, bc, dta, jnp.asarray(_EA_B), jnp.asarray(_ED_B), B, N, reverse=True)
    y = _ssd_dir(xs, bc, dta, jnp.asarray(_EA_F), jnp.asarray(_ED_F), B, N, reverse=False,
                 p3=p.reshape(B, N, P_PAD), yb=yb, dskip_row=dskip_row, norm_g=norm_g)
    return y.reshape(B * N, BRANCH_W)


def _merge_kernel(h_ref, y0_ref, y1_ref, y2_ref, y3_ref, wg_ref, wb_ref, o_ref):
    h = h_ref[...]
    acc = None
    for i, y_ref in enumerate((y0_ref, y1_ref, y2_ref, y3_ref)):
        gate = _sigmoid(_dot(h, wg_ref[i]))
        term = gate * _dot(y_ref[...], wb_ref[i])
        acc = term if acc is None else acc + term
    o_ref[...] = acc.astype(BF16)


def _merge(h, ys, wg, wb, tm=1024, tn=256):
    T = h.shape[0]
    return pl.pallas_call(
        _merge_kernel,
        grid=(T // tm, D_MODEL // tn),
        in_specs=[pl.BlockSpec((tm, D_MODEL), lambda i, j: (i, 0))]
        + [pl.BlockSpec((tm, BRANCH_W), lambda i, j: (i, 0))] * 4
        + [
            pl.BlockSpec((4, D_MODEL, tn), lambda i, j: (0, 0, j)),
            pl.BlockSpec((4, BRANCH_W, tn), lambda i, j: (0, 0, j)),
        ],
        out_specs=pl.BlockSpec((tm, tn), lambda i, j: (i, j)),
        out_shape=jax.ShapeDtypeStruct((T, D_MODEL), BF16),
        compiler_params=_cparams(("parallel", "arbitrary")),
        name="merge",
    )(h, *ys, wg, wb)


def _outproj_kernel(x_ref, m_ref, wo_ref, g_ref, wr_ref, xo_ref, h2_ref, r_ref):
    x = x_ref[...] + _dot(m_ref[...], wo_ref[...])
    xo_ref[...] = x
    ms = jnp.mean(x * x, axis=-1, keepdims=True)
    h2 = x * lax.rsqrt(ms + NORM_EPS) * g_ref[...]
    h2_ref[...] = h2.astype(BF16)
    logits = _dot_hi(h2, wr_ref[...])
    tm = x.shape[0]
    lane = lax.broadcasted_iota(jnp.int32, (tm, LANE), 1).astype(F32)
    neg = -jnp.inf
    big = float(LANE)
    gl = jnp.where(lane < MOE_GROUPS, logits, neg)
    gmax = jnp.max(gl, axis=1, keepdims=True)
    g_idx = jnp.min(jnp.where(gl == gmax, lane, big), axis=1, keepdims=True)
    g_w = 1.0 / jnp.sum(jnp.exp(gl - gmax), axis=1, keepdims=True)
    e_lo = MOE_GROUPS + g_idx * MOE_EPG
    el = jnp.where(lane >= e_lo, jnp.where(lane < e_lo + MOE_EPG, logits, neg), neg)
    m1 = jnp.max(el, axis=1, keepdims=True)
    i1 = jnp.min(jnp.where(el == m1, lane, big), axis=1, keepdims=True)
    el2 = jnp.where(lane == i1, neg, el)
    m2 = jnp.max(el2, axis=1, keepdims=True)
    i2 = jnp.min(jnp.where(el2 == m2, lane, big), axis=1, keepdims=True)
    den = jnp.sum(jnp.exp(el - m1), axis=1, keepdims=True)
    p1 = 1.0 / den
    p2 = jnp.exp(m2 - m1) / den
    w1 = p1 / (p1 + p2) * g_w
    w2 = p2 / (p1 + p2) * g_w
    e1 = i1 - MOE_GROUPS
    e2 = i2 - MOE_GROUPS
    r_ref[...] = jnp.where(lane == 0, e1, jnp.where(lane == 1, e2, jnp.where(lane == 2, w1, jnp.where(lane == 3, w2, 0.0))))


def _out_proj(x, merged, wo, g, wr, tm=512):
    T = x.shape[0]
    return pl.pallas_call(
        _outproj_kernel,
        grid=(T // tm,),
        in_specs=[
            pl.BlockSpec((tm, D_MODEL), lambda i: (i, 0)),
            pl.BlockSpec((tm, D_MODEL), lambda i: (i, 0)),
            pl.BlockSpec((D_MODEL, D_MODEL), lambda i: (0, 0)),
            pl.BlockSpec((1, D_MODEL), lambda i: (0, 0)),
            pl.BlockSpec((D_MODEL, LANE), lambda i: (0, 0)),
        ],
        out_specs=[
            pl.BlockSpec((tm, D_MODEL), lambda i: (i, 0)),
            pl.BlockSpec((tm, D_MODEL), lambda i: (i, 0)),
            pl.BlockSpec((tm, LANE), lambda i: (i, 0)),
        ],
        out_shape=[
            jax.ShapeDtypeStruct((T, D_MODEL), F32),
            jax.ShapeDtypeStruct((T, D_MODEL), BF16),
            jax.ShapeDtypeStruct((T, LANE), F32),
        ],
        compiler_params=_cparams(("parallel",)),
        name="out_proj_router",
    )(x, merged, wo, g, wr)


def _moe_ffn_kernel(be_ref, nu_ref, xs_ref, w1_ref, w3_ref, w2_ref, ys_ref):
    used = pl.program_id(0) < nu_ref[0]

    @pl.when(used)
    def _():
        xb = xs_ref[...]
        a = _dot(xb, w1_ref[0].astype(BF16))
        u = _dot(xb, w3_ref[0].astype(BF16))
        hmid = (_silu(a) * u).astype(BF16)
        ys_ref[...] = _dot(hmid, w2_ref[0].astype(BF16)).astype(BF16)

    @pl.when(jnp.logical_not(used))
    def _():
        ys_ref[...] = jnp.zeros_like(ys_ref)


def _moe_ffn(xs, blk_e, nused, w1, w3, w2):
    R = xs.shape[0]
    nblk = R // MOE_BLK
    return pl.pallas_call(
        _moe_ffn_kernel,
        grid_spec=pltpu.PrefetchScalarGridSpec(
            num_scalar_prefetch=2,
            grid=(nblk,),
            in_specs=[
                pl.BlockSpec((MOE_BLK, D_MODEL), lambda b, be, nu: (b, 0)),
                pl.BlockSpec((1, D_MODEL, MOE_FF), lambda b, be, nu: (be[b], 0, 0)),
                pl.BlockSpec((1, D_MODEL, MOE_FF), lambda b, be, nu: (be[b], 0, 0)),
                pl.BlockSpec((1, MOE_FF, D_MODEL), lambda b, be, nu: (be[b], 0, 0)),
            ],
            out_specs=pl.BlockSpec((MOE_BLK, D_MODEL), lambda b, be, nu: (b, 0)),
        ),
        out_shape=jax.ShapeDtypeStruct((R, D_MODEL), BF16),
        compiler_params=_cparams(("arbitrary",)),
        name="moe_ffn",
    )(blk_e, nused, xs, w1, w3, w2)


def _combine_kernel(x_ref, a_ref, b_ref, r_ref, g_ref, o_ref, *, final):
    r = r_ref[...]
    x = x_ref[...] + (r[:, 2:3] * a_ref[...].astype(F32) + r[:, 3:4] * b_ref[...].astype(F32))
    if final:
        ms = jnp.mean(x * x, axis=-1, keepdims=True)
        x = x * lax.rsqrt(ms + NORM_EPS) * g_ref[...]
    o_ref[...] = x


def _combine(x, a, b, route, g, final, row0, tm=512):
    T = x.shape[0]
    off = row0 // tm
    return pl.pallas_call(
        functools.partial(_combine_kernel, final=final),
        grid=(T // tm,),
        in_specs=[pl.BlockSpec((tm, D_MODEL), lambda i: (i, 0))]
        + [pl.BlockSpec((tm, D_MODEL), lambda i: (i + off, 0))] * 2
        + [pl.BlockSpec((tm, LANE), lambda i: (i + off, 0)), pl.BlockSpec((1, D_MODEL), lambda i: (0, 0))],
        out_specs=pl.BlockSpec((tm, D_MODEL), lambda i: (i, 0)),
        out_shape=jax.ShapeDtypeStruct((T, D_MODEL), F32),
        compiler_params=_cparams(("parallel",)),
        name="moe_combine",
    )(x, a, b, route, g)


def _moe(xs_groups, h2, route, w1, w3, w2, layer, g_final, final):
    T = h2.shape[0]
    K = 2
    e_idx = route[:, 0:K].astype(jnp.int32)
    flat_e = e_idx.reshape(-1)
    onehot = (flat_e[:, None] == jnp.arange(MOE_EXPERTS)[None, :]).astype(jnp.int32)
    csum = jnp.cumsum(onehot, axis=0)
    rank = jnp.take_along_axis(csum, flat_e[:, None], axis=1)[:, 0] - 1
    counts = csum[-1]
    pcounts = (counts + MOE_BLK - 1) // MOE_BLK * MOE_BLK
    pends = jnp.cumsum(pcounts)
    pstarts = pends - pcounts
    dest = pstarts[flat_e] + rank
    R = T * K + MOE_EXPERTS * MOE_BLK
    nblk = R // MOE_BLK
    row_src = jnp.zeros((R,), jnp.int32).at[dest].set(jnp.arange(T * K, dtype=jnp.int32) // K)
    nused = (pends[-1] // MOE_BLK).astype(jnp.int32)
    blk_start = jnp.arange(nblk, dtype=jnp.int32) * MOE_BLK
    blk_e = jnp.minimum(jnp.sum((pends[None, :] <= blk_start[:, None]).astype(jnp.int32), axis=1), MOE_EXPERTS - 1)
    last_e = blk_e[jnp.maximum(nused - 1, 0)]
    blk_e = jnp.where(jnp.arange(nblk) < nused, blk_e, last_e)
    xs = jnp.take(h2, row_src, axis=0, mode="clip")
    ys = _moe_ffn(xs, blk_e + layer * MOE_EXPERTS, nused.reshape(1), w1, w3, w2)
    pos = dest.reshape(T, K)
    ga = jnp.take(ys, pos[:, 0], axis=0, mode="clip")
    gb = jnp.take(ys, pos[:, 1], axis=0, mode="clip")
    outs, row0 = [], 0
    for x in xs_groups:
        outs.append(_combine(x, ga, gb, route, g_final, final, row0))
        row0 += x.shape[0]
    return outs


def _prep_layer(l, w_in, mlstm_gate_bias, ssd_conv_w, ssd_conv_b, ssd_dt_bias, ssd_a_log, ssd_d,
                gqa_q_norm, gqa_k_norm, w_router_group, w_router_expert):
    w = w_in[l]
    wm = jnp.take(w, jnp.asarray(_MAIN_IDX), axis=1)
    dscale = jnp.ones((P_COLS,), F32).at[DF_Q:DF_Q + BRANCH_W].set((DIFF_HD ** -0.5) * LOG2E)
    wm = jnp.pad((wm * dscale[None, :]).astype(BF16), ((0, 0), (0, P_PAD - P_COLS)))
    ws = jnp.zeros((D_MODEL, SMALL_COLS), F32).at[:, :32].set(jnp.take(w, jnp.asarray(_SMALL_IDX), axis=1)).astype(BF16)
    gb_row = jnp.zeros((1, SMALL_COLS), F32).at[0, :16].set(mlstm_gate_bias[l])
    dtb_row = jnp.zeros((1, LANE), F32).at[0, 16:32].set(ssd_dt_bias[l].reshape(-1))
    a_row = jnp.zeros((1, LANE), F32).at[0, 16:32].set((-jnp.exp(ssd_a_log[l])).reshape(-1))
    dskip_row = jnp.repeat(ssd_d[l], SSD_HD)[None, :]
    qg = jnp.tile(gqa_q_norm[l][jnp.asarray(_DEINT64)], LANE // GQA_HD)[None, :]
    kg = jnp.tile(gqa_k_norm[l][jnp.asarray(_DEINT64)], LANE // GQA_HD)[None, :]
    wr = jnp.zeros((D_MODEL, LANE), F32).at[:, :MOE_GROUPS].set(w_router_group[l])
    wr = wr.at[:, MOE_GROUPS:MOE_GROUPS + MOE_EXPERTS].set(w_router_expert[l])
    return dict(wm=wm, ws=ws, gb_row=gb_row, dtb_row=dtb_row, a_row=a_row, dskip_row=dskip_row,
                qg=qg, kg=kg, wr=wr, conv_w=ssd_conv_w[l], conv_b=ssd_conv_b[l][None, :])


def _bd_ones():
    i = np.arange(LANE)
    return (i[:, None] // GQA_HD == i[None, :] // GQA_HD).astype(np.float32)


def _mixer_layer(x, B, N, l, lp, prm, tabs):
    cos_t, sin_t, ktab, qtab = tabs
    bd32 = jnp.asarray(_bd_ones())
    bd = bd32.astype(BF16)
    lambda_init = 0.8 - 0.6 * math.exp(-0.3 * l)
    p, h, small = _in_proj(x, prm["norm_mix"][l][None, :], lp["wm"], lp["ws"])
    y0 = _mlstm_branch(p, small, lp["gb_row"], prm["mlstm_norm"][l][None, :], B, N)
    y1 = _gqa_branch(p, cos_t, sin_t, lp["qg"], lp["kg"], bd, B, N)
    y2 = _ssd_branch(p, small, lp["conv_w"], lp["conv_b"], lp["dtb_row"], lp["a_row"], lp["dskip_row"],
                     prm["ssd_norm"][l][None, :], B, N)
    y3 = _diff_branch(p, ktab, qtab, bd32, prm["diff_lambda"][l], prm["diff_norm"][l][None, :], lambda_init, B, N)
    merged = _merge(h, (y0, y1, y2, y3), lp["wg"], lp["wb"])
    return _out_proj(x, merged, lp["wo"], prm["norm_ffn"][l][None, :], lp["wr"])


def _model(x_groups, prm, L):
    dims = [(x.shape[0], x.shape[1]) for x in x_groups]
    xs = [x.reshape(B * N, D_MODEL) for x, (B, N) in zip(x_groups, dims)]
    tabs = [_rope_tables(N) + _alibi_tables(N) for _, N in dims]
    for l in range(DEPTH):
        mixed = [_mixer_layer(x, B, N, l, L[l], prm, tb) for x, (B, N), tb in zip(xs, dims, tabs)]
        h2cat, rcat = (jnp.concatenate([m[i] for m in mixed], axis=0) for i in (1, 2))
        xs = _moe([m[0] for m in mixed], h2cat, rcat, prm["w1"], prm["w3"], prm["w2"], l,
                  prm["norm_final"][None, :], final=(l == DEPTH - 1))
    return tuple(x.reshape(B, N, D_MODEL) for x, (B, N) in zip(xs, dims))


def kernel(x_prompt, x_sample, norm_mix, w_in, mlstm_gate_bias, mlstm_norm, gqa_q_norm, gqa_k_norm, ssd_conv_w, ssd_conv_b, ssd_dt_bias, ssd_a_log, ssd_d, ssd_norm, diff_lambda, diff_norm, w_branch, w_gate, w_out, norm_ffn, w_router_group, w_router_expert, moe_w_gate, moe_w_up, moe_w_down, norm_final):
    prm = dict(norm_mix=norm_mix, mlstm_norm=mlstm_norm, ssd_norm=ssd_norm, diff_lambda=diff_lambda,
               diff_norm=diff_norm, norm_ffn=norm_ffn, norm_final=norm_final)
    layers = []
    for l in range(DEPTH):
        lp = _prep_layer(l, w_in, mlstm_gate_bias, ssd_conv_w, ssd_conv_b, ssd_dt_bias, ssd_a_log, ssd_d,
                         gqa_q_norm, gqa_k_norm, w_router_group, w_router_expert)
        lp["wg"] = w_gate[l].astype(BF16)
        lp["wb"] = w_branch[l].astype(BF16)
        lp["wo"] = w_out[l].astype(BF16)
        layers.append(lp)
    prm["w1"] = moe_w_gate.reshape(DEPTH * MOE_EXPERTS, D_MODEL, MOE_FF)
    prm["w3"] = moe_w_up.reshape(DEPTH * MOE_EXPERTS, D_MODEL, MOE_FF)
    prm["w2"] = moe_w_down.reshape(DEPTH * MOE_EXPERTS, MOE_FF, D_MODEL)
    return _model([x_prompt, x_sample], prm, layers)
```

```python
import functools
import math

import jax
import jax.numpy as jnp
import numpy as np
from jax import lax
from jax.experimental import pallas as pl
from jax.experimental.pallas import tpu as pltpu

F32 = jnp.float32
BF16 = jnp.bfloat16

D_MODEL = 1024
DEPTH = 2
GRID_W = 64
BRANCH_W = 512
NORM_EPS = 1e-6
MLSTM_HEADS = 4
MLSTM_HD = 128
GQA_HD = 64
GQA_HEADS = 8
GQA_KV_HEADS = 2
ROPE_THETA = 10000.0
SSD_HD = 64
SSD_HEADS = 8
SSD_GROUPS = 2
SSD_STATE = 128
SSD_CONV = 5
DIFF_HD = 64
DIFF_HEADS = 4
MOE_GROUPS = 4
MOE_EPG = 8
MOE_EXPERTS = 32
MOE_FF = 512
CHUNK = 128
LOG2E = 1.4426950408889634

_MLSTM_COLS = 4 * BRANCH_W + 4 * MLSTM_HEADS
_GQA_COLS = GQA_HEADS * GQA_HD + 2 * GQA_KV_HEADS * GQA_HD
_SSD_XBC = BRANCH_W + 2 * SSD_GROUPS * SSD_STATE
_SSD_COLS = BRANCH_W + _SSD_XBC + 2 * SSD_HEADS
_O0 = _MLSTM_COLS
_O1 = _O0 + _GQA_COLS
_O2 = _O1 + _SSD_COLS

ML_Q, ML_K, ML_V, ML_O = 0, 512, 1024, 1536
GQ_Q = 2048
SS_Z = 2560
DF_Q, DF_K, DF_V = 3072, 3584, 4096
SS_X, SS_BC = 4608, 5120
GQ_KV = 5632
P_COLS = 5888
P_PAD = 6144
SMALL_COLS = 128

LANE = 128
VMEM_LIMIT = 48 * 1024 * 1024

MOE_BLK = 512


def _cparams(sem):
    return pltpu.CompilerParams(dimension_semantics=sem, vmem_limit_bytes=VMEM_LIMIT)


def _dot(a, b):
    return jnp.dot(a, b, preferred_element_type=F32)


def _dot_nt(a, b):
    return lax.dot_general(a, b, (((1,), (1,)), ((), ())), preferred_element_type=F32)


def _dot_tn(a, b):
    return lax.dot_general(a, b, (((0,), (0,)), ((), ())), preferred_element_type=F32)


def _dot_hi(a, b):
    return jnp.dot(a, b, preferred_element_type=F32, precision=lax.Precision.HIGHEST)


def _sigmoid(x):
    return 1.0 / (1.0 + jnp.exp(-x))


def _silu(x):
    return x * _sigmoid(x)


def _log_sigmoid(x):
    return jnp.minimum(x, 0.0) - jnp.log(1.0 + jnp.exp(-jnp.abs(x)))


def _softplus(x):
    return jnp.maximum(x, 0.0) + jnp.log(1.0 + jnp.exp(-jnp.abs(x)))


def _main_col_index():
    gq = np.arange(GQA_HEADS * GQA_HD).reshape(GQA_HEADS, GQA_HD // 2, 2)
    gq = np.concatenate([gq[..., 0], gq[..., 1]], axis=-1).reshape(-1)
    gk = np.arange(GQA_KV_HEADS * GQA_HD).reshape(GQA_KV_HEADS, GQA_HD // 2, 2)
    gk = np.concatenate([gk[..., 0], gk[..., 1]], axis=-1).reshape(-1)
    segs = [
        np.arange(0, 4 * BRANCH_W),
        _O0 + gq,
        _O1 + np.arange(0, BRANCH_W),
        _O2 + np.arange(0, 3 * BRANCH_W),
        _O1 + BRANCH_W + np.arange(0, _SSD_XBC),
        _O0 + GQA_HEADS * GQA_HD + gk,
        _O0 + GQA_HEADS * GQA_HD + GQA_KV_HEADS * GQA_HD + np.arange(GQA_KV_HEADS * GQA_HD),
    ]
    idx = np.concatenate(segs)
    assert idx.shape[0] == P_COLS
    return idx


def _small_col_index():
    return np.concatenate([4 * BRANCH_W + np.arange(4 * MLSTM_HEADS),
                           _O1 + BRANCH_W + _SSD_XBC + np.arange(2 * SSD_HEADS)])


_MAIN_IDX = _main_col_index()
_SMALL_IDX = _small_col_index()
_DEINT64 = np.concatenate([np.arange(0, GQA_HD, 2), np.arange(1, GQA_HD, 2)])


def _inproj_kernel(x_ref, g_ref, w_ref, ws_ref, p_ref, h_ref, s_ref, h_scr):
    @pl.when(pl.program_id(1) == 0)
    def _():
        x = x_ref[...]
        ms = jnp.mean(x * x, axis=-1, keepdims=True)
        h = (x * lax.rsqrt(ms + NORM_EPS) * g_ref[...]).astype(BF16)
        h_scr[...] = h
        h_ref[...] = h
        s_ref[...] = _dot(h, ws_ref[...])

    p_ref[...] = _dot(h_scr[...], w_ref[...]).astype(BF16)


def _in_proj(x, g, w_main, w_small, tm=1024, tn=1024):
    T = x.shape[0]
    return pl.pallas_call(
        _inproj_kernel,
        grid=(T // tm, P_PAD // tn),
        in_specs=[
            pl.BlockSpec((tm, D_MODEL), lambda i, j: (i, 0)),
            pl.BlockSpec((1, D_MODEL), lambda i, j: (0, 0)),
            pl.BlockSpec((D_MODEL, tn), lambda i, j: (0, j)),
            pl.BlockSpec((D_MODEL, SMALL_COLS), lambda i, j: (0, 0)),
        ],
        out_specs=[
            pl.BlockSpec((tm, tn), lambda i, j: (i, j)),
            pl.BlockSpec((tm, D_MODEL), lambda i, j: (i, 0)),
            pl.BlockSpec((tm, SMALL_COLS), lambda i, j: (i, 0)),
        ],
        out_shape=[
            jax.ShapeDtypeStruct((T, P_PAD), BF16),
            jax.ShapeDtypeStruct((T, D_MODEL), BF16),
            jax.ShapeDtypeStruct((T, SMALL_COLS), F32),
        ],
        scratch_shapes=[pltpu.VMEM((tm, D_MODEL), BF16)],
        compiler_params=_cparams(("parallel", "arbitrary")),
        name="in_proj",
    )(x, g, w_main, w_small)


def _tri_masks(L, reverse):
    r = lax.broadcasted_iota(jnp.int32, (L, L), 0)
    c = lax.broadcasted_iota(jnp.int32, (L, L), 1)
    return (c >= r) if reverse else (c <= r)


def _mlstm_kernel(*refs, reverse, final):
    q_ref = refs[0]
    ct_scr, n_scr, m_scr = refs[-3:]

    @pl.when(pl.program_id(1) == 0)
    def _():
        ct_scr[...] = jnp.zeros_like(ct_scr)
        n_scr[...] = jnp.zeros_like(n_scr)
        m_scr[...] = jnp.zeros_like(m_scr)

    _mlstm_step(refs, reverse, final)


def _mlstm_step(refs, reverse, final):
    if final:
        (q_ref, k_ref, v_ref, sm_ref, gb_ref, o_ref, hb_ref, ng_ref,
         out_ref, ct_scr, n_scr, m_scr) = refs
    else:
        q_ref, k_ref, v_ref, sm_ref, gb_ref, out_ref, ct_scr, n_scr, m_scr = refs
    L = CHUNK
    H = MLSTM_HEADS
    d = MLSTM_HD
    nbat = q_ref.shape[0]
    pairs = [(bb, h) for bb in range(nbat) for h in range(H)]
    mask = _tri_masks(L, reverse)
    tri = mask.astype(F32)
    i_off = 8 if reverse else 0
    f_off = i_off + 4
    edge = 0 if reverse else L - 1
    scale = d ** -0.5
    lane = lax.broadcasted_iota(jnp.int32, (1, LANE), 1)

    gates = [sm_ref[bb] + gb_ref[...] for bb in range(nbat)]
    bcum = [_dot_hi(tri, _log_sigmoid(g)) for g in gates]
    gates_t = [g.T for g in gates]
    bcum_t = [b.T for b in bcum]
    m_all = [m_scr[bb] for bb in range(nbat)]

    def cols(h):
        return slice(h * d, (h + 1) * d)

    b_col, i_col, m_prev, m_t, w_intra, w_inter = {}, {}, {}, {}, {}, {}
    for pr in pairs:
        bb, h = pr
        b_col[pr] = bcum[bb][:, f_off + h:f_off + h + 1]
        i_col[pr] = gates[bb][:, i_off + h:i_off + h + 1]
        a_row = gates_t[bb][i_off + h:i_off + h + 1, :] - bcum_t[bb][f_off + h:f_off + h + 1, :]
        m_prev[pr] = m_all[bb][:, h:h + 1]
        dmat = jnp.where(mask, b_col[pr] + a_row, -jnp.inf)
        inter = b_col[pr] + m_prev[pr]
        m_t[pr] = jnp.maximum(inter, jnp.max(dmat, axis=1, keepdims=True))
        w_intra[pr] = jnp.exp(dmat - m_t[pr])
        w_inter[pr] = jnp.exp(inter - m_t[pr])

    ks, s = {}, {}
    for pr in pairs:
        bb, h = pr
        ks[pr] = (k_ref[bb, :, cols(h)].astype(F32) * scale).astype(BF16)
        s[pr] = _dot_nt(q_ref[bb, :, cols(h)], ks[pr]) * w_intra[pr]

    hs = {}
    for pr in pairs:
        bb, h = pr
        qh = q_ref[bb, :, cols(h)]
        ct = ct_scr[bb * H + h]
        n_row = n_scr[bb * H + h]
        num = _dot(s[pr].astype(BF16), v_ref[bb, :, cols(h)]) + w_inter[pr] * _dot(qh, ct.astype(BF16))
        qn = jnp.sum(qh.astype(F32) * n_row, axis=1, keepdims=True)
        den = jnp.sum(s[pr], axis=1, keepdims=True) + w_inter[pr] * qn
        hs[pr] = num / jnp.maximum(jnp.abs(den), jnp.exp(-m_t[pr]))

    for pr in pairs:
        bb, h = pr
        vh = v_ref[bb, :, cols(h)]
        btot = b_col[pr][edge:edge + 1, :]
        dec = btot - b_col[pr] + i_col[pr]
        m_new = jnp.maximum(btot + m_prev[pr], jnp.max(dec, axis=0, keepdims=True))
        ws = jnp.exp(dec - m_new)
        wc = jnp.exp(btot + m_prev[pr] - m_new)
        wsv = (ws * vh.astype(F32)).astype(BF16)
        ct_scr[bb * H + h] = wc * ct_scr[bb * H + h] + _dot_tn(ks[pr], wsv)
        n_scr[bb * H + h] = wc * n_scr[bb * H + h] + jnp.sum(ws * ks[pr].astype(F32), axis=0, keepdims=True)
        m_all[bb] = jnp.where(lane == h, m_new, m_all[bb])
    for bb in range(nbat):
        m_scr[bb] = m_all[bb]

    for pr in pairs:
        bb, h = pr
        if not final:
            out_ref[bb, :, cols(h)] = hs[pr]
        else:
            hsum = hs[pr] + hb_ref[bb, :, cols(h)]
            ms = jnp.mean(hsum * hsum, axis=1, keepdims=True)
            y = hsum * lax.rsqrt(ms + NORM_EPS) * ng_ref[:, cols(h)]
            o = o_ref[bb, :, cols(h)].astype(F32)
            out_ref[bb, :, cols(h)] = (_sigmoid(o) * y).astype(BF16)


SCAN_NBAT = 2


def _mlstm_dir(p3, small3, gate_bias_row, B, N, reverse, hb=None, norm_g=None):
    L = CHUNK
    nc = N // L
    nbat = SCAN_NBAT
    final = hb is not None

    def crow(c):
        return (nc - 1 - c) if reverse else c

    W = BRANCH_W
    in_specs = [
        pl.BlockSpec((nbat, L, W), lambda b, c: (b, crow(c), ML_Q // W)),
        pl.BlockSpec((nbat, L, W), lambda b, c: (b, crow(c), ML_K // W)),
        pl.BlockSpec((nbat, L, W), lambda b, c: (b, crow(c), ML_V // W)),
        pl.BlockSpec((nbat, L, SMALL_COLS), lambda b, c: (b, crow(c), 0)),
        pl.BlockSpec((1, SMALL_COLS), lambda b, c: (0, 0)),
    ]
    args = [p3, p3, p3, small3, gate_bias_row]
    if final:
        in_specs += [
            pl.BlockSpec((nbat, L, W), lambda b, c: (b, crow(c), ML_O // W)),
            pl.BlockSpec((nbat, L, W), lambda b, c: (b, crow(c), 0)),
            pl.BlockSpec((1, W), lambda b, c: (0, 0)),
        ]
        args += [p3, hb, norm_g]
    return pl.pallas_call(
        functools.partial(_mlstm_kernel, reverse=reverse, final=final),
        grid=(B // nbat, nc),
        in_specs=in_specs,
        out_specs=pl.BlockSpec((nbat, L, W), lambda b, c: (b, crow(c), 0)),
        out_shape=jax.ShapeDtypeStruct((B, N, W), BF16 if final else F32),
        scratch_shapes=[
            pltpu.VMEM((nbat * MLSTM_HEADS, MLSTM_HD, MLSTM_HD), F32),
            pltpu.VMEM((nbat * MLSTM_HEADS, 1, MLSTM_HD), F32),
            pltpu.VMEM((nbat, 1, LANE), F32),
        ],
        compiler_params=_cparams(("parallel", "arbitrary")),
        name="mlstm_bwd" if reverse else "mlstm_fwd",
    )(*args)


def _mlstm_branch(p, small, gate_bias_row, norm_g, B, N):
    p3 = p.reshape(B, N, P_PAD)
    small3 = small.reshape(B, N, SMALL_COLS)
    hb = _mlstm_dir(p3, small3, gate_bias_row, B, N, reverse=True)
    y = _mlstm_dir(p3, small3, gate_bias_row, B, N, reverse=False, hb=hb, norm_g=norm_g)
    return y.reshape(B * N, BRANCH_W)


def _gqa_prep_kernel(q_ref, kv_ref, cos_ref, sin_ref, qg_ref, kg_ref, bd_ref, qo_ref, ko_ref, vo_ref):
    tm = q_ref.shape[0]
    bd = bd_ref[...]
    cos = cos_ref[...]
    sin = sin_ref[...]
    lane = lax.broadcasted_iota(jnp.int32, (tm, LANE), 1)
    first = (lane % GQA_HD) < (GQA_HD // 2)
    lo = lane < GQA_HD

    def norm_rope(x, g):
        ss = _dot((x * x).astype(BF16), bd)
        xn = x * lax.rsqrt(ss * (1.0 / GQA_HD) + NORM_EPS) * g
        partner = jnp.where(first, pltpu.roll(xn, LANE - GQA_HD // 2, axis=1),
                            pltpu.roll(xn, GQA_HD // 2, axis=1))
        return xn * cos + partner * sin

    qscale = (GQA_HD ** -0.5) * LOG2E
    for a in range(GQA_HEADS * GQA_HD // LANE):
        x = q_ref[:, a * LANE:(a + 1) * LANE].astype(F32)
        qo_ref[:, a * LANE:(a + 1) * LANE] = (norm_rope(x, qg_ref[...]) * qscale).astype(BF16)
    k = norm_rope(kv_ref[:, 0:LANE].astype(F32), kg_ref[...])
    k_sw = pltpu.roll(k, GQA_HD, axis=1)
    ko_ref[0] = jnp.where(lo, k, k_sw).astype(BF16)
    ko_ref[1] = jnp.where(lo, k_sw, k).astype(BF16)
    v = kv_ref[:, LANE:2 * LANE].astype(F32)
    v_sw = pltpu.roll(v, GQA_HD, axis=1)
    vo_ref[0] = jnp.where(lo, v, 1.0).astype(BF16)
    vo_ref[1] = jnp.where(lo, v_sw, 1.0).astype(BF16)


def _gqa_prep(p, cos_t, sin_t, qg, kg, bd, B, N, tm=512):
    T = B * N
    nb = N // tm
    return pl.pallas_call(
        _gqa_prep_kernel,
        grid=(T // tm,),
        in_specs=[
            pl.BlockSpec((tm, 512), lambda i: (i, GQ_Q // 512)),
            pl.BlockSpec((tm, 256), lambda i: (i, GQ_KV // 256)),
            pl.BlockSpec((tm, LANE), lambda i: (i % nb, 0)),
            pl.BlockSpec((tm, LANE), lambda i: (i % nb, 0)),
            pl.BlockSpec((1, LANE), lambda i: (0, 0)),
            pl.BlockSpec((1, LANE), lambda i: (0, 0)),
            pl.BlockSpec((LANE, LANE), lambda i: (0, 0)),
        ],
        out_specs=[
            pl.BlockSpec((tm, 512), lambda i: (i, 0)),
            pl.BlockSpec((GQA_KV_HEADS, tm, LANE), lambda i: (0, i, 0)),
            pl.BlockSpec((GQA_KV_HEADS, tm, LANE), lambda i: (0, i, 0)),
        ],
        out_shape=[
            jax.ShapeDtypeStruct((T, 512), BF16),
            jax.ShapeDtypeStruct((GQA_KV_HEADS, T, LANE), BF16),
            jax.ShapeDtypeStruct((GQA_KV_HEADS, T, LANE), BF16),
        ],
        compiler_params=_cparams(("parallel",)),
        name="gqa_prep",
    )(p, p, cos_t, sin_t, qg, kg, bd)


FLASH_TKB = 512


def _softmax_keys(s_ref, p_ref, m_scr, al_scr, l_scr, M):
    tkb = s_ref.shape[0]
    for c in range(M // LANE):
        cols = slice(c * LANE, (c + 1) * LANE)
        mx = jnp.max(jnp.max(s_ref[:, cols].reshape(tkb // 8, 8, LANE), axis=0), axis=0, keepdims=True)
        m_prev = m_scr[:, cols]
        m_new = jnp.maximum(m_prev, mx)
        alpha = jnp.exp2(m_prev - m_new)
        p = jnp.exp2(s_ref[:, cols] - m_new)
        p_ref[:, cols] = p.astype(BF16)
        if l_scr is not None:
            lsum = jnp.sum(jnp.sum(p.reshape(tkb // 8, 8, LANE), axis=0), axis=0, keepdims=True)
            l_scr[:, cols] = alpha * l_scr[:, cols] + lsum
        al_scr[:, cols] = alpha
        m_scr[:, cols] = m_new


def _flash_blocks(nb, qk, pv, softmax, fix=None):
    qk(0)
    for i in range(nb):
        if i + 1 < nb:
            qk(i + 1)
        if fix is not None:
            fix(i)
        softmax(i)
        pv(i)


FLASH_SAFE_LOG2 = 100.0


def _flash_one_pass(nb, scores, values, m_scr, l_scr, acc_scr, p_scr):
    for i in range(nb):
        s = scores(i)
        tkb, M = s.shape
        m_used = m_scr[...]
        p = jnp.exp2(s - m_used)
        p_scr[i % 2] = p.astype(BF16)
        mx = jnp.max(jnp.max(s.reshape(tkb // 8, 8, M), axis=0), axis=0, keepdims=True)
        m_new = jnp.maximum(m_used, mx)
        alpha = jnp.exp2(m_used - m_new)
        if l_scr is not None:
            lsum = jnp.sum(jnp.sum(p.reshape(tkb // 8, 8, M), axis=0), axis=0, keepdims=True)
            l_scr[...] = alpha * (l_scr[...] + lsum)
        acc_scr[...] = alpha * (acc_scr[...] + _dot_tn(values(i), p_scr[i % 2]))
        m_scr[...] = m_new


def _flash_score_bound(qs, kmax):
    qf = qs.astype(F32)
    n2 = _dot_nt(jnp.ones((8, LANE), BF16), (qf * qf).astype(BF16))[0:1, :]
    return jnp.sqrt(n2) * kmax * 1.01 + 1.0


def _flash_is_safe(u_scr, m_scr):
    return jnp.where(jnp.max(u_scr[...] - m_scr[...]) <= FLASH_SAFE_LOG2, 1, 0)


def _gqa_flash_kernel(q_ref, k_ref, v_ref, kmax_ref, o_ref, qs_scr, m_scr, al_scr, acc_scr, s_scr, p_scr,
                      u_scr, safe_ref):
    tq = q_ref.shape[0]
    M = 4 * tq
    tkb = s_scr.shape[1]
    nb = k_ref.shape[0] // tkb
    kj = pl.program_id(3)

    @pl.when(kj == 0)
    def _():
        lane = lax.broadcasted_iota(jnp.int32, (tq, LANE), 1)
        lo = lane < GQA_HD
        zero = jnp.zeros((tq, LANE), BF16)
        for a in range(2):
            qa = q_ref[:, a * LANE:(a + 1) * LANE]
            qs_scr[(2 * a) * tq:(2 * a + 1) * tq, :] = jnp.where(lo, qa, zero)
            qs_scr[(2 * a + 1) * tq:(2 * a + 2) * tq, :] = jnp.where(lo, zero, qa)
        acc_scr[...] = jnp.zeros_like(acc_scr)
        u = _flash_score_bound(qs_scr[...], kmax_ref[:, 0:1])
        u_scr[...] = u
        early = jnp.max(u) <= 0.5 * FLASH_SAFE_LOG2
        safe_ref[0] = jnp.where(early, 2, 0)
        m_scr[...] = jnp.where(early, -u, jnp.full_like(u, -jnp.inf))

    def scores(i):
        return _dot_nt(k_ref[i * tkb:(i + 1) * tkb, :], qs_scr[...])

    def values(i):
        return v_ref[i * tkb:(i + 1) * tkb, :]

    def qk(i):
        s_scr[i % 2] = scores(i)

    def softmax(i):
        _softmax_keys(s_scr.at[i % 2], p_scr.at[i % 2], m_scr, al_scr, None, M)

    def pv(i):
        acc_scr[...] = al_scr[...] * acc_scr[...] + _dot_tn(values(i), p_scr[i % 2])

    one_pass = safe_ref[0] >= 1

    @pl.when(jnp.logical_not(one_pass))
    def _():
        _flash_blocks(nb, qk, pv, softmax)

    @pl.when(one_pass)
    def _():
        _flash_one_pass(nb, scores, values, m_scr, None, acc_scr, p_scr)

    @pl.when(kj == 0)
    def _():
        safe_ref[0] = jnp.maximum(safe_ref[0], _flash_is_safe(u_scr, m_scr))

    @pl.when(kj == pl.num_programs(3) - 1)
    def _():
        inv = 1.0 / acc_scr[GQA_HD:GQA_HD + 1, :]
        ot = jnp.concatenate([acc_scr[0:GQA_HD, h * tq:(h + 1) * tq] * inv[:, h * tq:(h + 1) * tq]
                              for h in range(4)], axis=0)
        o_ref[...] = ot.T.astype(BF16)


def _gqa_flash(qn, k2, va, B, N, tq=256, tk=4096):
    T = B * N
    tk = min(tk, N)
    nq, nk = N // tq, N // tk
    M = 4 * tq
    tkb = FLASH_TKB
    kf = k2[:, :, 0:GQA_HD].astype(F32).reshape(GQA_KV_HEADS, B, N, GQA_HD)
    kmax = jnp.max(jnp.sqrt(jnp.sum(kf * kf, axis=-1)), axis=-1)
    kmax = jnp.broadcast_to(kmax.T.reshape(B * GQA_KV_HEADS, 1, 1), (B * GQA_KV_HEADS, 1, LANE))
    return pl.pallas_call(
        _gqa_flash_kernel,
        grid=(B, GQA_KV_HEADS, nq, nk),
        in_specs=[
            pl.BlockSpec((tq, 256), lambda b, g, i, j: (b * nq + i, g)),
            pl.BlockSpec((None, tk, LANE), lambda b, g, i, j: (g, b * nk + j, 0)),
            pl.BlockSpec((None, tk, LANE), lambda b, g, i, j: (g, b * nk + j, 0)),
            pl.BlockSpec((None, 1, LANE), lambda b, g, i, j: (b * GQA_KV_HEADS + g, 0, 0)),
        ],
        out_specs=pl.BlockSpec((tq, 256), lambda b, g, i, j: (b * nq + i, g)),
        out_shape=jax.ShapeDtypeStruct((T, 512), BF16),
        scratch_shapes=[
            pltpu.VMEM((M, LANE), BF16),
            pltpu.VMEM((1, M), F32),
            pltpu.VMEM((1, M), F32),
            pltpu.VMEM((LANE, M), F32),
            pltpu.VMEM((2, tkb, M), F32),
            pltpu.VMEM((2, tkb, M), BF16),
            pltpu.VMEM((1, M), F32),
            pltpu.SMEM((1,), jnp.int32),
        ],
        compiler_params=_cparams(("parallel", "parallel", "parallel", "arbitrary")),
        name="gqa_flash",
    )(qn, k2, va, kmax)


def _rope_tables(N):
    rows = N // GRID_W
    row = jnp.repeat(jnp.arange(rows, dtype=F32), GRID_W)
    col = (jnp.arange(N) % GRID_W).astype(F32)
    half = GQA_HD // 2
    inv = 1.0 / (ROPE_THETA ** (jnp.arange(0, half, 2, dtype=F32) / half))
    ang = jnp.concatenate([row[:, None] * inv, col[:, None] * inv], axis=-1)
    cos, sin = jnp.cos(ang), jnp.sin(ang)
    cos_h = jnp.concatenate([cos, cos], axis=-1)
    sin_h = jnp.concatenate([-sin, sin], axis=-1)
    return jnp.tile(cos_h, (1, LANE // GQA_HD)), jnp.tile(sin_h, (1, LANE // GQA_HD))


def _gqa_branch(p, cos_t, sin_t, qg, kg, bd, B, N):
    qn, k2, va = _gqa_prep(p, cos_t, sin_t, qg, kg, bd, B, N)
    return _gqa_flash(qn, k2, va, B, N)


DIFF_SKIP_LOG2 = 40.0


def _alibi_slope_log2(h):
    return (2.0 ** (-8.0 * (h + 1) / DIFF_HEADS)) * LOG2E


def _alibi_tables(N):
    pos = np.arange(N)
    hi = ((pos // 128) * 128).astype(np.float32)
    lo = (pos % 128).astype(np.float32)
    kt = np.zeros((DIFF_HEADS, N, LANE), np.float32)
    qt = np.zeros((DIFF_HEADS, N, LANE), np.float32)
    for h in range(DIFF_HEADS):
        s = np.float32(_alibi_slope_log2(h))
        s1 = np.float32(s).astype(BF16).astype(np.float32)
        s2 = np.float32(s - s1).astype(BF16).astype(np.float32)
        for c, val in enumerate((hi, hi, lo, lo)):
            kt[h, :, c] = val
            qt[h, :, 4 + c] = -val
        for c, val in enumerate((s1, s2, s1, s2)):
            kt[h, :, 4 + c] = val
            qt[h, :, c] = val
    return (jnp.asarray(kt.reshape(DIFF_HEADS * N, LANE), BF16), jnp.asarray(qt.reshape(DIFF_HEADS * N, LANE), BF16))


def _diff_flash_kernel(lo_ref, hi_ref, q_ref, qa_ref, k_ref, ka_ref, v_ref, kmax_ref, lp_ref, ng_ref, o_ref,
                       qs_scr, m_scr, l_scr, al_scr, acc_scr, s_scr, p_scr, u_scr, safe_ref, *, lambda_init):
    tq = q_ref.shape[0]
    tk = k_ref.shape[0]
    M = 2 * tq
    tkb = s_scr.shape[1]
    nb = tk // tkb
    b, h, qi, kj = pl.program_id(0), pl.program_id(1), pl.program_id(2), pl.program_id(3)
    idx = (b * pl.num_programs(1) + h) * pl.num_programs(2) + qi
    first, last = lo_ref[idx], hi_ref[idx]

    @pl.when(kj == first)
    def _():
        lane = lax.broadcasted_iota(jnp.int32, (tq, LANE), 1)
        lo = lane < DIFF_HD
        zero = jnp.zeros((tq, LANE), BF16)
        q = q_ref[...]
        qa = qa_ref[...]
        for var, aug in enumerate((qa, -qa)):
            qs_scr[var, 0:tq, 0:LANE] = jnp.where(lo, q, zero)
            qs_scr[var, tq:M, 0:LANE] = jnp.where(lo, zero, q)
            qs_scr[var, 0:tq, LANE:2 * LANE] = aug
            qs_scr[var, tq:M, LANE:2 * LANE] = aug
        l_scr[...] = jnp.zeros_like(l_scr)
        acc_scr[...] = jnp.zeros_like(acc_scr)
        u = _flash_score_bound(qs_scr[0, :, 0:LANE], kmax_ref[:, 0:1])
        u_scr[...] = u
        early = jnp.max(u) <= 0.5 * FLASH_SAFE_LOG2
        safe_ref[0] = jnp.where(early, 2, 0)
        m_scr[...] = jnp.where(early, -u, jnp.full_like(u, -jnp.inf))

    is_left = (kj + 1) * tk <= qi * tq
    is_right = kj * tk >= (qi + 1) * tq
    pure = jnp.logical_or(is_left, is_right)
    active = jnp.logical_and(kj >= first, kj <= last)
    sel = jnp.where(is_right, 1, 0)

    def scores(i):
        rows = slice(i * tkb, (i + 1) * tkb)
        kk = jnp.concatenate([k_ref[rows, :], ka_ref[rows, :]], axis=1)
        return _dot_nt(kk, qs_scr[sel])

    def values(i):
        return v_ref[i * tkb:(i + 1) * tkb, :]

    def qk(i):
        s_scr[i % 2] = scores(i)

    def fix(i):
        slope = jnp.exp2(-2.0 * (jnp.full((1, 1), h, jnp.int32).astype(F32) + 1.0)) * LOG2E
        j = kj * tk + i * tkb + lax.broadcasted_iota(jnp.int32, (tkb, 1), 0)
        col = lax.broadcasted_iota(jnp.int32, (1, M), 1)
        iq = qi * tq + col - jnp.where(col >= tq, tq, 0)
        d = jnp.maximum(j - iq, 0).astype(F32)
        s_scr[i % 2] = s_scr[i % 2] - (2.0 * slope) * d

    def softmax(i):
        _softmax_keys(s_scr.at[i % 2], p_scr.at[i % 2], m_scr, al_scr, l_scr, M)

    def pv(i):
        acc_scr[...] = al_scr[...] * acc_scr[...] + _dot_tn(values(i), p_scr[i % 2])

    @pl.when(jnp.logical_and(active, pure))
    def _():
        safe = safe_ref[0]
        one_pass = jnp.logical_or(safe == 2, jnp.logical_and(safe == 1, kj != first))

        @pl.when(jnp.logical_not(one_pass))
        def _():
            _flash_blocks(nb, qk, pv, softmax)

        @pl.when(one_pass)
        def _():
            _flash_one_pass(nb, scores, values, m_scr, l_scr, acc_scr, p_scr)

    @pl.when(jnp.logical_and(active, jnp.logical_not(pure)))
    def _():
        _flash_blocks(nb, qk, pv, softmax, fix=fix)

    @pl.when(kj == first)
    def _():
        safe_ref[0] = jnp.maximum(safe_ref[0], _flash_is_safe(u_scr, m_scr))

    @pl.when(kj == last)
    def _():
        lp = lp_ref[...]
        s01 = jnp.sum(jnp.sum(lp[0:1] * lp[1:2], axis=1, keepdims=True), axis=0, keepdims=True)
        s23 = jnp.sum(jnp.sum(lp[2:3] * lp[3:4], axis=1, keepdims=True), axis=0, keepdims=True)
        lam = jnp.exp(s01) - jnp.exp(s23) + lambda_init
        inv = 1.0 / l_scr[...]
        ot = acc_scr[:, 0:tq] * inv[:, 0:tq] - lam * (acc_scr[:, tq:M] * inv[:, tq:M])
        o = ot.T
        ms = jnp.mean(o * o, axis=1, keepdims=True)
        o_ref[...] = (o * lax.rsqrt(ms + NORM_EPS) * ng_ref[...] * (1.0 - lambda_init)).astype(BF16)


def _diff_prep_kernel(q_ref, k_ref, v_ref, bd_ref, ko_ref, vo_ref, nrm_ref):
    tm = q_ref.shape[0]
    lane = lax.broadcasted_iota(jnp.int32, (tm, LANE), 1)
    nrm = jnp.zeros((tm, LANE), F32)
    for h in range(DIFF_HEADS):
        cols = slice(h * LANE, (h + 1) * LANE)
        ko_ref[h] = k_ref[:, cols]
        vo_ref[h] = v_ref[:, cols]
        for src, off in ((q_ref, 0), (k_ref, DIFF_HEADS)):
            x = src[:, cols].astype(F32)
            ss = _dot_hi(x * x, bd_ref[...])
            n = jnp.sqrt(jnp.maximum(ss, pltpu.roll(ss, DIFF_HD, axis=1)))
            nrm = jnp.where(lane == off + h, n, nrm)
    nrm_ref[...] = nrm


def _diff_prep(p, bd32, B, N, tm=512):
    T = B * N
    W = BRANCH_W
    return pl.pallas_call(
        _diff_prep_kernel,
        grid=(T // tm,),
        in_specs=[
            pl.BlockSpec((tm, W), lambda i: (i, DF_Q // W)),
            pl.BlockSpec((tm, W), lambda i: (i, DF_K // W)),
            pl.BlockSpec((tm, W), lambda i: (i, DF_V // W)),
            pl.BlockSpec((LANE, LANE), lambda i: (0, 0)),
        ],
        out_specs=[
            pl.BlockSpec((DIFF_HEADS, tm, LANE), lambda i: (0, i, 0)),
            pl.BlockSpec((DIFF_HEADS, tm, LANE), lambda i: (0, i, 0)),
            pl.BlockSpec((tm, LANE), lambda i: (i, 0)),
        ],
        out_shape=[
            jax.ShapeDtypeStruct((DIFF_HEADS, T, LANE), BF16),
            jax.ShapeDtypeStruct((DIFF_HEADS, T, LANE), BF16),
            jax.ShapeDtypeStruct((T, LANE), F32),
        ],
        compiler_params=_cparams(("parallel",)),
        name="diff_prep",
    )(p, p, p, bd32)


def _diff_bands(nrm, B, N, tq, tk):
    nq, nkt = N // tq, N // tk
    qn = nrm[:, 0:DIFF_HEADS].reshape(B, N, DIFF_HEADS)
    kn = nrm[:, DIFF_HEADS:2 * DIFF_HEADS].reshape(B, N, DIFF_HEADS)
    qmax = jnp.max(qn.reshape(B, nq, tq, DIFF_HEADS), axis=2)
    kself = jnp.max(kn.reshape(B, nq, tq, DIFF_HEADS), axis=2)
    kmax = jnp.max(kn.reshape(B, nkt, tk, DIFF_HEADS), axis=2)
    bound = qmax[:, :, None, :] * (kmax[:, None, :, :] + kself[:, :, None, :]) * 1.001 + DIFF_SKIP_LOG2
    q0 = np.arange(nq)[:, None] * tq
    k0 = np.arange(nkt)[None, :] * tk
    dist = np.maximum(0, np.maximum(q0 - (k0 + tk - 1), k0 - (q0 + tq - 1))).astype(np.float32)
    slopes = np.array([_alibi_slope_log2(h) for h in range(DIFF_HEADS)], np.float32)
    keep = bound > jnp.asarray(dist)[None, :, :, None] * jnp.asarray(slopes)[None, None, None, :]
    keep = jnp.transpose(keep, (0, 3, 1, 2))
    first = jnp.argmax(keep, axis=-1).astype(jnp.int32)
    last = (nkt - 1 - jnp.argmax(keep[..., ::-1], axis=-1)).astype(jnp.int32)
    return first.reshape(-1), last.reshape(-1)


def _diff_branch(p, ktab, qtab, bd32, lam_params, norm_g, lambda_init, B, N, tq=512, tk=2048):
    T = B * N
    nq, nk = N // tq, N // tk
    M = 2 * tq
    tkb = FLASH_TKB
    kd, vd, nrm = _diff_prep(p, bd32, B, N)
    first, last = _diff_bands(nrm, B, N, tq, tk)
    kmax = jnp.max(nrm[:, DIFF_HEADS:2 * DIFF_HEADS].reshape(B, N, DIFF_HEADS), axis=1)
    kmax = jnp.broadcast_to(kmax.reshape(B * DIFF_HEADS, 1, 1), (B * DIFF_HEADS, 1, LANE))

    def kstep(b, h, i, j, lo, hi):
        idx = (b * DIFF_HEADS + h) * nq + i
        return jnp.minimum(jnp.maximum(j, lo[idx]), hi[idx])

    return pl.pallas_call(
        functools.partial(_diff_flash_kernel, lambda_init=lambda_init),
        grid_spec=pltpu.PrefetchScalarGridSpec(
            num_scalar_prefetch=2,
            grid=(B, DIFF_HEADS, nq, nk),
            in_specs=[
                pl.BlockSpec((tq, LANE), lambda b, h, i, j, lo, hi: (b * nq + i, DF_Q // LANE + h)),
                pl.BlockSpec((tq, LANE), lambda b, h, i, j, lo, hi: (h * nq + i, 0)),
                pl.BlockSpec((None, tk, LANE), lambda b, h, i, j, lo, hi: (h, b * nk + kstep(b, h, i, j, lo, hi), 0)),
                pl.BlockSpec((tk, LANE), lambda b, h, i, j, lo, hi: (h * nk + kstep(b, h, i, j, lo, hi), 0)),
                pl.BlockSpec((None, tk, LANE), lambda b, h, i, j, lo, hi: (h, b * nk + kstep(b, h, i, j, lo, hi), 0)),
                pl.BlockSpec((None, 1, LANE), lambda b, h, i, j, lo, hi: (b * DIFF_HEADS + h, 0, 0)),
                pl.BlockSpec((4, DIFF_HD), lambda b, h, i, j, lo, hi: (0, 0)),
                pl.BlockSpec((1, LANE), lambda b, h, i, j, lo, hi: (0, h)),
            ],
            out_specs=pl.BlockSpec((tq, LANE), lambda b, h, i, j, lo, hi: (b * nq + i, h)),
            scratch_shapes=[
                pltpu.VMEM((2, M, 2 * LANE), BF16),
                pltpu.VMEM((1, M), F32),
                pltpu.VMEM((1, M), F32),
                pltpu.VMEM((1, M), F32),
                pltpu.VMEM((LANE, M), F32),
                pltpu.VMEM((2, tkb, M), F32),
                pltpu.VMEM((2, tkb, M), BF16),
                pltpu.VMEM((1, M), F32),
                pltpu.SMEM((1,), jnp.int32),
            ],
        ),
        out_shape=jax.ShapeDtypeStruct((T, BRANCH_W), BF16),
        compiler_params=_cparams(("parallel", "parallel", "parallel", "arbitrary")),
        name="diff_flash",
    )(first, last, p, qtab, kd, ktab, vd, kmax, lam_params, norm_g)


SSD_HALO = 16


def _ssd_prep_kernel(x_ref, xp_ref, xn_ref, bc_ref, bcp_ref, bcn_ref, sm_ref, cw_ref, cb_ref,
                     dtb_ref, arow_ref, xo_ref, bco_ref, dto_ref):
    tc = x_ref.shape[0]
    c = pl.program_id(1)
    nc = pl.num_programs(1)
    has_prev = jnp.where(c > 0, 1.0, 0.0)
    has_next = jnp.where(c < nc - 1, 1.0, 0.0)
    pad = SSD_CONV // 2

    def conv(main_ref, prev_ref, next_ref, off):
        xf = jnp.concatenate([prev_ref[...].astype(F32) * has_prev, main_ref[...].astype(F32),
                              next_ref[...].astype(F32) * has_next], axis=0)
        n = xf.shape[0]
        acc = jnp.zeros((tc, xf.shape[1]), F32) + cb_ref[:, off:off + xf.shape[1]]
        for j in range(SSD_CONV):
            sh = pltpu.roll(xf, (pad - j) % n, axis=0)[SSD_HALO:SSD_HALO + tc, :]
            acc = acc + sh * cw_ref[j:j + 1, off:off + xf.shape[1]]
        return _silu(acc)

    xo_ref[...] = conv(x_ref, xp_ref, xn_ref, 0).astype(BF16)
    bco_ref[...] = conv(bc_ref, bcp_ref, bcn_ref, BRANCH_W).astype(BF16)
    dt = _softplus(sm_ref[...] + dtb_ref[...])
    a = pltpu.roll(dt * arow_ref[...], LANE - 16, axis=1)
    lane = lax.broadcasted_iota(jnp.int32, (tc, LANE), 1)
    dto_ref[...] = jnp.where(lane < 16, a, jnp.where(lane < 32, dt, 0.0))


def _ssd_prep(p, small, conv_w, conv_b, dtb_row, a_row, B, N, tc=512):
    T = B * N
    nc = N // tc
    hb = tc // SSD_HALO
    W = BRANCH_W

    def main(col):
        return pl.BlockSpec((tc, W), lambda b, c: (b * nc + c, col))

    def prev(col):
        return pl.BlockSpec((SSD_HALO, W), lambda b, c: (jnp.maximum((b * nc + c) * hb - 1, 0), col))

    def nxt(col):
        return pl.BlockSpec((SSD_HALO, W), lambda b, c: (jnp.minimum((b * nc + c + 1) * hb, T // SSD_HALO - 1), col))

    cx, cbc = SS_X // W, SS_BC // W
    return pl.pallas_call(
        _ssd_prep_kernel,
        grid=(B, nc),
        in_specs=[
            main(cx), prev(cx), nxt(cx), main(cbc), prev(cbc), nxt(cbc),
            pl.BlockSpec((tc, SMALL_COLS), lambda b, c: (b * nc + c, 0)),
            pl.BlockSpec((SSD_CONV, 2 * W), lambda b, c: (0, 0)),
            pl.BlockSpec((1, 2 * W), lambda b, c: (0, 0)),
            pl.BlockSpec((1, LANE), lambda b, c: (0, 0)),
            pl.BlockSpec((1, LANE), lambda b, c: (0, 0)),
        ],
        out_specs=[
            pl.BlockSpec((tc, W), lambda b, c: (b * nc + c, 0)),
            pl.BlockSpec((tc, W), lambda b, c: (b * nc + c, 0)),
            pl.BlockSpec((tc, LANE), lambda b, c: (b * nc + c, 0)),
        ],
        out_shape=[
            jax.ShapeDtypeStruct((T, W), BF16),
            jax.ShapeDtypeStruct((T, W), BF16),
            jax.ShapeDtypeStruct((T, LANE), F32),
        ],
        compiler_params=_cparams(("parallel", "parallel")),
        name="ssd_prep",
    )(p, p, p, p, p, p, small, conv_w, conv_b, dtb_row, a_row)


def _ssd_kernel(*refs, reverse, final):
    x_ref = refs[0]
    hs_scr = refs[-1]

    @pl.when(pl.program_id(1) == 0)
    def _():
        hs_scr[...] = jnp.zeros_like(hs_scr)

    for bb in range(x_ref.shape[0]):
        _ssd_chunk(bb, refs, reverse, final)


def _ssd_chunk(bb, refs, reverse, final):
    if final:
        (x_ref, bc_ref, dta_ref, ea_ref, ed_ref, z_ref, yb_ref, dsk_ref, ng_ref,
         out_ref, hs_scr) = refs
    else:
        x_ref, bc_ref, dta_ref, ea_ref, ed_ref, out_ref, hs_scr = refs
    L = CHUNK
    S = SSD_STATE
    W = BRANCH_W
    GW = W // SSD_GROUPS
    mask = _tri_masks(L, reverse)
    tri = mask.astype(F32)
    dta = dta_ref[bb]
    acs = _dot_hi(tri, dta)
    acs_t = acs.T
    acs_e = _dot_hi(acs, ea_ref[...])
    dt_e = _dot_hi(dta, ed_ref[...])
    edge = 0 if reverse else L - 1
    a_off = 8 if reverse else 0
    acs_end = acs_e[edge:edge + 1, :]
    xf = x_ref[bb].astype(F32)
    xdt = xf * dt_e
    xdt_b = xdt.astype(BF16)
    xdend = (xdt * jnp.exp(acs_end - acs_e)).astype(BF16)
    e_acs = jnp.exp(acs_e)
    cdec = jnp.exp(acs_end)
    lane = lax.broadcasted_iota(jnp.int32, (L, LANE), 1)
    lo = lane < SSD_HD
    zero = jnp.zeros((L, LANE), BF16)

    ys = []
    for g in range(SSD_GROUPS):
        bg = bc_ref[bb, :, g * S:(g + 1) * S]
        cg = bc_ref[bb, :, SSD_GROUPS * S + g * S:SSD_GROUPS * S + (g + 1) * S]
        cb = _dot_nt(cg, bg)
        hs = hs_scr[bb * SSD_GROUPS + g]
        y_off = _dot(cg, hs.astype(BF16)) * e_acs[:, g * GW:(g + 1) * GW]
        hs_scr[bb * SSD_GROUPS + g] = (cdec[:, g * GW:(g + 1) * GW] * hs
                                       + _dot_tn(bg, xdend[:, g * GW:(g + 1) * GW]))
        for pr in range(GW // LANE):
            col = g * GW + pr * LANE
            xp = xdt_b[:, col:col + LANE]
            yd = jnp.zeros((L, LANE), F32)
            for hh in range(2):
                hd = (col // SSD_HD) + hh
                a_col = acs[:, a_off + hd:a_off + hd + 1]
                a_row = acs_t[a_off + hd:a_off + hd + 1, :]
                decay = jnp.exp(jnp.where(mask, a_col - a_row, -jnp.inf))
                wm = (cb * decay).astype(BF16)
                xm = jnp.where(lo, xp, zero) if hh == 0 else jnp.where(lo, zero, xp)
                yd = yd + _dot(wm, xm)
            ys.append(yd + y_off[:, pr * LANE:(pr + 1) * LANE])
    y = jnp.concatenate(ys, axis=1)

    if not final:
        out_ref[bb] = y
    else:
        y = y + yb_ref[bb] + dsk_ref[...] * xf
        y = y * _silu(z_ref[bb].astype(F32))
        outs = []
        for g in range(SSD_GROUPS):
            yg = y[:, g * GW:(g + 1) * GW]
            ms = jnp.mean(yg * yg, axis=1, keepdims=True)
            outs.append(yg * lax.rsqrt(ms + NORM_EPS))
        out_ref[bb] = (jnp.concatenate(outs, axis=1) * ng_ref[...]).astype(BF16)


def _ssd_dir(xs, bc, dta, ea, ed, B, N, reverse, p3=None, yb=None, dskip_row=None, norm_g=None):
    L = CHUNK
    nc = N // L
    nbat = SCAN_NBAT
    W = BRANCH_W
    final = yb is not None

    def crow(c):
        return (nc - 1 - c) if reverse else c

    in_specs = [
        pl.BlockSpec((nbat, L, W), lambda b, c: (b, crow(c), 0)),
        pl.BlockSpec((nbat, L, W), lambda b, c: (b, crow(c), 0)),
        pl.BlockSpec((nbat, L, LANE), lambda b, c: (b, crow(c), 0)),
        pl.BlockSpec((LANE, W), lambda b, c: (0, 0)),
        pl.BlockSpec((LANE, W), lambda b, c: (0, 0)),
    ]
    args = [xs, bc, dta, ea, ed]
    if final:
        in_specs += [
            pl.BlockSpec((nbat, L, W), lambda b, c: (b, crow(c), SS_Z // W)),
            pl.BlockSpec((nbat, L, W), lambda b, c: (b, crow(c), 0)),
            pl.BlockSpec((1, W), lambda b, c: (0, 0)),
            pl.BlockSpec((1, W), lambda b, c: (0, 0)),
        ]
        args += [p3, yb, dskip_row, norm_g]
    return pl.pallas_call(
        functools.partial(_ssd_kernel, reverse=reverse, final=final),
        grid=(B // nbat, nc),
        in_specs=in_specs,
        out_specs=pl.BlockSpec((nbat, L, W), lambda b, c: (b, crow(c), 0)),
        out_shape=jax.ShapeDtypeStruct((B, N, W), BF16 if final else F32),
        scratch_shapes=[pltpu.VMEM((nbat * SSD_GROUPS, SSD_STATE, W // SSD_GROUPS), F32)],
        compiler_params=_cparams(("parallel", "arbitrary")),
        name="ssd_bwd" if reverse else "ssd_fwd",
    )(*args)


def _ssd_expand_mats():
    ea_f = np.zeros((LANE, BRANCH_W), np.float32)
    ea_b = np.zeros((LANE, BRANCH_W), np.float32)
    ed_f = np.zeros((LANE, BRANCH_W), np.float32)
    ed_b = np.zeros((LANE, BRANCH_W), np.float32)
    for h in range(SSD_HEADS):
        ea_f[h, h * SSD_HD:(h + 1) * SSD_HD] = 1.0
        ea_b[8 + h, h * SSD_HD:(h + 1) * SSD_HD] = 1.0
        ed_f[16 + h, h * SSD_HD:(h + 1) * SSD_HD] = 1.0
        ed_b[24 + h, h * SSD_HD:(h + 1) * SSD_HD] = 1.0
    return ea_f, ea_b, ed_f, ed_b


_EA_F, _EA_B, _ED_F, _ED_B = _ssd_expand_mats()


def _ssd_branch(p, small, conv_w, conv_b, dtb_row, a_row, dskip_row, norm_g, B, N):
    xs, bc, dta = _ssd_prep(p, small, conv_w, conv_b, dtb_row, a_row, B, N)
    xs, bc, dta = (a.reshape(B, N, a.shape[-1]) for a in (xs, bc, dta))
    yb = _ssd_dir(xs, bc, dta, jnp.asarray(_EA_B), jnp.asarray(_ED_B), B, N, reverse=True)
    y = _ssd_dir(xs, bc, dta, jnp.asarray(_EA_F), jnp.asarray(_ED_F), B, N, reverse=False,
                 p3=p.reshape(B, N, P_PAD), yb=yb, dskip_row=dskip_row, norm_g=norm_g)
    return y.reshape(B * N, BRANCH_W)


def _merge_kernel(h_ref, y0_ref, y1_ref, y2_ref, y3_ref, wg_ref, wb_ref, o_ref):
    h = h_ref[...]
    acc = None
    for i, y_ref in enumerate((y0_ref, y1_ref, y2_ref, y3_ref)):
        gate = _sigmoid(_dot(h, wg_ref[i]))
        term = gate * _dot(y_ref[...], wb_ref[i])
        acc = term if acc is None else acc + term
    o_ref[...] = acc.astype(BF16)


def _merge(h, ys, wg, wb, tm=1024, tn=256):
    T = h.shape[0]
    return pl.pallas_call(
        _merge_kernel,
        grid=(T // tm, D_MODEL // tn),
        in_specs=[pl.BlockSpec((tm, D_MODEL), lambda i, j: (i, 0))]
        + [pl.BlockSpec((tm, BRANCH_W), lambda i, j: (i, 0))] * 4
        + [
            pl.BlockSpec((4, D_MODEL, tn), lambda i, j: (0, 0, j)),
            pl.BlockSpec((4, BRANCH_W, tn), lambda i, j: (0, 0, j)),
        ],
        out_specs=pl.BlockSpec((tm, tn), lambda i, j: (i, j)),
        out_shape=jax.ShapeDtypeStruct((T, D_MODEL), BF16),
        compiler_params=_cparams(("parallel", "arbitrary")),
        name="merge",
    )(h, *ys, wg, wb)


def _outproj_kernel(x_ref, m_ref, wo_ref, g_ref, wr_ref, xo_ref, h2_ref, r_ref):
    x = x_ref[...] + _dot(m_ref[...], wo_ref[...])
    xo_ref[...] = x
    ms = jnp.mean(x * x, axis=-1, keepdims=True)
    h2 = x * lax.rsqrt(ms + NORM_EPS) * g_ref[...]
    h2_ref[...] = h2.astype(BF16)
    logits = _dot_hi(h2, wr_ref[...])
    tm = x.shape[0]
    lane = lax.broadcasted_iota(jnp.int32, (tm, LANE), 1).astype(F32)
    neg = -jnp.inf
    big = float(LANE)
    gl = jnp.where(lane < MOE_GROUPS, logits, neg)
    gmax = jnp.max(gl, axis=1, keepdims=True)
    g_idx = jnp.min(jnp.where(gl == gmax, lane, big), axis=1, keepdims=True)
    g_w = 1.0 / jnp.sum(jnp.exp(gl - gmax), axis=1, keepdims=True)
    e_lo = MOE_GROUPS + g_idx * MOE_EPG
    el = jnp.where(lane >= e_lo, jnp.where(lane < e_lo + MOE_EPG, logits, neg), neg)
    m1 = jnp.max(el, axis=1, keepdims=True)
    i1 = jnp.min(jnp.where(el == m1, lane, big), axis=1, keepdims=True)
    el2 = jnp.where(lane == i1, neg, el)
    m2 = jnp.max(el2, axis=1, keepdims=True)
    i2 = jnp.min(jnp.where(el2 == m2, lane, big), axis=1, keepdims=True)
    den = jnp.sum(jnp.exp(el - m1), axis=1, keepdims=True)
    p1 = 1.0 / den
    p2 = jnp.exp(m2 - m1) / den
    w1 = p1 / (p1 + p2) * g_w
    w2 = p2 / (p1 + p2) * g_w
    e1 = i1 - MOE_GROUPS
    e2 = i2 - MOE_GROUPS
    r_ref[...] = jnp.where(lane == 0, e1, jnp.where(lane == 1, e2, jnp.where(lane == 2, w1, jnp.where(lane == 3, w2, 0.0))))


def _out_proj(x, merged, wo, g, wr, tm=512):
    T = x.shape[0]
    return pl.pallas_call(
        _outproj_kernel,
        grid=(T // tm,),
        in_specs=[
            pl.BlockSpec((tm, D_MODEL), lambda i: (i, 0)),
            pl.BlockSpec((tm, D_MODEL), lambda i: (i, 0)),
            pl.BlockSpec((D_MODEL, D_MODEL), lambda i: (0, 0)),
            pl.BlockSpec((1, D_MODEL), lambda i: (0, 0)),
            pl.BlockSpec((D_MODEL, LANE), lambda i: (0, 0)),
        ],
        out_specs=[
            pl.BlockSpec((tm, D_MODEL), lambda i: (i, 0)),
            pl.BlockSpec((tm, D_MODEL), lambda i: (i, 0)),
            pl.BlockSpec((tm, LANE), lambda i: (i, 0)),
        ],
        out_shape=[
            jax.ShapeDtypeStruct((T, D_MODEL), F32),
            jax.ShapeDtypeStruct((T, D_MODEL), BF16),
            jax.ShapeDtypeStruct((T, LANE), F32),
        ],
        compiler_params=_cparams(("parallel",)),
        name="out_proj_router",
    )(x, merged, wo, g, wr)


def _moe_ffn_kernel(be_ref, nu_ref, xs_ref, w1_ref, w3_ref, w2_ref, ys_ref):
    used = pl.program_id(0) < nu_ref[0]

    @pl.when(used)
    def _():
        xb = xs_ref[...]
        a = _dot(xb, w1_ref[0].astype(BF16))
        u = _dot(xb, w3_ref[0].astype(BF16))
        hmid = (_silu(a) * u).astype(BF16)
        ys_ref[...] = _dot(hmid, w2_ref[0].astype(BF16)).astype(BF16)

    @pl.when(jnp.logical_not(used))
    def _():
        ys_ref[...] = jnp.zeros_like(ys_ref)


def _moe_ffn(xs, blk_e, nused, w1, w3, w2):
    R = xs.shape[0]
    nblk = R // MOE_BLK
    return pl.pallas_call(
        _moe_ffn_kernel,
        grid_spec=pltpu.PrefetchScalarGridSpec(
            num_scalar_prefetch=2,
            grid=(nblk,),
            in_specs=[
                pl.BlockSpec((MOE_BLK, D_MODEL), lambda b, be, nu: (b, 0)),
                pl.BlockSpec((1, D_MODEL, MOE_FF), lambda b, be, nu: (be[b], 0, 0)),
                pl.BlockSpec((1, D_MODEL, MOE_FF), lambda b, be, nu: (be[b], 0, 0)),
                pl.BlockSpec((1, MOE_FF, D_MODEL), lambda b, be, nu: (be[b], 0, 0)),
            ],
            out_specs=pl.BlockSpec((MOE_BLK, D_MODEL), lambda b, be, nu: (b, 0)),
        ),
        out_shape=jax.ShapeDtypeStruct((R, D_MODEL), BF16),
        compiler_params=_cparams(("arbitrary",)),
        name="moe_ffn",
    )(blk_e, nused, xs, w1, w3, w2)


def _combine_kernel(x_ref, a_ref, b_ref, r_ref, g_ref, o_ref, *, final):
    r = r_ref[...]
    x = x_ref[...] + (r[:, 2:3] * a_ref[...].astype(F32) + r[:, 3:4] * b_ref[...].astype(F32))
    if final:
        ms = jnp.mean(x * x, axis=-1, keepdims=True)
        x = x * lax.rsqrt(ms + NORM_EPS) * g_ref[...]
    o_ref[...] = x


def _combine(x, a, b, route, g, final, row0, tm=512):
    T = x.shape[0]
    off = row0 // tm
    return pl.pallas_call(
        functools.partial(_combine_kernel, final=final),
        grid=(T // tm,),
        in_specs=[pl.BlockSpec((tm, D_MODEL), lambda i: (i, 0))]
        + [pl.BlockSpec((tm, D_MODEL), lambda i: (i + off, 0))] * 2
        + [pl.BlockSpec((tm, LANE), lambda i: (i + off, 0)), pl.BlockSpec((1, D_MODEL), lambda i: (0, 0))],
        out_specs=pl.BlockSpec((tm, D_MODEL), lambda i: (i, 0)),
        out_shape=jax.ShapeDtypeStruct((T, D_MODEL), F32),
        compiler_params=_cparams(("parallel",)),
        name="moe_combine",
    )(x, a, b, route, g)


def _moe(xs_groups, h2, route, w1, w3, w2, layer, g_final, final):
    T = h2.shape[0]
    K = 2
    e_idx = route[:, 0:K].astype(jnp.int32)
    flat_e = e_idx.reshape(-1)
    onehot = (flat_e[:, None] == jnp.arange(MOE_EXPERTS)[None, :]).astype(jnp.int32)
    csum = jnp.cumsum(onehot, axis=0)
    rank = jnp.take_along_axis(csum, flat_e[:, None], axis=1)[:, 0] - 1
    counts = csum[-1]
    pcounts = (counts + MOE_BLK - 1) // MOE_BLK * MOE_BLK
    pends = jnp.cumsum(pcounts)
    pstarts = pends - pcounts
    dest = pstarts[flat_e] + rank
    R = T * K + MOE_EXPERTS * MOE_BLK
    nblk = R // MOE_BLK
    row_src = jnp.zeros((R,), jnp.int32).at[dest].set(jnp.arange(T * K, dtype=jnp.int32) // K)
    nused = (pends[-1] // MOE_BLK).astype(jnp.int32)
    blk_start = jnp.arange(nblk, dtype=jnp.int32) * MOE_BLK
    blk_e = jnp.minimum(jnp.sum((pends[None, :] <= blk_start[:, None]).astype(jnp.int32), axis=1), MOE_EXPERTS - 1)
    last_e = blk_e[jnp.maximum(nused - 1, 0)]
    blk_e = jnp.where(jnp.arange(nblk) < nused, blk_e, last_e)
    xs = jnp.take(h2, row_src, axis=0, mode="clip")
    ys = _moe_ffn(xs, blk_e + layer * MOE_EXPERTS, nused.reshape(1), w1, w3, w2)
    pos = dest.reshape(T, K)
    ga = jnp.take(ys, pos[:, 0], axis=0, mode="clip")
    gb = jnp.take(ys, pos[:, 1], axis=0, mode="clip")
    outs, row0 = [], 0
    for x in xs_groups:
        outs.append(_combine(x, ga, gb, route, g_final, final, row0))
        row0 += x.shape[0]
    return outs


def _prep_layer(l, w_in, mlstm_gate_bias, ssd_conv_w, ssd_conv_b, ssd_dt_bias, ssd_a_log, ssd_d,
                gqa_q_norm, gqa_k_norm, w_router_group, w_router_expert):
    w = w_in[l]
    wm = jnp.take(w, jnp.asarray(_MAIN_IDX), axis=1)
    dscale = jnp.ones((P_COLS,), F32).at[DF_Q:DF_Q + BRANCH_W].set((DIFF_HD ** -0.5) * LOG2E)
    wm = jnp.pad((wm * dscale[None, :]).astype(BF16), ((0, 0), (0, P_PAD - P_COLS)))
    ws = jnp.zeros((D_MODEL, SMALL_COLS), F32).at[:, :32].set(jnp.take(w, jnp.asarray(_SMALL_IDX), axis=1)).astype(BF16)
    gb_row = jnp.zeros((1, SMALL_COLS), F32).at[0, :16].set(mlstm_gate_bias[l])
    dtb_row = jnp.zeros((1, LANE), F32).at[0, 16:32].set(ssd_dt_bias[l].reshape(-1))
    a_row = jnp.zeros((1, LANE), F32).at[0, 16:32].set((-jnp.exp(ssd_a_log[l])).reshape(-1))
    dskip_row = jnp.repeat(ssd_d[l], SSD_HD)[None, :]
    qg = jnp.tile(gqa_q_norm[l][jnp.asarray(_DEINT64)], LANE // GQA_HD)[None, :]
    kg = jnp.tile(gqa_k_norm[l][jnp.asarray(_DEINT64)], LANE // GQA_HD)[None, :]
    wr = jnp.zeros((D_MODEL, LANE), F32).at[:, :MOE_GROUPS].set(w_router_group[l])
    wr = wr.at[:, MOE_GROUPS:MOE_GROUPS + MOE_EXPERTS].set(w_router_expert[l])
    return dict(wm=wm, ws=ws, gb_row=gb_row, dtb_row=dtb_row, a_row=a_row, dskip_row=dskip_row,
                qg=qg, kg=kg, wr=wr, conv_w=ssd_conv_w[l], conv_b=ssd_conv_b[l][None, :])


def _bd_ones():
    i = np.arange(LANE)
    return (i[:, None] // GQA_HD == i[None, :] // GQA_HD).astype(np.float32)


def _mixer_layer(x, B, N, l, lp, prm, tabs):
    cos_t, sin_t, ktab, qtab = tabs
    bd32 = jnp.asarray(_bd_ones())
    bd = bd32.astype(BF16)
    lambda_init = 0.8 - 0.6 * math.exp(-0.3 * l)
    p, h, small = _in_proj(x, prm["norm_mix"][l][None, :], lp["wm"], lp["ws"])
    y0 = _mlstm_branch(p, small, lp["gb_row"], prm["mlstm_norm"][l][None, :], B, N)
    y1 = _gqa_branch(p, cos_t, sin_t, lp["qg"], lp["kg"], bd, B, N)
    y2 = _ssd_branch(p, small, lp["conv_w"], lp["conv_b"], lp["dtb_row"], lp["a_row"], lp["dskip_row"],
                     prm["ssd_norm"][l][None, :], B, N)
    y3 = _diff_branch(p, ktab, qtab, bd32, prm["diff_lambda"][l], prm["diff_norm"][l][None, :], lambda_init, B, N)
    merged = _merge(h, (y0, y1, y2, y3), lp["wg"], lp["wb"])
    return _out_proj(x, merged, lp["wo"], prm["norm_ffn"][l][None, :], lp["wr"])


def _model(x_groups, prm, L):
    dims = [(x.shape[0], x.shape[1]) for x in x_groups]
    xs = [x.reshape(B * N, D_MODEL) for x, (B, N) in zip(x_groups, dims)]
    tabs = [_rope_tables(N) + _alibi_tables(N) for _, N in dims]
    for l in range(DEPTH):
        mixed = [_mixer_layer(x, B, N, l, L[l], prm, tb) for x, (B, N), tb in zip(xs, dims, tabs)]
        h2cat, rcat = (jnp.concatenate([m[i] for m in mixed], axis=0) for i in (1, 2))
        xs = _moe([m[0] for m in mixed], h2cat, rcat, prm["w1"], prm["w3"], prm["w2"], l,
                  prm["norm_final"][None, :], final=(l == DEPTH - 1))
    return tuple(x.reshape(B, N, D_MODEL) for x, (B, N) in zip(xs, dims))


def kernel(x_prompt, x_sample, norm_mix, w_in, mlstm_gate_bias, mlstm_norm, gqa_q_norm, gqa_k_norm, ssd_conv_w, ssd_conv_b, ssd_dt_bias, ssd_a_log, ssd_d, ssd_norm, diff_lambda, diff_norm, w_branch, w_gate, w_out, norm_ffn, w_router_group, w_router_expert, moe_w_gate, moe_w_up, moe_w_down, norm_final):
    prm = dict(norm_mix=norm_mix, mlstm_norm=mlstm_norm, ssd_norm=ssd_norm, diff_lambda=diff_lambda,
               diff_norm=diff_norm, norm_ffn=norm_ffn, norm_final=norm_final)
    layers = []
    for l in range(DEPTH):
        lp = _prep_layer(l, w_in, mlstm_gate_bias, ssd_conv_w, ssd_conv_b, ssd_dt_bias, ssd_a_log, ssd_d,
                         gqa_q_norm, gqa_k_norm, w_router_group, w_router_expert)
        lp["wg"] = w_gate[l].astype(BF16)
        lp["wb"] = w_branch[l].astype(BF16)
        lp["wo"] = w_out[l].astype(BF16)
        layers.append(lp)
    prm["w1"] = moe_w_gate.reshape(DEPTH * MOE_EXPERTS, D_MODEL, MOE_FF)
    prm["w3"] = moe_w_up.reshape(DEPTH * MOE_EXPERTS, D_MODEL, MOE_FF)
    prm["w2"] = moe_w_down.reshape(DEPTH * MOE_EXPERTS, MOE_FF, D_MODEL)
    return _model([x_prompt, x_sample], prm, layers)
```
